```python
import math
import jax, jax.numpy as jnp
from jax import lax
import numpy as np

D_MODEL = 1024
BATCH = 4
SEQ = 4096
DEPTH = 2
DEC_BATCH = 8
DEC_SEQ = 16
PAST_LEN = 2048

CHUNK = 64
QBLOCK = 128
D_PLE = 256
D_FF = 2816
EPS = 1e-6
N_NORMS = 8
N_EVEN = (DEPTH + 1) // 2
N_ODD = DEPTH // 2

SB_HEADS = 8
SB_HD = 64
DF_HEADS = 4
DF_HD = 64
DF_ROT = DF_HD // 4
ROPE_THETA = 500000.0
MLA_HEADS = 16
MLA_NOPE = 64
MLA_ROPE = 32
MLA_VD = 64
Q_LORA = 384
KV_LORA = 256
MLA_THETA = 10000.0

SB_W = SB_HEADS * SB_HD
DF_QK_W = DF_HEADS * 2 * DF_HD
DF_V_W = DF_HEADS * 2 * DF_HD
AB_IN = 3 * SB_W + 2 * DF_QK_W + DF_V_W
AB_OUT = SB_W + DF_V_W
MLA_QD = MLA_NOPE + MLA_ROPE
MLA_IN = Q_LORA + KV_LORA + MLA_ROPE

kernel_name = 'stickbreak_diff_mla_macaron_stream'


def _rms(x, g):
    x32 = x.astype(jnp.float32)
    y = x32 * lax.rsqrt(jnp.mean(x32 * x32, axis=-1, keepdims=True) + EPS)
    return (y * g.astype(jnp.float32)).astype(x.dtype)


def _rope(x, pos, rot, theta):
    half = rot // 2
    inv = jnp.float32(theta) ** (-(jnp.arange(half, dtype=jnp.float32) * (2.0 / rot)))
    ang = pos.astype(jnp.float32)[:, None] * inv[None, :]
    shape = (1, pos.shape[0]) + (1,) * (x.ndim - 3) + (half,)
    cos = jnp.cos(ang).reshape(shape)
    sin = jnp.sin(ang).reshape(shape)
    xf = x.astype(jnp.float32)
    x1 = xf[..., :half]
    x2 = xf[..., half:rot]
    out = jnp.concatenate([x1 * cos - x2 * sin, x2 * cos + x1 * sin, xf[..., rot:]], axis=-1)
    return out.astype(x.dtype)


def _swiglu(x, w_gu, w_dn):
    g, u = jnp.split(x @ w_gu, 2, axis=-1)
    return (jax.nn.silu(g) * u) @ w_dn


def _sweep(block_fn, q_pos, *qs):
    T = q_pos.shape[0]
    qb = QBLOCK if T % QBLOCK == 0 else T
    nb = T // qb

    def split(a):
        return jnp.moveaxis(a.reshape((a.shape[0], nb, qb) + a.shape[2:]), 1, 0)

    out = lax.map(lambda args: block_fn(*args), (q_pos.reshape(nb, qb),) + tuple(split(a) for a in qs))
    out = jnp.moveaxis(out, 0, 1)
    return out.reshape((out.shape[0], T) + out.shape[3:])


def _chunk_visible(pb, k_pos):
    return (k_pos[None, :] // CHUNK) <= (pb[:, None] // CHUNK)


def _stick_breaking(q, k, v, q_pos, k_pos):
    scale = SB_HD ** -0.5

    def block(pb, qb_):
        z = jnp.einsum('bqhd,bkhd->bhqk', qb_, k).astype(jnp.float32) * scale
        earlier = k_pos[None, :] < pb[:, None]
        log_fail = jnp.where(earlier, jax.nn.log_sigmoid(-z), 0.0)
        between = lax.cumsum(log_fail, axis=3, reverse=True) - log_fail
        w = jnp.where(earlier, jnp.exp(jax.nn.log_sigmoid(z) + between), 0.0)
        return jnp.einsum('bhqk,bkhd->bqhd', w.astype(v.dtype), v)

    return _sweep(block, q_pos, q)


def _diff_attn(q, k, v, q_pos, k_pos, lam):
    scale = DF_HD ** -0.5

    def block(pb, qb_):
        s = jnp.einsum('bqhcd,bkhcd->bhcqk', qb_, k).astype(jnp.float32) * scale
        vis = _chunk_visible(pb, k_pos)
        pr = jax.nn.softmax(jnp.where(vis, s, -jnp.inf), axis=-1)
        w = pr[:, :, 0] - lam * pr[:, :, 1]
        return jnp.einsum('bhqk,bkhe->bqhe', w.astype(v.dtype), v)

    return _sweep(block, q_pos, q)


def _mla_attn(q_nope, q_rope, k_nope, k_rope, v, q_pos, k_pos):
    scale = MLA_QD ** -0.5

    def block(pb, qn, qr):
        s = (jnp.einsum('bqhd,bkhd->bhqk', qn, k_nope).astype(jnp.float32)
             + jnp.einsum('bqhr,bkr->bhqk', qr, k_rope).astype(jnp.float32)) * scale
        pr = jax.nn.softmax(jnp.where(_chunk_visible(pb, k_pos), s, -jnp.inf), axis=-1)
        return jnp.einsum('bhqk,bkhd->bqhd', pr.astype(v.dtype), v)

    return _sweep(block, q_pos, q_nope, q_rope)


def _mixer_ab(h, start, layer, w_in, w_out, lam_vecs, subln, cache):
    B, T, _ = h.shape
    pos = start + jnp.arange(T)
    k_pos = jnp.arange(start + T)
    proj = h @ w_in
    cuts = [SB_W, 2 * SB_W, 3 * SB_W, 3 * SB_W + DF_QK_W, 3 * SB_W + 2 * DF_QK_W]
    sb_q, sb_k, sb_v, df_q, df_k, df_v = jnp.split(proj, cuts, axis=-1)
    sb_q = sb_q.reshape(B, T, SB_HEADS, SB_HD)
    sb_k = sb_k.reshape(B, T, SB_HEADS, SB_HD)
    sb_v = sb_v.reshape(B, T, SB_HEADS, SB_HD)
    df_q = _rope(df_q.reshape(B, T, DF_HEADS, 2, DF_HD), pos, DF_ROT, ROPE_THETA)
    df_k = _rope(df_k.reshape(B, T, DF_HEADS, 2, DF_HD), pos, DF_ROT, ROPE_THETA)
    df_v = df_v.reshape(B, T, DF_HEADS, 2 * DF_HD)
    new = (sb_k, sb_v, df_k, df_v)
    if cache is None:
        ks = new
    else:
        ks = tuple(jnp.concatenate([c, n], axis=1) for c, n in zip(cache, new))
    sb_out = _stick_breaking(sb_q, ks[0], ks[1], pos, k_pos)
    lam_init = 0.8 - 0.6 * math.exp(-0.3 * layer)
    lv = lam_vecs.astype(jnp.float32)
    lam = jnp.exp(jnp.dot(lv[0], lv[1])) - jnp.exp(jnp.dot(lv[2], lv[3])) + lam_init
    df_out = _diff_attn(df_q, ks[2], ks[3], pos, k_pos, lam)
    df_out = _rms(df_out, subln) * (1.0 - lam_init)
    merged = jnp.concatenate([sb_out.reshape(B, T, SB_W), df_out.reshape(B, T, DF_V_W)], axis=-1)
    return merged @ w_out, new


def _mixer_mla(h, start, w_in, q_norm, kv_norm, w_uq, w_uk, w_uv, w_out, cache):
    B, T, _ = h.shape
    pos = start + jnp.arange(T)
    k_pos = jnp.arange(start + T)
    c_q, c_kv, k_r = jnp.split(h @ w_in, [Q_LORA, Q_LORA + KV_LORA], axis=-1)
    c_q = _rms(c_q, q_norm)
    c_kv = _rms(c_kv, kv_norm)
    q = (c_q @ w_uq).reshape(B, T, MLA_HEADS, MLA_QD)
    q_nope = q[..., :MLA_NOPE]
    q_rope = _rope(q[..., MLA_NOPE:], pos, MLA_ROPE, MLA_THETA)
    k_r = _rope(k_r, pos, MLA_ROPE, MLA_THETA)
    new = (c_kv, k_r)
    if cache is None:
        lat, kr = new
    else:
        lat = jnp.concatenate([cache[0], c_kv], axis=1)
        kr = jnp.concatenate([cache[1], k_r], axis=1)
    Tk = lat.shape[1]
    k_nope = (lat @ w_uk).reshape(B, Tk, MLA_HEADS, MLA_NOPE)
    v = (lat @ w_uv).reshape(B, Tk, MLA_HEADS, MLA_VD)
    out = _mla_attn(q_nope, q_rope, k_nope, kr, v, pos, k_pos)
    return out.reshape(B, T, MLA_HEADS * MLA_VD) @ w_out, new


def _trunk(x, p, start, caches, W):
    ab_rows, mla_rows = [], []
    for i in range(DEPTH):
        g = W['norms'][i]
        x = x + 0.5 * _rms(_swiglu(_rms(x, g[0]), W['ffn1_gu'][i], W['ffn1_dn'][i]), g[1])
        h = _rms(x, g[2])
        j = i // 2
        if i % 2 == 0:
            c = None if caches is None else tuple(a[j] for a in caches[:4])
            m, rows = _mixer_ab(h, start, i, W['ab_in'][j], W['ab_out'][j], W['df_lambda'][j], W['df_subln'][j], c)
            ab_rows.append(rows)
        else:
            c = None if caches is None else tuple(a[j] for a in caches[4:])
            m, rows = _mixer_mla(h, start, W['mla_in'][j], W['mla_q_norm'][j], W['mla_kv_norm'][j],
                                 W['mla_uq'][j], W['mla_uk'][j], W['mla_uv'][j], W['mla_out'][j], c)
            mla_rows.append(rows)
        x = x + _rms(m, g[3])
        x = x + 0.5 * _rms(_swiglu(_rms(x, g[4]), W['ffn2_gu'][i], W['ffn2_dn'][i]), g[5])
        gate = jax.nn.sigmoid(_rms(x, g[6]) @ W['ple_gate'][i])
        x = x + _rms(gate * (p[i] @ W['ple_in'][i]), g[7])
    sb_k, sb_v, df_k, df_v = [jnp.stack(r, axis=0) for r in zip(*ab_rows)]
    lat, kr = [jnp.stack(r, axis=0) for r in zip(*mla_rows)]
    return x, sb_k, sb_v, df_k, df_v, lat, kr


def setup_inputs(seed: int = 0) -> dict:
    key = jax.random.key(seed)
    ks = jax.random.split(key, 32)
    f32 = jnp.float32

    def w(k, shape, fan_in):
        return jax.random.normal(k, shape, f32) * (fan_in ** -0.5)

    def gain(k, shape):
        return 1.0 + 0.05 * jax.random.normal(k, shape, f32)

    return {
        'x_prompt': jax.random.normal(ks[0], (BATCH, SEQ, D_MODEL), f32),
        'x_sample': jax.random.normal(ks[1], (DEC_BATCH, DEC_SEQ, D_MODEL), f32),
        'p_prompt': jax.random.normal(ks[2], (DEPTH, BATCH, SEQ, D_PLE), f32),
        'p_sample': jax.random.normal(ks[3], (DEPTH, DEC_BATCH, DEC_SEQ, D_PLE), f32),
        'cache_sb_k': jax.random.normal(ks[4], (N_EVEN, DEC_BATCH, PAST_LEN, SB_HEADS, SB_HD), f32),
        'cache_sb_v': jax.random.normal(ks[5], (N_EVEN, DEC_BATCH, PAST_LEN, SB_HEADS, SB_HD), f32),
        'cache_df_k': jax.random.normal(ks[6], (N_EVEN, DEC_BATCH, PAST_LEN, DF_HEADS, 2, DF_HD), f32),
        'cache_df_v': jax.random.normal(ks[7], (N_EVEN, DEC_BATCH, PAST_LEN, DF_HEADS, 2 * DF_HD), f32),
        'cache_mla_latent': jax.random.normal(ks[8], (N_ODD, DEC_BATCH, PAST_LEN, KV_LORA), f32),
        'cache_mla_krope': jax.random.normal(ks[9], (N_ODD, DEC_BATCH, PAST_LEN, MLA_ROPE), f32),
        'norms': gain(ks[10], (DEPTH, N_NORMS, D_MODEL)),
        'ffn1_gu': w(ks[11], (DEPTH, D_MODEL, 2 * D_FF), D_MODEL),
        'ffn1_dn': w(ks[12], (DEPTH, D_FF, D_MODEL), D_FF),
        'ffn2_gu': w(ks[13], (DEPTH, D_MODEL, 2 * D_FF), D_MODEL),
        'ffn2_dn': w(ks[14], (DEPTH, D_FF, D_MODEL), D_FF),
        'ple_in': w(ks[15], (DEPTH, D_PLE, D_MODEL), D_PLE),
        'ple_gate': w(ks[16], (DEPTH, D_MODEL, D_MODEL), D_MODEL),
        'ab_in': w(ks[17], (N_EVEN, D_MODEL, AB_IN), D_MODEL),
        'ab_out': w(ks[18], (N_EVEN, AB_OUT, D_MODEL), AB_OUT),
        'df_lambda': 0.1 * jax.random.normal(ks[19], (N_EVEN, 4, DF_HD), f32),
        'df_subln': gain(ks[20], (N_EVEN, 2 * DF_HD)),
        'mla_in': w(ks[21], (N_ODD, D_MODEL, MLA_IN), D_MODEL),
        'mla_q_norm': gain(ks[22], (N_ODD, Q_LORA)),
        'mla_kv_norm': gain(ks[23], (N_ODD, KV_LORA)),
        'mla_uq': w(ks[24], (N_ODD, Q_LORA, MLA_HEADS * MLA_QD), Q_LORA),
        'mla_uk': w(ks[25], (N_ODD, KV_LORA, MLA_HEADS * MLA_NOPE), KV_LORA),
        'mla_uv': w(ks[26], (N_ODD, KV_LORA, MLA_HEADS * MLA_VD), KV_LORA),
        'mla_out': w(ks[27], (N_ODD, MLA_HEADS * MLA_VD, D_MODEL), MLA_HEADS * MLA_VD),
    }


def reference(x_prompt, x_sample, p_prompt, p_sample, cache_sb_k, cache_sb_v, cache_df_k, cache_df_v,
              cache_mla_latent, cache_mla_krope, norms, ffn1_gu, ffn1_dn, ffn2_gu, ffn2_dn, ple_in, ple_gate,
              ab_in, ab_out, df_lambda, df_subln, mla_in, mla_q_norm, mla_kv_norm, mla_uq, mla_uk, mla_uv, mla_out):
    W = dict(norms=norms, ffn1_gu=ffn1_gu, ffn1_dn=ffn1_dn, ffn2_gu=ffn2_gu, ffn2_dn=ffn2_dn,
             ple_in=ple_in, ple_gate=ple_gate, ab_in=ab_in, ab_out=ab_out, df_lambda=df_lambda,
             df_subln=df_subln, mla_in=mla_in, mla_q_norm=mla_q_norm, mla_kv_norm=mla_kv_norm,
             mla_uq=mla_uq, mla_uk=mla_uk, mla_uv=mla_uv, mla_out=mla_out)
    (y_prompt, sb_k_p, sb_v_p, df_k_p, df_v_p, lat_p, kr_p) = _trunk(x_prompt, p_prompt, 0, None, W)
    caches = (cache_sb_k, cache_sb_v, cache_df_k, cache_df_v, cache_mla_latent, cache_mla_krope)
    past = cache_sb_k.shape[2]
    (y_sample, sb_k_s, sb_v_s, df_k_s, df_v_s, lat_s, kr_s) = _trunk(x_sample, p_sample, past, caches, W)
    return (y_prompt, y_sample, sb_k_p, sb_v_p, df_k_p, df_v_p, lat_p, kr_p,
            sb_k_s, sb_v_s, df_k_s, df_v_s, lat_s, kr_s)
```

```python
import functools
import math

import jax
import jax.numpy as jnp
from jax import lax
from jax.experimental import pallas as pl
from jax.experimental.pallas import tpu as pltpu

F32 = jnp.float32
BF16 = jnp.bfloat16

EPS = 1e-6
CHUNK = 64
SB_HEADS, SB_HD = 8, 64
DF_HEADS, DF_HD = 4, 64
DF_ROT = DF_HD // 4
ROPE_THETA = 500000.0
MLA_HEADS, MLA_NOPE, MLA_ROPE, MLA_VD = 16, 64, 32, 64
Q_LORA, KV_LORA = 384, 256
MLA_THETA = 10000.0
MLA_QD = MLA_NOPE + MLA_ROPE

LANES = 128
SB_W = SB_HEADS * SB_HD
DF_W = DF_HEADS * 2 * DF_HD
MLA_QW = MLA_HEADS * LANES
MLA_VW = MLA_HEADS * MLA_VD
MLA_IN_PAD = 768

VMEM_LIMIT = 56 * 1024 * 1024


def _cparams(sem):
    return pltpu.CompilerParams(dimension_semantics=sem, vmem_limit_bytes=VMEM_LIMIT)


def _rms(x, g):
    return x * lax.rsqrt(jnp.mean(x * x, axis=-1, keepdims=True) + EPS) * g


def _dot(a, b):
    return jnp.dot(a, b, preferred_element_type=F32)


def _dot_t(a, b):
    return lax.dot_general(a, b, (((1,), (1,)), ((), ())), preferred_element_type=F32)


def _rope_slab(x, cos, sin_up, sin_dn, shift):
    n = x.shape[-1]
    return x * cos + pltpu.roll(x, n - shift, 1) * sin_up + pltpu.roll(x, shift, 1) * sin_dn


def _ffn_kernel(x_ref, gin_ref, gout_ref, wg_ref, wu_ref, wd_ref, o_ref, h_ref, acc_ref):
    j = pl.program_id(1)

    @pl.when(j == 0)
    def _():
        h_ref[...] = _rms(x_ref[...], gin_ref[...]).astype(BF16)
        acc_ref[...] = jnp.zeros_like(acc_ref)

    h = h_ref[...]
    g = _dot(h, wg_ref[...])
    u = _dot(h, wu_ref[...])
    a = (g * jax.nn.sigmoid(g) * u).astype(BF16)
    acc_ref[...] += _dot(a, wd_ref[...])

    @pl.when(j == pl.num_programs(1) - 1)
    def _():
        o_ref[...] = x_ref[...] + 0.5 * _rms(acc_ref[...], gout_ref[...])


def _ffn_block(x, g_in, g_out, w_gu, w_dn, tm, tf):
    m, d = x.shape
    d_ff = w_dn.shape[0]
    nf = d_ff // tf
    row = lambda i, j: (i, 0)
    const = lambda i, j: (0, 0)
    return pl.pallas_call(
        _ffn_kernel,
        grid=(m // tm, nf),
        in_specs=[
            pl.BlockSpec((tm, d), row),
            pl.BlockSpec((1, d), const),
            pl.BlockSpec((1, d), const),
            pl.BlockSpec((d, tf), lambda i, j: (0, j)),
            pl.BlockSpec((d, tf), lambda i, j: (0, j + nf)),
            pl.BlockSpec((tf, d), lambda i, j: (j, 0)),
        ],
        out_specs=pl.BlockSpec((tm, d), row),
        out_shape=jax.ShapeDtypeStruct((m, d), F32),
        scratch_shapes=[pltpu.VMEM((tm, d), BF16), pltpu.VMEM((tm, d), F32)],
        compiler_params=_cparams(("parallel", "arbitrary")),
        name="ffn_block",
    )(x, g_in, g_out, w_gu, w_gu, w_dn)


def _ab_in_kernel(x_ref, g_ref, w_ref, cos_ref, sup_ref, sdn_ref,
                  sbq_ref, sbk_ref, sbv_ref, dfq_ref, dfk_ref, dfv_ref,
                  sbk_f_ref, sbv_f_ref, dfk_f_ref, dfv_f_ref):
    h = _rms(x_ref[...], g_ref[...]).astype(BF16)
    cos, sup, sdn = cos_ref[...], sup_ref[...], sdn_ref[...]

    def proj(c):
        return _dot(h, w_ref[:, c * SB_W:(c + 1) * SB_W])

    def rope(x, scale, out_b, out_f):
        for s in range(x.shape[-1] // LANES):
            sl = slice(s * LANES, (s + 1) * LANES)
            r = _rope_slab(x[:, sl], cos, sup, sdn, DF_ROT // 2)
            if out_f is not None:
                out_f[:, sl] = r
            out_b[:, sl] = (r * scale).astype(BF16)

    sbq_ref[...] = (proj(0) * (SB_HD ** -0.5)).astype(BF16)
    sbk = proj(1)
    sbk_f_ref[...] = sbk
    sbk_ref[...] = sbk.astype(BF16)
    sbv = proj(2)
    sbv_f_ref[...] = sbv
    sbv_ref[...] = sbv.astype(BF16)
    rope(proj(3), DF_HD ** -0.5, dfq_ref, None)
    rope(proj(4), 1.0, dfk_ref, dfk_f_ref)
    dfv = proj(5)
    dfv_f_ref[...] = dfv
    dfv_ref[...] = dfv.astype(BF16)


def _ab_in_proj(x, g, w_in, tabs, tm):
    m, d = x.shape
    n_tab = tabs[0].shape[0] // tm
    row = lambda i: (i, 0)
    const = lambda i: (0, 0)
    tab = lambda i: (i % n_tab, 0)
    wide = pl.BlockSpec((tm, SB_W), row)
    bf = jax.ShapeDtypeStruct((m, SB_W), BF16)
    f32 = jax.ShapeDtypeStruct((m, SB_W), F32)
    return pl.pallas_call(
        _ab_in_kernel,
        grid=(m // tm,),
        in_specs=[
            pl.BlockSpec((tm, d), row),
            pl.BlockSpec((1, d), const),
            pl.BlockSpec(w_in.shape, const),
            pl.BlockSpec((tm, LANES), tab),
            pl.BlockSpec((tm, LANES), tab),
            pl.BlockSpec((tm, LANES), tab),
        ],
        out_specs=[wide] * 10,
        out_shape=[bf] * 6 + [f32] * 4,
        compiler_params=_cparams(("parallel",)),
        name="ab_in_proj",
    )(x, g, w_in, *tabs)


def _out_proj_kernel(*refs, n_parts):
    x_ref, g_ref, w_ref = refs[0], refs[1], refs[2]
    parts = refs[3:3 + n_parts]
    o_ref = refs[3 + n_parts]
    off = 0
    m = None
    for p in parts:
        k = p.shape[-1]
        t = _dot(p[...], w_ref[off:off + k, :])
        m = t if m is None else m + t
        off += k
    o_ref[...] = x_ref[...] + _rms(m, g_ref[...])


def _out_proj(x, g, w_out, parts, tm):
    m, d = x.shape
    row = lambda i: (i, 0)
    const = lambda i: (0, 0)
    return pl.pallas_call(
        functools.partial(_out_proj_kernel, n_parts=len(parts)),
        grid=(m // tm,),
        in_specs=[
            pl.BlockSpec((tm, d), row),
            pl.BlockSpec((1, d), const),
            pl.BlockSpec(w_out.shape, const),
        ] + [pl.BlockSpec((tm, p.shape[-1]), row) for p in parts],
        out_specs=pl.BlockSpec((tm, d), row),
        out_shape=jax.ShapeDtypeStruct((m, d), F32),
        compiler_params=_cparams(("parallel",)),
        name="out_proj",
    )(x, g, w_out, *parts)


def _ple_kernel(x_ref, p_ref, gg_ref, go_ref, wg_ref, wp_ref, o_ref):
    x = x_ref[...]
    gate = jax.nn.sigmoid(_dot(_rms(x, gg_ref[...]).astype(BF16), wg_ref[...]))
    emb = _dot(p_ref[...].astype(BF16), wp_ref[...])
    o_ref[...] = x + _rms(gate * emb, go_ref[...])


def _ple_block(x, p, g_gate, g_out, w_gate, w_in, tm):
    m, d = x.shape
    row = lambda i: (i, 0)
    const = lambda i: (0, 0)
    return pl.pallas_call(
        _ple_kernel,
        grid=(m // tm,),
        in_specs=[
            pl.BlockSpec((tm, d), row),
            pl.BlockSpec((tm, p.shape[-1]), row),
            pl.BlockSpec((1, d), const),
            pl.BlockSpec((1, d), const),
            pl.BlockSpec(w_gate.shape, const),
            pl.BlockSpec(w_in.shape, const),
        ],
        out_specs=pl.BlockSpec((tm, d), row),
        out_shape=jax.ShapeDtypeStruct((m, d), F32),
        compiler_params=_cparams(("parallel",)),
        name="ple_block",
    )(x, p, g_gate, g_out, w_gate, w_in)


def _mla_in_kernel(x_ref, g_ref, w_ref, qn_ref, kvn_ref, wuq_ref,
                   qcos_ref, qsup_ref, qsdn_ref, kcos_ref, ksup_ref, ksdn_ref,
                   q_ref, lat_ref, kr_ref):
    h = _rms(x_ref[...], g_ref[...]).astype(BF16)
    proj = _dot(h, w_ref[...])
    c_q = _rms(proj[:, :Q_LORA], qn_ref[...]).astype(BF16)
    lat_ref[...] = _rms(proj[:, Q_LORA:Q_LORA + KV_LORA], kvn_ref[...])
    kr = _rope_slab(proj[:, Q_LORA + KV_LORA:], kcos_ref[...], ksup_ref[...], ksdn_ref[...],
                    MLA_ROPE // 2)
    kr_ref[...] = kr[:, :MLA_ROPE]
    q = _dot(c_q, wuq_ref[...])
    cos, sup, sdn = qcos_ref[...], qsup_ref[...], qsdn_ref[...]
    for hd in range(MLA_HEADS):
        sl = slice(hd * LANES, (hd + 1) * LANES)
        r = _rope_slab(q[:, sl], cos, sup, sdn, MLA_ROPE // 2)
        q_ref[:, sl] = (r * (MLA_QD ** -0.5)).astype(BF16)


def _mla_in_proj(x, g, w_in, q_norm, kv_norm, w_uq, qtabs, ktabs, tm):
    m, d = x.shape
    n_tab = qtabs[0].shape[0] // tm
    row = lambda i: (i, 0)
    const = lambda i: (0, 0)
    tab = lambda i: (i % n_tab, 0)
    tspec = pl.BlockSpec((tm, LANES), tab)
    return pl.pallas_call(
        _mla_in_kernel,
        grid=(m // tm,),
        in_specs=[
            pl.BlockSpec((tm, d), row),
            pl.BlockSpec((1, d), const),
            pl.BlockSpec(w_in.shape, const),
            pl.BlockSpec((1, Q_LORA), const),
            pl.BlockSpec((1, KV_LORA), const),
            pl.BlockSpec(w_uq.shape, const),
        ] + [tspec] * 6,
        out_specs=[
            pl.BlockSpec((tm, MLA_QW), row),
            pl.BlockSpec((tm, KV_LORA), row),
            pl.BlockSpec((tm, MLA_ROPE), row),
        ],
        out_shape=[
            jax.ShapeDtypeStruct((m, MLA_QW), BF16),
            jax.ShapeDtypeStruct((m, KV_LORA), F32),
            jax.ShapeDtypeStruct((m, MLA_ROPE), F32),
        ],
        compiler_params=_cparams(("parallel",)),
        name="mla_in_proj",
    )(x, g, w_in, q_norm, kv_norm, w_uq, *qtabs, *ktabs)


def _mla_kv_kernel(lat_ref, kr_ref, wuk_ref, place_ref, wuv_ref, k_ref, v_ref):
    lat = lat_ref[...].astype(BF16)
    k_ref[...] = (_dot(lat, wuk_ref[...]) + _dot(kr_ref[...].astype(BF16), place_ref[...])).astype(BF16)
    v_ref[...] = _dot(lat, wuv_ref[...]).astype(BF16)


def _mla_kv_expand(lat, kr, w_uk, place, w_uv, tm):
    m = lat.shape[0]
    row = lambda i: (i, 0)
    const = lambda i: (0, 0)
    return pl.pallas_call(
        _mla_kv_kernel,
        grid=(m // tm,),
        in_specs=[
            pl.BlockSpec((tm, KV_LORA), row),
            pl.BlockSpec((tm, MLA_ROPE), row),
            pl.BlockSpec(w_uk.shape, const),
            pl.BlockSpec(place.shape, const),
            pl.BlockSpec(w_uv.shape, const),
        ],
        out_specs=[pl.BlockSpec((tm, MLA_QW), row), pl.BlockSpec((tm, MLA_VW), row)],
        out_shape=[jax.ShapeDtypeStruct((m, MLA_QW), BF16), jax.ShapeDtypeStruct((m, MLA_VW), BF16)],
        compiler_params=_cparams(("parallel",)),
        name="mla_kv_expand",
    )(lat, kr, w_uk, place, w_uv)


def _positions(q0, kb, tq, tk):
    t = q0 + lax.broadcasted_iota(jnp.int32, (tq, tk), 0)
    s = kb * tk + lax.broadcasted_iota(jnp.int32, (tq, tk), 1)
    return t, s


def _sb_kernel(q_ref, k_ref, v_ref, o_ref, *, tq, tk, q_start):
    q0 = q_start + pl.program_id(2) * tq
    n_full = q0 // tk
    q = q_ref[0]
    lane = lax.broadcasted_iota(jnp.int32, (tq, LANES), 1)
    q_heads = (jnp.where(lane < SB_HD, q, jnp.zeros_like(q)),
               jnp.where(lane >= SB_HD, q, jnp.zeros_like(q)))
    later = (lax.broadcasted_iota(jnp.int32, (tk, tk), 0)
             > lax.broadcasted_iota(jnp.int32, (tk, tk), 1)).astype(BF16)

    def block(kb, state, masked):
        start = pl.multiple_of(kb * tk, tk)
        ks = k_ref[0, pl.ds(start, tk), :]
        vs = v_ref[0, pl.ds(start, tk), :]
        if masked:
            t, s = _positions(q0, kb, tq, tk)
            earlier = s < t
        new = []
        for qh, (run, acc) in zip(q_heads, state):
            z = _dot_t(qh, ks)
            soft = jnp.log1p(jnp.exp(-jnp.abs(z)))
            log_fail = -(jnp.maximum(z, 0.0) + soft)
            log_hit = jnp.minimum(z, 0.0) - soft
            if masked:
                log_fail = jnp.where(earlier, log_fail, 0.0)
            hi = log_fail.astype(BF16)
            lo = (log_fail - hi.astype(F32)).astype(BF16)
            inside = _dot(hi, later) + _dot(lo, later)
            w = jnp.exp(log_hit + inside + run)
            if masked:
                w = jnp.where(earlier, w, 0.0)
            acc = acc + _dot(w.astype(BF16), vs)
            run = run + inside[:, 0:1] + log_fail[:, 0:1]
            new.append((run, acc))
        return tuple(new)

    zero = (jnp.zeros((tq, 1), F32), jnp.zeros((tq, LANES), F32))
    state = block(n_full, (zero, zero), True)
    state = lax.fori_loop(0, n_full, lambda i, st: block(n_full - 1 - i, st, False), state)
    o_ref[0] = jnp.where(lane < SB_HD, state[0][1], state[1][1]).astype(o_ref.dtype)


def _softmax_step(s, vs, m, l, acc):
    m_new = jnp.maximum(m, jnp.max(s, axis=-1, keepdims=True))
    alpha = jnp.exp(m - m_new)
    p = jnp.exp(s - m_new)
    l = alpha * l + jnp.sum(p, axis=-1, keepdims=True)
    acc = alpha * acc + _dot(p.astype(BF16), vs)
    return m_new, l, acc


def _visible(q0, kb, tq, tk, n_keys):
    t, s = _positions(q0, kb, tq, tk)
    return ((s // CHUNK) <= (t // CHUNK)) & (s < n_keys)


def _softmax_init(tq):
    return (jnp.full((tq, 1), -jnp.inf, F32), jnp.zeros((tq, 1), F32), jnp.zeros((tq, LANES), F32))


def _df_kernel(lam_ref, sub_ref, q_ref, k_ref, v_ref, o_ref, *, tq, tk, q_start, n_keys, lam_init):
    q0 = q_start + pl.program_id(2) * tq
    n_full = q0 // tk
    q = q_ref[0]
    lane = lax.broadcasted_iota(jnp.int32, (tq, LANES), 1)
    q_parts = (jnp.where(lane < DF_HD, q, jnp.zeros_like(q)),
               jnp.where(lane >= DF_HD, q, jnp.zeros_like(q)))

    def block(kb, state, masked):
        start = pl.multiple_of(kb * tk, tk)
        ks = k_ref[0, pl.ds(start, tk), :]
        vs = v_ref[0, pl.ds(start, tk), :]
        if masked:
            vis = _visible(q0, kb, tq, tk, n_keys)
        new = []
        for qp, st in zip(q_parts, state):
            s = _dot_t(qp, ks)
            if masked:
                s = jnp.where(vis, s, -jnp.inf)
            new.append(_softmax_step(s, vs, *st))
        return tuple(new)

    state = (_softmax_init(tq), _softmax_init(tq))
    state = lax.fori_loop(0, n_full, lambda kb, st: block(kb, st, False), state)
    state = block(n_full, state, True)

    lv = lam_ref[...]
    lam = (jnp.exp(jnp.sum(lv[0:1] * lv[1:2], axis=-1, keepdims=True))
           - jnp.exp(jnp.sum(lv[2:3] * lv[3:4], axis=-1, keepdims=True)) + lam_init)
    (_, l0, a0), (_, l1, a1) = state
    out = a0 / l0 - lam * (a1 / l1)
    o_ref[0] = (_rms(out, sub_ref[...]) * (1.0 - lam_init)).astype(o_ref.dtype)


def _mla_kernel(q_ref, k_ref, v_ref, o_ref, *, tq, tk, q_start, n_keys):
    q0 = q_start + pl.program_id(2) * tq
    n_full = q0 // tk
    q = q_ref[0]
    q_heads = (q[:, :LANES], q[:, LANES:])
    lane = lax.broadcasted_iota(jnp.int32, (tq, LANES), 1)

    def block(kb, state, masked):
        start = pl.multiple_of(kb * tk, tk)
        ks = k_ref[0, pl.ds(start, tk), :]
        vs = v_ref[0, pl.ds(start, tk), :]
        if masked:
            vis = _visible(q0, kb, tq, tk, n_keys)
        new = []
        for i, (qh, st) in enumerate(zip(q_heads, state)):
            s = _dot_t(qh, ks[:, i * LANES:(i + 1) * LANES])
            if masked:
                s = jnp.where(vis, s, -jnp.inf)
            new.append(_softmax_step(s, vs, *st))
        return tuple(new)

    state = (_softmax_init(tq), _softmax_init(tq))
    state = lax.fori_loop(0, n_full, lambda kb, st: block(kb, st, False), state)
    state = block(n_full, state, True)
    (_, l0, a0), (_, l1, a1) = state
    o_ref[0] = jnp.where(lane < MLA_VD, a0 / l0, a1 / l1).astype(o_ref.dtype)


def _attention(kernel_fn, q, k, v, extra, *, q_lanes, k_lanes, tq, tk, name):
    b, t_q, _ = q.shape
    t_k = k.shape[1]
    groups = v.shape[-1] // LANES
    assert t_q % tq == 0 and t_k % tk == 0 and tk % tq == 0 and tk % CHUNK == 0
    const = lambda bi, g, qi: (0, 0)
    return pl.pallas_call(
        kernel_fn,
        grid=(b, groups, t_q // tq),
        in_specs=[pl.BlockSpec(e.shape, const) for e in extra] + [
            pl.BlockSpec((1, tq, q_lanes), lambda bi, g, qi: (bi, qi, g)),
            pl.BlockSpec((1, t_k, k_lanes), lambda bi, g, qi: (bi, 0, g)),
            pl.BlockSpec((1, t_k, LANES), lambda bi, g, qi: (bi, 0, g)),
        ],
        out_specs=pl.BlockSpec((1, tq, LANES), lambda bi, g, qi: (bi, qi, g)),
        out_shape=jax.ShapeDtypeStruct((b, t_q, groups * LANES), BF16),
        compiler_params=_cparams(("parallel", "parallel", "arbitrary")),
        name=name,
    )(*extra, q, k, v)


def _rope_tables(pos, rot, theta, period, offset):
    half = rot // 2
    inv = jnp.float32(theta) ** (-(jnp.arange(half, dtype=F32) * (2.0 / rot)))
    ang = pos.astype(F32)[:, None] * inv[None, :]
    cos, sin = jnp.cos(ang), jnp.sin(ang)
    lane = jnp.arange(LANES) % period - offset
    first = (lane >= 0) & (lane < half)
    second = (lane >= half) & (lane < rot)
    idx = jnp.clip(jnp.where(second, lane - half, lane), 0, half - 1)
    cos_l, sin_l = cos[:, idx], sin[:, idx]
    cos_t = jnp.where(first | second, cos_l, 1.0)
    sin_up = jnp.where(first, -sin_l, 0.0)
    sin_dn = jnp.where(second, sin_l, 0.0)
    return cos_t, sin_up, sin_dn


def _pad_heads(w, heads, width):
    k = w.shape[0]
    w = w.reshape(k, heads, width)
    return jnp.pad(w, ((0, 0), (0, 0), (0, LANES - width))).reshape(k, heads * LANES)


def _with_cache(cache, new, t_pad, dtype):
    b, past, w = cache.shape
    pad = jnp.zeros((b, t_pad - past - new.shape[1], w), dtype)
    return jnp.concatenate([cache.astype(dtype), new.astype(dtype), pad], axis=1)


def _trunk(x, p, start, caches, W, *, tm, tf, tq, tk):
    b, t, d = x.shape
    m = b * t
    depth = p.shape[0]
    x = x.reshape(m, d)
    pos = start + jnp.arange(t)
    if t % tm == 0:
        tab_pos = pos
    else:
        tab_pos = jnp.tile(pos, m // t)
    df_tabs = _rope_tables(tab_pos, DF_ROT, ROPE_THETA, DF_HD, 0)
    mq_tabs = _rope_tables(tab_pos, MLA_ROPE, MLA_THETA, LANES, MLA_NOPE)
    mk_tabs = _rope_tables(tab_pos, MLA_ROPE, MLA_THETA, LANES, 0)
    n_keys = start + t
    t_keys = -(-n_keys // tk) * tk

    ab_rows, mla_rows = [], []
    for i in range(depth):
        g = W['norms'][i][:, None, :]
        j = i // 2
        x = _ffn_block(x, g[0], g[1], W['ffn1_gu'][i], W['ffn1_dn'][i], tm, tf)
        if i % 2 == 0:
            (sbq, sbk, sbv, dfq, dfk, dfv, sbk_f, sbv_f, dfk_f, dfv_f) = _ab_in_proj(
                x, g[2], W['ab_in'][j], df_tabs, tm)
            ab_rows.append((sbk_f.reshape(b, t, SB_HEADS, SB_HD), sbv_f.reshape(b, t, SB_HEADS, SB_HD),
                            dfk_f.reshape(b, t, DF_HEADS, 2, DF_HD), dfv_f.reshape(b, t, DF_HEADS, 2 * DF_HD)))
            three = lambda a: a.reshape(b, t, a.shape[-1])
            if caches is None:
                keys = [three(a) for a in (sbk, sbv, dfk, dfv)]
            else:
                keys = [_with_cache(c[j].reshape(b, start, -1), three(n), t_keys, BF16)
                        for c, n in zip(caches[:4], (sbk_f, sbv_f, dfk_f, dfv_f))]
            sb_out = _attention(
                functools.partial(_sb_kernel, tq=tq, tk=tk, q_start=start),
                three(sbq), keys[0], keys[1], [], q_lanes=LANES, k_lanes=LANES, tq=tq, tk=tk, name="sb_attn")
            lam_init = 0.8 - 0.6 * math.exp(-0.3 * i)
            df_out = _attention(
                functools.partial(_df_kernel, tq=tq, tk=tk, q_start=start, n_keys=n_keys, lam_init=lam_init),
                three(dfq), keys[2], keys[3], [W['df_lambda'][j], W['df_subln'][j][None, :]],
                q_lanes=LANES, k_lanes=LANES, tq=tq, tk=tk, name="df_attn")
            parts = [sb_out.reshape(m, SB_W), df_out.reshape(m, DF_W)]
            w_out = W['ab_out'][j]
        else:
            q, lat, kr = _mla_in_proj(x, g[2], W['mla_in'][j], W['mla_q_norm'][j][None, :],
                                      W['mla_kv_norm'][j][None, :], W['mla_uq'][j], mq_tabs, mk_tabs, tm)
            mla_rows.append((lat.reshape(b, t, KV_LORA), kr.reshape(b, t, MLA_ROPE)))
            if caches is None:
                lat_all, kr_all = lat, kr
            else:
                lat_all = _with_cache(caches[4][j], lat.reshape(b, t, KV_LORA), t_keys, F32).reshape(-1, KV_LORA)
                kr_all = _with_cache(caches[5][j], kr.reshape(b, t, MLA_ROPE), t_keys, F32).reshape(-1, MLA_ROPE)
            mk = lat_all.shape[0]
            k_all, v_all = _mla_kv_expand(lat_all, kr_all, W['mla_uk'][j], W['mla_place'], W['mla_uv'][j],
                                          min(tm, mk))
            out = _attention(
                functools.partial(_mla_kernel, tq=tq, tk=tk, q_start=start, n_keys=n_keys),
                q.reshape(b, t, MLA_QW), k_all.reshape(b, -1, MLA_QW), v_all.reshape(b, -1, MLA_VW), [],
                q_lanes=2 * LANES, k_lanes=2 * LANES, tq=tq, tk=tk, name="mla_attn")
            parts = [out.reshape(m, MLA_VW)]
            w_out = W['mla_out'][j]
        x = _out_proj(x, g[3], w_out, parts, tm)
        x = _ffn_block(x, g[4], g[5], W['ffn2_gu'][i], W['ffn2_dn'][i], tm, tf)
        x = _ple_block(x, p[i].reshape(m, -1), g[6], g[7], W['ple_gate'][i], W['ple_in'][i], tm)
    sb_k, sb_v, df_k, df_v = [jnp.stack(r, axis=0) for r in zip(*ab_rows)]
    lat, kr = [jnp.stack(r, axis=0) for r in zip(*mla_rows)]
    return x.reshape(b, t, d), sb_k, sb_v, df_k, df_v, lat, kr


def _pick(n, candidates):
    for c in candidates:
        if n % c == 0:
            return c
    return n


def kernel(x_prompt, x_sample, p_prompt, p_sample, cache_sb_k, cache_sb_v, cache_df_k, cache_df_v,
           cache_mla_latent, cache_mla_krope, norms, ffn1_gu, ffn1_dn, ffn2_gu, ffn2_dn, ple_in, ple_gate,
           ab_in, ab_out, df_lambda, df_subln, mla_in, mla_q_norm, mla_kv_norm, mla_uq, mla_uk, mla_uv, mla_out):
    bf = lambda a: a.astype(BF16)
    n_odd = mla_in.shape[0]
    place = jnp.zeros((MLA_ROPE, MLA_HEADS, LANES), F32)
    place = place.at[jnp.arange(MLA_ROPE), :, MLA_NOPE + jnp.arange(MLA_ROPE)].set(1.0)
    W = dict(
        norms=norms, ffn1_gu=bf(ffn1_gu), ffn1_dn=bf(ffn1_dn), ffn2_gu=bf(ffn2_gu), ffn2_dn=bf(ffn2_dn),
        ple_in=bf(ple_in), ple_gate=bf(ple_gate), ab_in=bf(ab_in), ab_out=bf(ab_out),
        df_lambda=df_lambda, df_subln=df_subln,
        mla_in=bf(jnp.pad(mla_in, ((0, 0), (0, 0), (0, MLA_IN_PAD - mla_in.shape[-1])))),
        mla_q_norm=mla_q_norm, mla_kv_norm=mla_kv_norm,
        mla_uq=bf(jnp.stack([_pad_heads(mla_uq[j], MLA_HEADS, MLA_QD) for j in range(n_odd)])),
        mla_uk=bf(jnp.stack([_pad_heads(mla_uk[j], MLA_HEADS, MLA_NOPE) for j in range(n_odd)])),
        mla_uv=bf(mla_uv), mla_out=bf(mla_out),
        mla_place=bf(place.reshape(MLA_ROPE, MLA_QW)),
    )
    d_ff = ffn1_dn.shape[1]
    tf = _pick(d_ff, (1408, 256, 128))

    t_p = x_prompt.shape[1]
    m_p = x_prompt.shape[0] * t_p
    tq_p = _pick(t_p, (256, 128, 64))
    out_p = _trunk(x_prompt, p_prompt, 0, None, W, tm=_pick(m_p, (512, 256, 128)), tf=tf, tq=tq_p, tk=tq_p)

    caches = (cache_sb_k, cache_sb_v, cache_df_k, cache_df_v, cache_mla_latent, cache_mla_krope)
    past = cache_sb_k.shape[2]
    t_s = x_sample.shape[1]
    m_s = x_sample.shape[0] * t_s
    out_s = _trunk(x_sample, p_sample, past, caches, W, tm=_pick(m_s, (128,)), tf=tf, tq=t_s,
                   tk=_pick(past, (256, 128, 64)))

    (y_p, sb_k_p, sb_v_p, df_k_p, df_v_p, lat_p, kr_p) = out_p
    (y_s, sb_k_s, sb_v_s, df_k_s, df_v_s, lat_s, kr_s) = out_s
    return (y_p, y_s, sb_k_p, sb_v_p, df_k_p, df_v_p, lat_p, kr_p,
            sb_k_s, sb_v_s, df_k_s, df_v_s, lat_s, kr_s)
```

```python
import functools
import math

import jax
import jax.numpy as jnp
from jax import lax
from jax.experimental import pallas as pl
from jax.experimental.pallas import tpu as pltpu

F32 = jnp.float32
BF16 = jnp.bfloat16

EPS = 1e-6
CHUNK = 64
SB_HEADS, SB_HD = 8, 64
DF_HEADS, DF_HD = 4, 64
DF_ROT = DF_HD // 4
ROPE_THETA = 500000.0
MLA_HEADS, MLA_NOPE, MLA_ROPE, MLA_VD = 16, 64, 32, 64
Q_LORA, KV_LORA = 384, 256
MLA_THETA = 10000.0
MLA_QD = MLA_NOPE + MLA_ROPE
LOG2E = math.log2(math.e)
SB_UNDERFLOW = 151.0

LANES = 128
SB_W = SB_HEADS * SB_HD
DF_W = DF_HEADS * 2 * DF_HD
MLA_QW = MLA_HEADS * LANES
MLA_VW = MLA_HEADS * MLA_VD
MLA_IN_PAD = 768

VMEM_LIMIT = 56 * 1024 * 1024


def _cparams(sem):
    return pltpu.CompilerParams(dimension_semantics=sem, vmem_limit_bytes=VMEM_LIMIT)


def _rms(x, g):
    return x * lax.rsqrt(jnp.mean(x * x, axis=-1, keepdims=True) + EPS) * g


def _dot(a, b):
    return jnp.dot(a, b, preferred_element_type=F32)


def _dot_t(a, b):
    return lax.dot_general(a, b, (((1,), (1,)), ((), ())), preferred_element_type=F32)


def _rope_slab(x, cos, sin_up, sin_dn, shift):
    n = x.shape[-1]
    return x * cos + pltpu.roll(x, n - shift, 1) * sin_up + pltpu.roll(x, shift, 1) * sin_dn


def _ffn_kernel(x_ref, gin_ref, gout_ref, wg_ref, wu_ref, wd_ref, o_ref, h_ref, acc_ref):
    j = pl.program_id(1)

    @pl.when(j == 0)
    def _():
        h_ref[...] = _rms(x_ref[...], gin_ref[...]).astype(BF16)
        acc_ref[...] = jnp.zeros_like(acc_ref)

    h = h_ref[...]
    g = _dot(h, wg_ref[...])
    u = _dot(h, wu_ref[...])
    a = (g * jax.nn.sigmoid(g) * u).astype(BF16)
    acc_ref[...] += _dot(a, wd_ref[...])

    @pl.when(j == pl.num_programs(1) - 1)
    def _():
        o_ref[...] = x_ref[...] + 0.5 * _rms(acc_ref[...], gout_ref[...])


def _ffn_block(x, g_in, g_out, w_gu, w_dn, tm, tf):
    m, d = x.shape
    d_ff = w_dn.shape[0]
    nf = d_ff // tf
    row = lambda i, j: (i, 0)
    const = lambda i, j: (0, 0)
    return pl.pallas_call(
        _ffn_kernel,
        grid=(m // tm, nf),
        in_specs=[
            pl.BlockSpec((tm, d), row),
            pl.BlockSpec((1, d), const),
            pl.BlockSpec((1, d), const),
            pl.BlockSpec((d, tf), lambda i, j: (0, j)),
            pl.BlockSpec((d, tf), lambda i, j: (0, j + nf)),
            pl.BlockSpec((tf, d), lambda i, j: (j, 0)),
        ],
        out_specs=pl.BlockSpec((tm, d), row),
        out_shape=jax.ShapeDtypeStruct((m, d), F32),
        scratch_shapes=[pltpu.VMEM((tm, d), BF16), pltpu.VMEM((tm, d), F32)],
        compiler_params=_cparams(("parallel", "arbitrary")),
        name="ffn_block",
    )(x, g_in, g_out, w_gu, w_gu, w_dn)


def _ab_in_kernel(x_ref, g_ref, w_ref, cos_ref, sup_ref, sdn_ref,
                  sbq_ref, sbk_ref, sbv_ref, dfq_ref, dfk_ref, dfv_ref,
                  sbk_f_ref, sbv_f_ref, dfk_f_ref, dfv_f_ref):
    h = _rms(x_ref[...], g_ref[...]).astype(BF16)
    cos, sup, sdn = cos_ref[...], sup_ref[...], sdn_ref[...]

    def proj(c):
        return _dot(h, w_ref[:, c * SB_W:(c + 1) * SB_W])

    def rope(x, scale, out_b, out_f):
        for s in range(x.shape[-1] // LANES):
            sl = slice(s * LANES, (s + 1) * LANES)
            r = _rope_slab(x[:, sl], cos, sup, sdn, DF_ROT // 2)
            if out_f is not None:
                out_f[:, sl] = r
            out_b[:, sl] = (r * scale).astype(BF16)

    sbq_ref[...] = (proj(0) * (SB_HD ** -0.5 * LOG2E)).astype(BF16)
    sbk = proj(1)
    sbk_f_ref[...] = sbk
    sbk_ref[...] = sbk.astype(BF16)
    sbv = proj(2)
    sbv_f_ref[...] = sbv
    sbv_ref[...] = sbv.astype(BF16)
    rope(proj(3), DF_HD ** -0.5 * LOG2E, dfq_ref, None)
    rope(proj(4), 1.0, dfk_ref, dfk_f_ref)
    dfv = proj(5)
    dfv_f_ref[...] = dfv
    dfv_ref[...] = dfv.astype(BF16)


def _ab_in_proj(x, g, w_in, tabs, tm):
    m, d = x.shape
    n_tab = tabs[0].shape[0] // tm
    row = lambda i: (i, 0)
    const = lambda i: (0, 0)
    tab = lambda i: (i % n_tab, 0)
    wide = pl.BlockSpec((tm, SB_W), row)
    bf = jax.ShapeDtypeStruct((m, SB_W), BF16)
    f32 = jax.ShapeDtypeStruct((m, SB_W), F32)
    return pl.pallas_call(
        _ab_in_kernel,
        grid=(m // tm,),
        in_specs=[
            pl.BlockSpec((tm, d), row),
            pl.BlockSpec((1, d), const),
            pl.BlockSpec(w_in.shape, const),
            pl.BlockSpec((tm, LANES), tab),
            pl.BlockSpec((tm, LANES), tab),
            pl.BlockSpec((tm, LANES), tab),
        ],
        out_specs=[wide] * 10,
        out_shape=[bf] * 6 + [f32] * 4,
        compiler_params=_cparams(("parallel",)),
        name="ab_in_proj",
    )(x, g, w_in, *tabs)


def _out_proj_kernel(*refs, n_parts):
    x_ref, g_ref, w_ref = refs[0], refs[1], refs[2]
    parts = refs[3:3 + n_parts]
    o_ref = refs[3 + n_parts]
    off = 0
    m = None
    for p in parts:
        k = p.shape[-1]
        t = _dot(p[...], w_ref[off:off + k, :])
        m = t if m is None else m + t
        off += k
    o_ref[...] = x_ref[...] + _rms(m, g_ref[...])


def _out_proj(x, g, w_out, parts, tm):
    m, d = x.shape
    row = lambda i: (i, 0)
    const = lambda i: (0, 0)
    return pl.pallas_call(
        functools.partial(_out_proj_kernel, n_parts=len(parts)),
        grid=(m // tm,),
        in_specs=[
            pl.BlockSpec((tm, d), row),
            pl.BlockSpec((1, d), const),
            pl.BlockSpec(w_out.shape, const),
        ] + [pl.BlockSpec((tm, p.shape[-1]), row) for p in parts],
        out_specs=pl.BlockSpec((tm, d), row),
        out_shape=jax.ShapeDtypeStruct((m, d), F32),
        compiler_params=_cparams(("parallel",)),
        name="out_proj",
    )(x, g, w_out, *parts)


def _ple_kernel(x_ref, p_ref, gg_ref, go_ref, wg_ref, wp_ref, o_ref):
    x = x_ref[...]
    gate = jax.nn.sigmoid(_dot(_rms(x, gg_ref[...]).astype(BF16), wg_ref[...]))
    emb = _dot(p_ref[...].astype(BF16), wp_ref[...])
    o_ref[...] = x + _rms(gate * emb, go_ref[...])


def _ple_block(x, p, g_gate, g_out, w_gate, w_in, tm):
    m, d = x.shape
    row = lambda i: (i, 0)
    const = lambda i: (0, 0)
    return pl.pallas_call(
        _ple_kernel,
        grid=(m // tm,),
        in_specs=[
            pl.BlockSpec((tm, d), row),
            pl.BlockSpec((tm, p.shape[-1]), row),
            pl.BlockSpec((1, d), const),
            pl.BlockSpec((1, d), const),
            pl.BlockSpec(w_gate.shape, const),
            pl.BlockSpec(w_in.shape, const),
        ],
        out_specs=pl.BlockSpec((tm, d), row),
        out_shape=jax.ShapeDtypeStruct((m, d), F32),
        compiler_params=_cparams(("parallel",)),
        name="ple_block",
    )(x, p, g_gate, g_out, w_gate, w_in)


def _mla_in_kernel(x_ref, g_ref, w_ref, qn_ref, kvn_ref, wuq_ref,
                   qcos_ref, qsup_ref, qsdn_ref, kcos_ref, ksup_ref, ksdn_ref,
                   q_ref, lat_ref, kr_ref):
    h = _rms(x_ref[...], g_ref[...]).astype(BF16)
    proj = _dot(h, w_ref[...])
    c_q = _rms(proj[:, :Q_LORA], qn_ref[...]).astype(BF16)
    lat_ref[...] = _rms(proj[:, Q_LORA:Q_LORA + KV_LORA], kvn_ref[...])
    kr = _rope_slab(proj[:, Q_LORA + KV_LORA:], kcos_ref[...], ksup_ref[...], ksdn_ref[...],
                    MLA_ROPE // 2)
    kr_ref[...] = kr[:, :MLA_ROPE]
    q = _dot(c_q, wuq_ref[...])
    cos, sup, sdn = qcos_ref[...], qsup_ref[...], qsdn_ref[...]
    for hd in range(MLA_HEADS):
        sl = slice(hd * LANES, (hd + 1) * LANES)
        r = _rope_slab(q[:, sl], cos, sup, sdn, MLA_ROPE // 2)
        q_ref[:, sl] = (r * (MLA_QD ** -0.5 * LOG2E)).astype(BF16)


def _mla_in_proj(x, g, w_in, q_norm, kv_norm, w_uq, qtabs, ktabs, tm):
    m, d = x.shape
    n_tab = qtabs[0].shape[0] // tm
    row = lambda i: (i, 0)
    const = lambda i: (0, 0)
    tab = lambda i: (i % n_tab, 0)
    tspec = pl.BlockSpec((tm, LANES), tab)
    return pl.pallas_call(
        _mla_in_kernel,
        grid=(m // tm,),
        in_specs=[
            pl.BlockSpec((tm, d), row),
            pl.BlockSpec((1, d), const),
            pl.BlockSpec(w_in.shape, const),
            pl.BlockSpec((1, Q_LORA), const),
            pl.BlockSpec((1, KV_LORA), const),
            pl.BlockSpec(w_uq.shape, const),
        ] + [tspec] * 6,
        out_specs=[
            pl.BlockSpec((tm, MLA_QW), row),
            pl.BlockSpec((tm, KV_LORA), row),
            pl.BlockSpec((tm, MLA_ROPE), row),
        ],
        out_shape=[
            jax.ShapeDtypeStruct((m, MLA_QW), BF16),
            jax.ShapeDtypeStruct((m, KV_LORA), F32),
            jax.ShapeDtypeStruct((m, MLA_ROPE), F32),
        ],
        compiler_params=_cparams(("parallel",)),
        name="mla_in_proj",
    )(x, g, w_in, q_norm, kv_norm, w_uq, *qtabs, *ktabs)


def _mla_kv_kernel(lat_ref, kr_ref, wuk_ref, place_ref, wuv_ref, k_ref, v_ref):
    lat = lat_ref[...].astype(BF16)
    k_ref[...] = (_dot(lat, wuk_ref[...]) + _dot(kr_ref[...].astype(BF16), place_ref[...])).astype(BF16)
    v_ref[...] = _dot(lat, wuv_ref[...]).astype(BF16)


def _mla_kv_expand(lat, kr, w_uk, place, w_uv, tm):
    m = lat.shape[0]
    row = lambda i: (i, 0)
    const = lambda i: (0, 0)
    return pl.pallas_call(
        _mla_kv_kernel,
        grid=(m // tm,),
        in_specs=[
            pl.BlockSpec((tm, KV_LORA), row),
            pl.BlockSpec((tm, MLA_ROPE), row),
            pl.BlockSpec(w_uk.shape, const),
            pl.BlockSpec(place.shape, const),
            pl.BlockSpec(w_uv.shape, const),
        ],
        out_specs=[pl.BlockSpec((tm, MLA_QW), row), pl.BlockSpec((tm, MLA_VW), row)],
        out_shape=[jax.ShapeDtypeStruct((m, MLA_QW), BF16), jax.ShapeDtypeStruct((m, MLA_VW), BF16)],
        compiler_params=_cparams(("parallel",)),
        name="mla_kv_expand",
    )(lat, kr, w_uk, place, w_uv)


def _positions(q0, kb, tq, tk):
    t = q0 + lax.broadcasted_iota(jnp.int32, (tq, tk), 0)
    s = kb * tk + lax.broadcasted_iota(jnp.int32, (tq, tk), 1)
    return t, s


def _sb_kernel(q_ref, k_ref, v_ref, o_ref, *, tq, tk, q_start):
    q0 = q_start + pl.program_id(2) * tq
    n_full = q0 // tk
    q = q_ref[0]
    lane = lax.broadcasted_iota(jnp.int32, (tq, LANES), 1)
    q_heads = (jnp.where(lane < SB_HD, q, jnp.zeros_like(q)),
               jnp.where(lane >= SB_HD, q, jnp.zeros_like(q)))
    later = (lax.broadcasted_iota(jnp.int32, (tk, tk), 0)
             > lax.broadcasted_iota(jnp.int32, (tk, tk), 1)).astype(BF16)

    def load(kb):
        start = pl.multiple_of(kb * tk, tk)
        return k_ref[0, pl.ds(start, tk), :], v_ref[0, pl.ds(start, tk), :]

    def miss_cost(z):
        cost = jnp.maximum(z, 0.0) + jnp.log2(1.0 + jnp.exp2(-jnp.abs(z)))
        return cost, z - cost

    def cost_after(cost):
        hi = cost.astype(BF16)
        lo = (cost - hi.astype(F32)).astype(BF16)
        return _dot(hi, later) + _dot(lo, later)

    def block_total(cost, after):
        return after[:, 0:1] + cost[:, 0:1]

    lower = jnp.maximum(n_full - 1, 0)
    (k_lo, v_lo), (k_up, v_up) = load(lower), load(lower + 1)
    t, s_lo = _positions(q0, lower, tq, tk)
    earlier_lo = s_lo < t
    earlier_up = s_lo + tk < t
    state = []
    for qh in q_heads:
        cost_up, hit_up = miss_cost(_dot_t(qh, k_up))
        cost_lo, hit_lo = miss_cost(_dot_t(qh, k_lo))
        cost_up = jnp.where(earlier_up, cost_up, 0.0)
        cost_lo = jnp.where(earlier_lo, cost_lo, 0.0)
        after_up, after_lo = cost_after(cost_up), cost_after(cost_lo)
        run_up = block_total(cost_up, after_up)
        w_up = jnp.where(earlier_up, jnp.exp2(hit_up - after_up), 0.0)
        w_lo = jnp.where(earlier_lo, jnp.exp2(hit_lo - after_lo - run_up), 0.0)
        acc = _dot(w_up.astype(BF16), v_up) + _dot(w_lo.astype(BF16), v_lo)
        state.append((run_up + block_total(cost_lo, after_lo), acc))

    def least_run(st):
        return jnp.min(jnp.minimum(st[0][0], st[1][0]))

    def more(c):
        kb, least, _ = c
        return (kb >= 0) & (least < SB_UNDERFLOW)

    def block(c):
        kb, _, st = c
        ks, vs = load(kb)
        new = []
        for qh, (run, acc) in zip(q_heads, st):
            cost, hit = miss_cost(_dot_t(qh, ks))
            after = cost_after(cost)
            acc = acc + _dot(jnp.exp2(hit - after - run).astype(BF16), vs)
            new.append((run + block_total(cost, after), acc))
        return kb - 1, least_run(new), tuple(new)

    _, _, state = lax.while_loop(more, block, (lower - 1, least_run(state), tuple(state)))
    o_ref[0] = jnp.where(lane < SB_HD, state[0][1], state[1][1]).astype(o_ref.dtype)


def _visible(q0, kb, tq, tk, n_keys):
    t, s = _positions(q0, kb, tq, tk)
    return ((s // CHUNK) <= (t // CHUNK)) & (s < n_keys)


def _softmax_sweep(score_fns, v_ref, s_refs, q0, *, tq, tk, n_keys):
    n_full = q0 // tk
    n_chain = len(score_fns)
    s_even, s_odd = s_refs

    def issue(kb, buf):
        row_max = []
        for c, f in enumerate(score_fns):
            s = f(pl.multiple_of(kb * tk, tk))
            buf[c] = s
            row_max.append(jnp.max(s, axis=-1, keepdims=True))
        return tuple(row_max)

    def reduce(kb, buf, row_max, state, masked):
        vs = v_ref[0, pl.ds(pl.multiple_of(kb * tk, tk), tk), :]
        if masked:
            vis = _visible(q0, kb, tq, tk, n_keys)
        new = []
        for c, (m, l, acc) in enumerate(state):
            s = buf[c]
            if masked:
                s = jnp.where(vis, s, -jnp.inf)
                blk_max = jnp.max(s, axis=-1, keepdims=True)
            else:
                blk_max = row_max[c]
            m_new = jnp.maximum(m, blk_max)
            alpha = jnp.exp2(m - m_new)
            p = jnp.exp2(s - m_new)
            l = alpha * l + jnp.sum(p, axis=-1, keepdims=True)
            acc = alpha * acc + _dot(p.astype(BF16), vs)
            new.append((m_new, l, acc))
        return tuple(new)

    def pair(j, carry):
        row_max, state = carry
        odd_max = issue(2 * j + 1, s_odd)
        state = reduce(2 * j, s_even, row_max, state, False)
        row_max = issue(2 * j + 2, s_even)
        return row_max, reduce(2 * j + 1, s_odd, odd_max, state, False)

    init = (jnp.full((tq, 1), -jnp.inf, F32), jnp.zeros((tq, 1), F32), jnp.zeros((tq, LANES), F32))
    row_max, state = lax.fori_loop(0, n_full // 2, pair, (issue(0, s_even), (init,) * n_chain))

    def last_is_even(state):
        return reduce(n_full, s_even, None, state, True)

    def last_is_odd(state):
        issue(n_full, s_odd)
        state = reduce(n_full - 1, s_even, row_max, state, False)
        return reduce(n_full, s_odd, None, state, True)

    return lax.cond(n_full % 2 == 1, last_is_odd, last_is_even, state)


def _df_kernel(lam_ref, sub_ref, q_ref, k_ref, v_ref, o_ref, *s_refs, tq, tk, q_start, n_keys, lam_init):
    q0 = q_start + pl.program_id(2) * tq
    q = q_ref[0]
    lane = lax.broadcasted_iota(jnp.int32, (tq, LANES), 1)
    q_parts = (jnp.where(lane < DF_HD, q, jnp.zeros_like(q)),
               jnp.where(lane >= DF_HD, q, jnp.zeros_like(q)))
    score_fns = [lambda start, qp=qp: _dot_t(qp, k_ref[0, pl.ds(start, tk), :]) for qp in q_parts]
    (_, l0, a0), (_, l1, a1) = _softmax_sweep(score_fns, v_ref, s_refs, q0, tq=tq, tk=tk, n_keys=n_keys)

    lv = lam_ref[...]
    lam = (jnp.exp(jnp.sum(lv[0:1] * lv[1:2], axis=-1, keepdims=True))
           - jnp.exp(jnp.sum(lv[2:3] * lv[3:4], axis=-1, keepdims=True)) + lam_init)
    out = a0 / l0 - lam * (a1 / l1)
    o_ref[0] = (_rms(out, sub_ref[...]) * (1.0 - lam_init)).astype(o_ref.dtype)


def _mla_kernel(q_ref, k_ref, v_ref, o_ref, *s_refs, tq, tk, q_start, n_keys):
    q0 = q_start + pl.program_id(2) * tq
    q = q_ref[0]
    lane = lax.broadcasted_iota(jnp.int32, (tq, LANES), 1)
    score_fns = [
        lambda start, i=i: _dot_t(q[:, i * LANES:(i + 1) * LANES],
                                  k_ref[0, pl.ds(start, tk), i * LANES:(i + 1) * LANES])
        for i in range(2)]
    (_, l0, a0), (_, l1, a1) = _softmax_sweep(score_fns, v_ref, s_refs, q0, tq=tq, tk=tk, n_keys=n_keys)
    o_ref[0] = jnp.where(lane < MLA_VD, a0 / l0, a1 / l1).astype(o_ref.dtype)


def _attention(kernel_fn, q, k, v, extra, *, q_lanes, k_lanes, tq, tk, name, score_chains=0):
    b, t_q, _ = q.shape
    t_k = k.shape[1]
    groups = v.shape[-1] // LANES
    assert t_q % tq == 0 and t_k % tk == 0 and tk % tq == 0 and tk % CHUNK == 0
    const = lambda bi, g, qi: (0, 0)
    return pl.pallas_call(
        kernel_fn,
        grid=(b, groups, t_q // tq),
        in_specs=[pl.BlockSpec(e.shape, const) for e in extra] + [
            pl.BlockSpec((1, tq, q_lanes), lambda bi, g, qi: (bi, qi, g)),
            pl.BlockSpec((1, t_k, k_lanes), lambda bi, g, qi: (bi, 0, g)),
            pl.BlockSpec((1, t_k, LANES), lambda bi, g, qi: (bi, 0, g)),
        ],
        out_specs=pl.BlockSpec((1, tq, LANES), lambda bi, g, qi: (bi, qi, g)),
        out_shape=jax.ShapeDtypeStruct((b, t_q, groups * LANES), BF16),
        scratch_shapes=[pltpu.VMEM((score_chains, tq, tk), F32)] * 2 if score_chains else [],
        compiler_params=_cparams(("parallel", "parallel", "arbitrary")),
        name=name,
    )(*extra, q, k, v)


def _rope_tables(pos, rot, theta, period, offset):
    half = rot // 2
    inv = jnp.float32(theta) ** (-(jnp.arange(half, dtype=F32) * (2.0 / rot)))
    ang = pos.astype(F32)[:, None] * inv[None, :]
    cos, sin = jnp.cos(ang), jnp.sin(ang)
    lane = jnp.arange(LANES) % period - offset
    first = (lane >= 0) & (lane < half)
    second = (lane >= half) & (lane < rot)
    idx = jnp.clip(jnp.where(second, lane - half, lane), 0, half - 1)
    cos_l, sin_l = cos[:, idx], sin[:, idx]
    cos_t = jnp.where(first | second, cos_l, 1.0)
    sin_up = jnp.where(first, -sin_l, 0.0)
    sin_dn = jnp.where(second, sin_l, 0.0)
    return cos_t, sin_up, sin_dn


def _pad_heads(w, heads, width):
    k = w.shape[0]
    w = w.reshape(k, heads, width)
    return jnp.pad(w, ((0, 0), (0, 0), (0, LANES - width))).reshape(k, heads * LANES)


def _with_cache(cache, new, t_pad, dtype):
    b, past, w = cache.shape
    pad = jnp.zeros((b, t_pad - past - new.shape[1], w), dtype)
    return jnp.concatenate([cache.astype(dtype), new.astype(dtype), pad], axis=1)


def _trunk(x, p, start, caches, W, *, tm, tf, tq, tk, sb_tq, sb_tk):
    b, t, d = x.shape
    m = b * t
    depth = p.shape[0]
    x = x.reshape(m, d)
    pos = start + jnp.arange(t)
    if t % tm == 0:
        tab_pos = pos
    else:
        tab_pos = jnp.tile(pos, m // t)
    df_tabs = _rope_tables(tab_pos, DF_ROT, ROPE_THETA, DF_HD, 0)
    mq_tabs = _rope_tables(tab_pos, MLA_ROPE, MLA_THETA, LANES, MLA_NOPE)
    mk_tabs = _rope_tables(tab_pos, MLA_ROPE, MLA_THETA, LANES, 0)
    n_keys = start + t
    t_keys = -(-n_keys // tk) * tk

    ab_rows, mla_rows = [], []
    for i in range(depth):
        g = W['norms'][i][:, None, :]
        j = i // 2
        x = _ffn_block(x, g[0], g[1], W['ffn1_gu'][i], W['ffn1_dn'][i], tm, tf)
        if i % 2 == 0:
            (sbq, sbk, sbv, dfq, dfk, dfv, sbk_f, sbv_f, dfk_f, dfv_f) = _ab_in_proj(
                x, g[2], W['ab_in'][j], df_tabs, tm)
            ab_rows.append((sbk_f.reshape(b, t, SB_HEADS, SB_HD), sbv_f.reshape(b, t, SB_HEADS, SB_HD),
                            dfk_f.reshape(b, t, DF_HEADS, 2, DF_HD), dfv_f.reshape(b, t, DF_HEADS, 2 * DF_HD)))
            three = lambda a: a.reshape(b, t, a.shape[-1])
            if caches is None:
                keys = [three(a) for a in (sbk, sbv, dfk, dfv)]
            else:
                keys = [_with_cache(c[j].reshape(b, start, -1), three(n), t_keys, BF16)
                        for c, n in zip(caches[:4], (sbk_f, sbv_f, dfk_f, dfv_f))]
            sb_out = _attention(
                functools.partial(_sb_kernel, tq=sb_tq, tk=sb_tk, q_start=start),
                three(sbq), keys[0], keys[1], [], q_lanes=LANES, k_lanes=LANES, tq=sb_tq, tk=sb_tk,
                name="sb_attn")
            lam_init = 0.8 - 0.6 * math.exp(-0.3 * i)
            df_out = _attention(
                functools.partial(_df_kernel, tq=tq, tk=tk, q_start=start, n_keys=n_keys, lam_init=lam_init),
                three(dfq), keys[2], keys[3], [W['df_lambda'][j], W['df_subln'][j][None, :]],
                q_lanes=LANES, k_lanes=LANES, tq=tq, tk=tk, name="df_attn", score_chains=2)
            parts = [sb_out.reshape(m, SB_W), df_out.reshape(m, DF_W)]
            w_out = W['ab_out'][j]
        else:
            q, lat, kr = _mla_in_proj(x, g[2], W['mla_in'][j], W['mla_q_norm'][j][None, :],
                                      W['mla_kv_norm'][j][None, :], W['mla_uq'][j], mq_tabs, mk_tabs, tm)
            mla_rows.append((lat.reshape(b, t, KV_LORA), kr.reshape(b, t, MLA_ROPE)))
            if caches is None:
                lat_all, kr_all = lat, kr
            else:
                lat_all = _with_cache(caches[4][j], lat.reshape(b, t, KV_LORA), t_keys, F32).reshape(-1, KV_LORA)
                kr_all = _with_cache(caches[5][j], kr.reshape(b, t, MLA_ROPE), t_keys, F32).reshape(-1, MLA_ROPE)
            mk = lat_all.shape[0]
            k_all, v_all = _mla_kv_expand(lat_all, kr_all, W['mla_uk'][j], W['mla_place'], W['mla_uv'][j],
                                          min(tm, mk))
            out = _attention(
                functools.partial(_mla_kernel, tq=tq, tk=tk, q_start=start, n_keys=n_keys),
                q.reshape(b, t, MLA_QW), k_all.reshape(b, -1, MLA_QW), v_all.reshape(b, -1, MLA_VW), [],
                q_lanes=2 * LANES, k_lanes=2 * LANES, tq=tq, tk=tk, name="mla_attn", score_chains=2)
            parts = [out.reshape(m, MLA_VW)]
            w_out = W['mla_out'][j]
        x = _out_proj(x, g[3], w_out, parts, tm)
        x = _ffn_block(x, g[4], g[5], W['ffn2_gu'][i], W['ffn2_dn'][i], tm, tf)
        x = _ple_block(x, p[i].reshape(m, -1), g[6], g[7], W['ple_gate'][i], W['ple_in'][i], tm)
    sb_k, sb_v, df_k, df_v = [jnp.stack(r, axis=0) for r in zip(*ab_rows)]
    lat, kr = [jnp.stack(r, axis=0) for r in zip(*mla_rows)]
    return x.reshape(b, t, d), sb_k, sb_v, df_k, df_v, lat, kr


def _pick(n, candidates):
    for c in candidates:
        if n % c == 0:
            return c
    return n


def kernel(x_prompt, x_sample, p_prompt, p_sample, cache_sb_k, cache_sb_v, cache_df_k, cache_df_v,
           cache_mla_latent, cache_mla_krope, norms, ffn1_gu, ffn1_dn, ffn2_gu, ffn2_dn, ple_in, ple_gate,
           ab_in, ab_out, df_lambda, df_subln, mla_in, mla_q_norm, mla_kv_norm, mla_uq, mla_uk, mla_uv, mla_out):
    bf = lambda a: a.astype(BF16)
    n_odd = mla_in.shape[0]
    place = jnp.zeros((MLA_ROPE, MLA_HEADS, LANES), F32)
    place = place.at[jnp.arange(MLA_ROPE), :, MLA_NOPE + jnp.arange(MLA_ROPE)].set(1.0)
    W = dict(
        norms=norms, ffn1_gu=bf(ffn1_gu), ffn1_dn=bf(ffn1_dn), ffn2_gu=bf(ffn2_gu), ffn2_dn=bf(ffn2_dn),
        ple_in=bf(ple_in), ple_gate=bf(ple_gate), ab_in=bf(ab_in), ab_out=bf(ab_out),
        df_lambda=df_lambda, df_subln=df_subln,
        mla_in=bf(jnp.pad(mla_in, ((0, 0), (0, 0), (0, MLA_IN_PAD - mla_in.shape[-1])))),
        mla_q_norm=mla_q_norm, mla_kv_norm=mla_kv_norm,
        mla_uq=bf(jnp.stack([_pad_heads(mla_uq[j], MLA_HEADS, MLA_QD) for j in range(n_odd)])),
        mla_uk=bf(jnp.stack([_pad_heads(mla_uk[j], MLA_HEADS, MLA_NOPE) for j in range(n_odd)])),
        mla_uv=bf(mla_uv), mla_out=bf(mla_out),
        mla_place=bf(place.reshape(MLA_ROPE, MLA_QW)),
    )
    d_ff = ffn1_dn.shape[1]
    tf = _pick(d_ff, (1408, 256, 128))

    t_p = x_prompt.shape[1]
    m_p = x_prompt.shape[0] * t_p
    tk_p = _pick(t_p, (512, 256, 128, 64))
    tq_p = _pick(tk_p, (256, 128, 64))
    out_p = _trunk(x_prompt, p_prompt, 0, None, W, tm=_pick(m_p, (512, 256, 128)), tf=tf, tq=tq_p, tk=tk_p,
                   sb_tq=tq_p, sb_tk=tq_p)

    caches = (cache_sb_k, cache_sb_v, cache_df_k, cache_df_v, cache_mla_latent, cache_mla_krope)
    past = cache_sb_k.shape[2]
    t_s = x_sample.shape[1]
    m_s = x_sample.shape[0] * t_s
    tk_s = _pick(past, (256, 128, 64))
    out_s = _trunk(x_sample, p_sample, past, caches, W, tm=_pick(m_s, (128,)), tf=tf, tq=t_s, tk=tk_s,
                   sb_tq=t_s, sb_tk=tk_s)

    (y_p, sb_k_p, sb_v_p, df_k_p, df_v_p, lat_p, kr_p) = out_p
    (y_s, sb_k_s, sb_v_s, df_k_s, df_v_s, lat_s, kr_s) = out_s
    return (y_p, y_s, sb_k_p, sb_v_p, df_k_p, df_v_p, lat_p, kr_p,
            sb_k_s, sb_v_s, df_k_s, df_v_s, lat_s, kr_s)
```

```python
import functools
import math

import jax
import jax.numpy as jnp
from jax import lax
from jax.experimental import pallas as pl
from jax.experimental.pallas import tpu as pltpu

F32 = jnp.float32
BF16 = jnp.bfloat16

EPS = 1e-6
CHUNK = 64
SB_HEADS, SB_HD = 8, 64
DF_HEADS, DF_HD = 4, 64
DF_ROT = DF_HD // 4
ROPE_THETA = 500000.0
MLA_HEADS, MLA_NOPE, MLA_ROPE, MLA_VD = 16, 64, 32, 64
Q_LORA, KV_LORA = 384, 256
MLA_THETA = 10000.0
MLA_QD = MLA_NOPE + MLA_ROPE
LOG2E = math.log2(math.e)
SB_UNDERFLOW = 151.0

LANES = 128
SB_W = SB_HEADS * SB_HD
DF_W = DF_HEADS * 2 * DF_HD
MLA_QW = MLA_HEADS * LANES
MLA_VW = MLA_HEADS * MLA_VD
MLA_IN_PAD = 768

VMEM_LIMIT = 56 * 1024 * 1024


def _cparams(sem):
    return pltpu.CompilerParams(dimension_semantics=sem, vmem_limit_bytes=VMEM_LIMIT)


def _rms(x, g):
    return x * lax.rsqrt(jnp.mean(x * x, axis=-1, keepdims=True) + EPS) * g


def _dot(a, b):
    return jnp.dot(a, b, preferred_element_type=F32)


def _dot_t(a, b):
    return lax.dot_general(a, b, (((1,), (1,)), ((), ())), preferred_element_type=F32)


def _rope_slab(x, cos, sin_up, sin_dn, shift):
    n = x.shape[-1]
    return x * cos + pltpu.roll(x, n - shift, 1) * sin_up + pltpu.roll(x, shift, 1) * sin_dn


def _ffn_kernel(x_ref, gin_ref, gout_ref, wg_ref, wu_ref, wd_ref, o_ref, h_ref, acc_ref):
    j = pl.program_id(1)

    @pl.when(j == 0)
    def _():
        h_ref[...] = _rms(x_ref[...], gin_ref[...]).astype(BF16)
        acc_ref[...] = jnp.zeros_like(acc_ref)

    h = h_ref[...]
    g = _dot(h, wg_ref[...])
    u = _dot(h, wu_ref[...])
    a = (g * jax.nn.sigmoid(g) * u).astype(BF16)
    acc_ref[...] += _dot(a, wd_ref[...])

    @pl.when(j == pl.num_programs(1) - 1)
    def _():
        o_ref[...] = x_ref[...] + 0.5 * _rms(acc_ref[...], gout_ref[...])


def _ffn_block(x, g_in, g_out, w_gu, w_dn, tm, tf):
    m, d = x.shape
    d_ff = w_dn.shape[0]
    nf = d_ff // tf
    row = lambda i, j: (i, 0)
    const = lambda i, j: (0, 0)
    return pl.pallas_call(
        _ffn_kernel,
        grid=(m // tm, nf),
        in_specs=[
            pl.BlockSpec((tm, d), row),
            pl.BlockSpec((1, d), const),
            pl.BlockSpec((1, d), const),
            pl.BlockSpec((d, tf), lambda i, j: (0, j)),
            pl.BlockSpec((d, tf), lambda i, j: (0, j + nf)),
            pl.BlockSpec((tf, d), lambda i, j: (j, 0)),
        ],
        out_specs=pl.BlockSpec((tm, d), row),
        out_shape=jax.ShapeDtypeStruct((m, d), F32),
        scratch_shapes=[pltpu.VMEM((tm, d), BF16), pltpu.VMEM((tm, d), F32)],
        compiler_params=_cparams(("parallel", "arbitrary")),
        name="ffn_block",
    )(x, g_in, g_out, w_gu, w_gu, w_dn)


def _ab_in_kernel(x_ref, g_ref, w_ref, cos_ref, sup_ref, sdn_ref,
                  sbq_ref, sbk_ref, sbv_ref, dfq_ref, dfk_ref, dfv_ref,
                  sbk_f_ref, sbv_f_ref, dfk_f_ref, dfv_f_ref):
    h = _rms(x_ref[...], g_ref[...]).astype(BF16)
    cos, sup, sdn = cos_ref[...], sup_ref[...], sdn_ref[...]

    def proj(c):
        return _dot(h, w_ref[:, c * SB_W:(c + 1) * SB_W])

    def store_rows(out_f, x, first):
        for j in range(x.shape[-1] // SB_HD):
            out_f[pl.ds(first + j, x.shape[0], stride=SB_W // SB_HD), :] = x[:, j * SB_HD:(j + 1) * SB_HD]

    def rope(x, scale, out_b, out_f):
        for s in range(x.shape[-1] // LANES):
            sl = slice(s * LANES, (s + 1) * LANES)
            r = _rope_slab(x[:, sl], cos, sup, sdn, DF_ROT // 2)
            if out_f is not None:
                store_rows(out_f, r, s * (LANES // SB_HD))
            out_b[:, sl] = (r * scale).astype(BF16)

    sbq_ref[...] = (proj(0) * (SB_HD ** -0.5 * LOG2E)).astype(BF16)
    sbk = proj(1)
    store_rows(sbk_f_ref, sbk, 0)
    sbk_ref[...] = sbk.astype(BF16)
    sbv = proj(2)
    store_rows(sbv_f_ref, sbv, 0)
    sbv_ref[...] = sbv.astype(BF16)
    rope(proj(3), DF_HD ** -0.5 * LOG2E, dfq_ref, None)
    rope(proj(4), 1.0, dfk_ref, dfk_f_ref)
    dfv = proj(5)
    dfv_f_ref[...] = dfv
    dfv_ref[...] = dfv.astype(BF16)


def _ab_in_proj(x, g, w_in, tabs, tm):
    m, d = x.shape
    n_tab = tabs[0].shape[0] // tm
    row = lambda i: (i, 0)
    const = lambda i: (0, 0)
    tab = lambda i: (i % n_tab, 0)
    wide = pl.BlockSpec((tm, SB_W), row)
    bf = jax.ShapeDtypeStruct((m, SB_W), BF16)
    f32 = jax.ShapeDtypeStruct((m, SB_W), F32)
    n_rows = SB_W // SB_HD
    tall = pl.BlockSpec((tm * n_rows, SB_HD), row)
    f32_tall = jax.ShapeDtypeStruct((m * n_rows, SB_HD), F32)
    return pl.pallas_call(
        _ab_in_kernel,
        grid=(m // tm,),
        in_specs=[
            pl.BlockSpec((tm, d), row),
            pl.BlockSpec((1, d), const),
            pl.BlockSpec(w_in.shape, const),
            pl.BlockSpec((tm, LANES), tab),
            pl.BlockSpec((tm, LANES), tab),
            pl.BlockSpec((tm, LANES), tab),
        ],
        out_specs=[wide] * 6 + [tall] * 3 + [wide],
        out_shape=[bf] * 6 + [f32_tall] * 3 + [f32],
        compiler_params=_cparams(("parallel",)),
        name="ab_in_proj",
    )(x, g, w_in, *tabs)


def _out_proj_kernel(*refs, n_parts):
    x_ref, g_ref, w_ref = refs[0], refs[1], refs[2]
    parts = refs[3:3 + n_parts]
    o_ref = refs[3 + n_parts]
    off = 0
    m = None
    for p in parts:
        k = p.shape[-1]
        t = _dot(p[...], w_ref[off:off + k, :])
        m = t if m is None else m + t
        off += k
    o_ref[...] = x_ref[...] + _rms(m, g_ref[...])


def _out_proj(x, g, w_out, parts, tm):
    m, d = x.shape
    row = lambda i: (i, 0)
    const = lambda i: (0, 0)
    return pl.pallas_call(
        functools.partial(_out_proj_kernel, n_parts=len(parts)),
        grid=(m // tm,),
        in_specs=[
            pl.BlockSpec((tm, d), row),
            pl.BlockSpec((1, d), const),
            pl.BlockSpec(w_out.shape, const),
        ] + [pl.BlockSpec((tm, p.shape[-1]), row) for p in parts],
        out_specs=pl.BlockSpec((tm, d), row),
        out_shape=jax.ShapeDtypeStruct((m, d), F32),
        compiler_params=_cparams(("parallel",)),
        name="out_proj",
    )(x, g, w_out, *parts)


def _ple_kernel(x_ref, p_ref, gg_ref, go_ref, wg_ref, wp_ref, o_ref):
    x = x_ref[...]
    gate = jax.nn.sigmoid(_dot(_rms(x, gg_ref[...]).astype(BF16), wg_ref[...]))
    emb = _dot(p_ref[...].astype(BF16), wp_ref[...])
    o_ref[...] = x + _rms(gate * emb, go_ref[...])


def _ple_block(x, p, g_gate, g_out, w_gate, w_in, tm):
    m, d = x.shape
    row = lambda i: (i, 0)
    const = lambda i: (0, 0)
    return pl.pallas_call(
        _ple_kernel,
        grid=(m // tm,),
        in_specs=[
            pl.BlockSpec((tm, d), row),
            pl.BlockSpec((tm, p.shape[-1]), row),
            pl.BlockSpec((1, d), const),
            pl.BlockSpec((1, d), const),
            pl.BlockSpec(w_gate.shape, const),
            pl.BlockSpec(w_in.shape, const),
        ],
        out_specs=pl.BlockSpec((tm, d), row),
        out_shape=jax.ShapeDtypeStruct((m, d), F32),
        compiler_params=_cparams(("parallel",)),
        name="ple_block",
    )(x, p, g_gate, g_out, w_gate, w_in)


def _mla_in_kernel(x_ref, g_ref, w_ref, qn_ref, kvn_ref, wuq_ref,
                   qcos_ref, qsup_ref, qsdn_ref, kcos_ref, ksup_ref, ksdn_ref,
                   q_ref, lat_ref, kr_ref):
    h = _rms(x_ref[...], g_ref[...]).astype(BF16)
    proj = _dot(h, w_ref[...])
    c_q = _rms(proj[:, :Q_LORA], qn_ref[...]).astype(BF16)
    lat_ref[...] = _rms(proj[:, Q_LORA:Q_LORA + KV_LORA], kvn_ref[...])
    kr = _rope_slab(proj[:, Q_LORA + KV_LORA:], kcos_ref[...], ksup_ref[...], ksdn_ref[...],
                    MLA_ROPE // 2)
    kr_ref[...] = kr[:, :MLA_ROPE]
    q = _dot(c_q, wuq_ref[...])
    cos, sup, sdn = qcos_ref[...], qsup_ref[...], qsdn_ref[...]
    for hd in range(MLA_HEADS):
        sl = slice(hd * LANES, (hd + 1) * LANES)
        r = _rope_slab(q[:, sl], cos, sup, sdn, MLA_ROPE // 2)
        q_ref[:, sl] = (r * (MLA_QD ** -0.5 * LOG2E)).astype(BF16)


def _mla_in_proj(x, g, w_in, q_norm, kv_norm, w_uq, qtabs, ktabs, tm):
    m, d = x.shape
    n_tab = qtabs[0].shape[0] // tm
    row = lambda i: (i, 0)
    const = lambda i: (0, 0)
    tab = lambda i: (i % n_tab, 0)
    tspec = pl.BlockSpec((tm, LANES), tab)
    return pl.pallas_call(
        _mla_in_kernel,
        grid=(m // tm,),
        in_specs=[
            pl.BlockSpec((tm, d), row),
            pl.BlockSpec((1, d), const),
            pl.BlockSpec(w_in.shape, const),
            pl.BlockSpec((1, Q_LORA), const),
            pl.BlockSpec((1, KV_LORA), const),
            pl.BlockSpec(w_uq.shape, const),
        ] + [tspec] * 6,
        out_specs=[
            pl.BlockSpec((tm, MLA_QW), row),
            pl.BlockSpec((tm, KV_LORA), row),
            pl.BlockSpec((tm, MLA_ROPE), row),
        ],
        out_shape=[
            jax.ShapeDtypeStruct((m, MLA_QW), BF16),
            jax.ShapeDtypeStruct((m, KV_LORA), F32),
            jax.ShapeDtypeStruct((m, MLA_ROPE), F32),
        ],
        compiler_params=_cparams(("parallel",)),
        name="mla_in_proj",
    )(x, g, w_in, q_norm, kv_norm, w_uq, *qtabs, *ktabs)


def _mla_kv_kernel(lat_ref, kr_ref, wuk_ref, place_ref, wuv_ref, k_ref, v_ref):
    lat = lat_ref[...].astype(BF16)
    k_ref[...] = (_dot(lat, wuk_ref[...]) + _dot(kr_ref[...].astype(BF16), place_ref[...])).astype(BF16)
    v_ref[...] = _dot(lat, wuv_ref[...]).astype(BF16)


def _mla_kv_expand(lat, kr, w_uk, place, w_uv, tm):
    m = lat.shape[0]
    row = lambda i: (i, 0)
    const = lambda i: (0, 0)
    return pl.pallas_call(
        _mla_kv_kernel,
        grid=(m // tm,),
        in_specs=[
            pl.BlockSpec((tm, KV_LORA), row),
            pl.BlockSpec((tm, MLA_ROPE), row),
            pl.BlockSpec(w_uk.shape, const),
            pl.BlockSpec(place.shape, const),
            pl.BlockSpec(w_uv.shape, const),
        ],
        out_specs=[pl.BlockSpec((tm, MLA_QW), row), pl.BlockSpec((tm, MLA_VW), row)],
        out_shape=[jax.ShapeDtypeStruct((m, MLA_QW), BF16), jax.ShapeDtypeStruct((m, MLA_VW), BF16)],
        compiler_params=_cparams(("parallel",)),
        name="mla_kv_expand",
    )(lat, kr, w_uk, place, w_uv)


def _positions(q0, kb, tq, tk):
    t = q0 + lax.broadcasted_iota(jnp.int32, (tq, tk), 0)
    s = kb * tk + lax.broadcasted_iota(jnp.int32, (tq, tk), 1)
    return t, s


def _sb_kernel(q_ref, k_ref, v_ref, o_ref, *, tq, tk, q_start):
    q0 = q_start + pl.program_id(2) * tq
    n_full = q0 // tk
    q = q_ref[0]
    lane = lax.broadcasted_iota(jnp.int32, (tq, LANES), 1)
    q_heads = (jnp.where(lane < SB_HD, q, jnp.zeros_like(q)),
               jnp.where(lane >= SB_HD, q, jnp.zeros_like(q)))
    later = (lax.broadcasted_iota(jnp.int32, (tk, tk), 0)
             > lax.broadcasted_iota(jnp.int32, (tk, tk), 1)).astype(BF16)

    def load(kb):
        start = pl.multiple_of(kb * tk, tk)
        return k_ref[0, pl.ds(start, tk), :], v_ref[0, pl.ds(start, tk), :]

    def miss_cost(z):
        cost = jnp.maximum(z, 0.0) + jnp.log2(1.0 + jnp.exp2(-jnp.abs(z)))
        return cost, z - cost

    def cost_after(cost):
        hi = cost.astype(BF16)
        lo = (cost - hi.astype(F32)).astype(BF16)
        return _dot(hi, later) + _dot(lo, later)

    def block_total(cost, after):
        return after[:, 0:1] + cost[:, 0:1]

    lower = jnp.maximum(n_full - 1, 0)
    (k_lo, v_lo), (k_up, v_up) = load(lower), load(lower + 1)
    t, s_lo = _positions(q0, lower, tq, tk)
    earlier_lo = s_lo < t
    earlier_up = s_lo + tk < t
    state = []
    for qh in q_heads:
        cost_up, hit_up = miss_cost(_dot_t(qh, k_up))
        cost_lo, hit_lo = miss_cost(_dot_t(qh, k_lo))
        cost_up = jnp.where(earlier_up, cost_up, 0.0)
        cost_lo = jnp.where(earlier_lo, cost_lo, 0.0)
        after_up, after_lo = cost_after(cost_up), cost_after(cost_lo)
        run_up = block_total(cost_up, after_up)
        w_up = jnp.where(earlier_up, jnp.exp2(hit_up - after_up), 0.0)
        w_lo = jnp.where(earlier_lo, jnp.exp2(hit_lo - after_lo - run_up), 0.0)
        acc = _dot(w_up.astype(BF16), v_up) + _dot(w_lo.astype(BF16), v_lo)
        state.append((run_up + block_total(cost_lo, after_lo), acc))

    def least_run(st):
        return jnp.min(jnp.minimum(st[0][0], st[1][0]))

    def more(c):
        kb, least, _ = c
        return (kb >= 0) & (least < SB_UNDERFLOW)

    def block(c):
        kb, _, st = c
        ks, vs = load(kb)
        new = []
        for qh, (run, acc) in zip(q_heads, st):
            cost, hit = miss_cost(_dot_t(qh, ks))
            after = cost_after(cost)
            acc = acc + _dot(jnp.exp2(hit - after - run).astype(BF16), vs)
            new.append((run + block_total(cost, after), acc))
        return kb - 1, least_run(new), tuple(new)

    _, _, state = lax.while_loop(more, block, (lower - 1, least_run(state), tuple(state)))
    o_ref[0] = jnp.where(lane < SB_HD, state[0][1], state[1][1]).astype(o_ref.dtype)


def _visible(q0, kb, tq, tk, n_keys):
    t, s = _positions(q0, kb, tq, tk)
    return ((s // CHUNK) <= (t // CHUNK)) & (s < n_keys)


def _softmax_sweep(score_fns, v_ref, s_refs, q0, *, tq, tk, n_keys):
    n_full = q0 // tk
    n_chain = len(score_fns)
    s_even, s_odd = s_refs

    def issue(kb, buf):
        row_max = []
        for c, f in enumerate(score_fns):
            s = f(pl.multiple_of(kb * tk, tk))
            buf[c] = s
            row_max.append(jnp.max(s, axis=-1, keepdims=True))
        return tuple(row_max)

    def reduce(kb, buf, row_max, state, masked):
        v_all = v_ref[0, pl.ds(pl.multiple_of(kb * tk, tk), tk), :]
        if masked:
            vis = _visible(q0, kb, tq, tk, n_keys)
        new = []
        for c, (m, l, acc) in enumerate(state):
            vs = v_all[:, (c // 2) * LANES:(c // 2 + 1) * LANES]
            s = buf[c]
            if masked:
                s = jnp.where(vis, s, -jnp.inf)
                blk_max = jnp.max(s, axis=-1, keepdims=True)
            else:
                blk_max = row_max[c]
            m_new = jnp.maximum(m, blk_max)
            alpha = jnp.exp2(m - m_new)
            p = jnp.exp2(s - m_new)
            l = alpha * l + jnp.sum(p, axis=-1, keepdims=True)
            acc = alpha * acc + _dot(p.astype(BF16), vs)
            new.append((m_new, l, acc))
        return tuple(new)

    def pair(j, carry):
        row_max, state = carry
        odd_max = issue(2 * j + 1, s_odd)
        state = reduce(2 * j, s_even, row_max, state, False)
        row_max = issue(2 * j + 2, s_even)
        return row_max, reduce(2 * j + 1, s_odd, odd_max, state, False)

    init = (jnp.full((tq, 1), -jnp.inf, F32), jnp.zeros((tq, 1), F32), jnp.zeros((tq, LANES), F32))
    row_max, state = lax.fori_loop(0, n_full // 2, pair, (issue(0, s_even), (init,) * n_chain))

    def last_is_even(state):
        return reduce(n_full, s_even, None, state, True)

    def last_is_odd(state):
        issue(n_full, s_odd)
        state = reduce(n_full - 1, s_even, row_max, state, False)
        return reduce(n_full, s_odd, None, state, True)

    return lax.cond(n_full % 2 == 1, last_is_odd, last_is_even, state)


def _df_kernel(lam_ref, sub_ref, q_ref, k_ref, v_ref, o_ref, *s_refs, tq, tk, q_start, n_keys, lam_init):
    q0 = q_start + pl.program_id(2) * tq
    q = q_ref[0]
    heads = q.shape[-1] // LANES
    lane = lax.broadcasted_iota(jnp.int32, (tq, LANES), 1)
    score_fns = []
    for h in range(heads):
        sl = slice(h * LANES, (h + 1) * LANES)
        for qp in (jnp.where(lane < DF_HD, q[:, sl], jnp.zeros_like(q[:, sl])),
                   jnp.where(lane >= DF_HD, q[:, sl], jnp.zeros_like(q[:, sl]))):
            score_fns.append(lambda start, qp=qp, sl=sl: _dot_t(qp, k_ref[0, pl.ds(start, tk), sl]))
    state = _softmax_sweep(score_fns, v_ref, s_refs, q0, tq=tq, tk=tk, n_keys=n_keys)

    lv = lam_ref[...]
    lam = (jnp.exp(jnp.sum(lv[0:1] * lv[1:2], axis=-1, keepdims=True))
           - jnp.exp(jnp.sum(lv[2:3] * lv[3:4], axis=-1, keepdims=True)) + lam_init)
    for h in range(heads):
        (_, l0, a0), (_, l1, a1) = state[2 * h], state[2 * h + 1]
        out = a0 / l0 - lam * (a1 / l1)
        o_ref[0, :, h * LANES:(h + 1) * LANES] = (
            _rms(out, sub_ref[...]) * (1.0 - lam_init)).astype(o_ref.dtype)


def _mla_kernel(q_ref, k_ref, v_ref, o_ref, *s_refs, tq, tk, q_start, n_keys):
    q0 = q_start + pl.program_id(2) * tq
    q = q_ref[0]
    heads = q.shape[-1] // LANES
    lane = lax.broadcasted_iota(jnp.int32, (tq, LANES), 1)
    score_fns = [
        lambda start, sl=slice(h * LANES, (h + 1) * LANES): _dot_t(q[:, sl], k_ref[0, pl.ds(start, tk), sl])
        for h in range(heads)]
    state = _softmax_sweep(score_fns, v_ref, s_refs, q0, tq=tq, tk=tk, n_keys=n_keys)
    for g in range(heads // 2):
        (_, l0, a0), (_, l1, a1) = state[2 * g], state[2 * g + 1]
        o_ref[0, :, g * LANES:(g + 1) * LANES] = jnp.where(lane < MLA_VD, a0 / l0, a1 / l1).astype(o_ref.dtype)


def _mla_cached_kernel(q_ref, latc_ref, krc_ref, latn_ref, krn_ref, wuk_ref, place_ref, wuv_ref, o_ref,
                       *, q_start):
    q = q_ref[0]
    t = q.shape[0]
    slabs = [slice(h * LANES, (h + 1) * LANES) for h in range(MLA_HEADS)]
    q_all = jnp.concatenate([q[:, sl] for sl in slabs], axis=0)
    q_lat = jnp.concatenate([_dot_t(q[:, sl], wuk_ref[:, sl]) for sl in slabs], axis=0).astype(BF16)
    rows = MLA_HEADS * t
    pad = LANES - t

    def scores(lat, kr):
        kr_lanes = _dot(kr.astype(BF16), place_ref[...]).astype(BF16)
        return _dot_t(q_lat, lat) + _dot_t(q_all, kr_lanes)

    lat_c = latc_ref[0].astype(BF16)
    s_c = scores(lat_c, krc_ref[0])
    lat_n = jnp.concatenate([latn_ref[0], jnp.zeros((pad, KV_LORA), F32)], axis=0).astype(BF16)
    kr_n = jnp.concatenate([krn_ref[0], jnp.zeros((pad, MLA_ROPE), F32)], axis=0)
    t_pos = q_start + lax.broadcasted_iota(jnp.int32, (rows, LANES), 0) % t
    j = lax.broadcasted_iota(jnp.int32, (rows, LANES), 1)
    vis = (j < t) & (((q_start + j) // CHUNK) <= (t_pos // CHUNK))
    s_n = jnp.where(vis, scores(lat_n, kr_n), -jnp.inf)

    m = jnp.maximum(jnp.max(s_c, axis=-1, keepdims=True), jnp.max(s_n, axis=-1, keepdims=True))
    p_c, p_n = jnp.exp2(s_c - m), jnp.exp2(s_n - m)
    l = jnp.sum(p_c, axis=-1, keepdims=True) + jnp.sum(p_n, axis=-1, keepdims=True)
    o_lat = (_dot(p_c.astype(BF16), lat_c) + _dot(p_n.astype(BF16), lat_n)) / l
    full = _dot(o_lat.astype(BF16), wuv_ref[...])
    lane_head = lax.broadcasted_iota(jnp.int32, (t, MLA_VW), 1) // MLA_VD
    out = jnp.zeros((t, MLA_VW), F32)
    for h in range(MLA_HEADS):
        out = out + jnp.where(lane_head == h, full[h * t:(h + 1) * t, :], 0.0)
    o_ref[0] = out.astype(o_ref.dtype)


def _mla_cached_attention(q, lat_cache, kr_cache, lat_new, kr_new, w_uk, place, w_uv, q_start):
    b, t, _ = q.shape
    past = lat_cache.shape[1]
    assert t <= LANES and q_start == past
    per_b = lambda bi: (bi, 0, 0)
    const = lambda bi: (0, 0)
    return pl.pallas_call(
        functools.partial(_mla_cached_kernel, q_start=q_start),
        grid=(b,),
        in_specs=[
            pl.BlockSpec((1, t, MLA_QW), per_b),
            pl.BlockSpec((1, past, KV_LORA), per_b),
            pl.BlockSpec((1, past, MLA_ROPE), per_b),
            pl.BlockSpec((1, t, KV_LORA), per_b),
            pl.BlockSpec((1, t, MLA_ROPE), per_b),
            pl.BlockSpec(w_uk.shape, const),
            pl.BlockSpec(place.shape, const),
            pl.BlockSpec(w_uv.shape, const),
        ],
        out_specs=pl.BlockSpec((1, t, MLA_VW), per_b),
        out_shape=jax.ShapeDtypeStruct((b, t, MLA_VW), BF16),
        compiler_params=_cparams(("parallel",)),
        name="mla_cached_attn",
    )(q, lat_cache, kr_cache, lat_new, kr_new, w_uk, place, w_uv)


def _attention(kernel_fn, q, k, v, extra, *, q_lanes, k_lanes, tq, tk, name, score_chains=0, v_groups=1):
    b, t_q, _ = q.shape
    t_k = k.shape[1]
    groups = v.shape[-1] // (LANES * v_groups)
    q_lanes, k_lanes, v_lanes = q_lanes * v_groups, k_lanes * v_groups, LANES * v_groups
    assert t_q % tq == 0 and t_k % tk == 0 and tk % tq == 0 and tk % CHUNK == 0
    const = lambda bi, g, qi: (0, 0)
    return pl.pallas_call(
        kernel_fn,
        grid=(b, groups, t_q // tq),
        in_specs=[pl.BlockSpec(e.shape, const) for e in extra] + [
            pl.BlockSpec((1, tq, q_lanes), lambda bi, g, qi: (bi, qi, g)),
            pl.BlockSpec((1, t_k, k_lanes), lambda bi, g, qi: (bi, 0, g)),
            pl.BlockSpec((1, t_k, v_lanes), lambda bi, g, qi: (bi, 0, g)),
        ],
        out_specs=pl.BlockSpec((1, tq, v_lanes), lambda bi, g, qi: (bi, qi, g)),
        out_shape=jax.ShapeDtypeStruct((b, t_q, groups * v_lanes), BF16),
        scratch_shapes=[pltpu.VMEM((score_chains, tq, tk), F32)] * 2 if score_chains else [],
        compiler_params=_cparams(("parallel", "parallel", "arbitrary")),
        name=name,
    )(*extra, q, k, v)


def _rope_tables(pos, rot, theta, period, offset):
    half = rot // 2
    inv = jnp.float32(theta) ** (-(jnp.arange(half, dtype=F32) * (2.0 / rot)))
    ang = pos.astype(F32)[:, None] * inv[None, :]
    cos, sin = jnp.cos(ang), jnp.sin(ang)
    lane = jnp.arange(LANES) % period - offset
    first = (lane >= 0) & (lane < half)
    second = (lane >= half) & (lane < rot)
    idx = jnp.clip(jnp.where(second, lane - half, lane), 0, half - 1)
    cos_l, sin_l = cos[:, idx], sin[:, idx]
    cos_t = jnp.where(first | second, cos_l, 1.0)
    sin_up = jnp.where(first, -sin_l, 0.0)
    sin_dn = jnp.where(second, sin_l, 0.0)
    return cos_t, sin_up, sin_dn


def _pad_heads(w, heads, width):
    k = w.shape[0]
    w = w.reshape(k, heads, width)
    return jnp.pad(w, ((0, 0), (0, 0), (0, LANES - width))).reshape(k, heads * LANES)


def _with_cache(cache, new, t_pad, dtype):
    b, past, w = cache.shape
    pad = jnp.zeros((b, t_pad - past - new.shape[1], w), dtype)
    return jnp.concatenate([cache.astype(dtype), new.astype(dtype), pad], axis=1)


def _trunk(x, p, start, caches, W, *, tm, tf, tq, tk, sb_tq, sb_tk, df_groups, mla_groups):
    b, t, d = x.shape
    m = b * t
    depth = p.shape[0]
    x = x.reshape(m, d)
    pos = start + jnp.arange(t)
    if t % tm == 0:
        tab_pos = pos
    else:
        tab_pos = jnp.tile(pos, m // t)
    df_tabs = _rope_tables(tab_pos, DF_ROT, ROPE_THETA, DF_HD, 0)
    mq_tabs = _rope_tables(tab_pos, MLA_ROPE, MLA_THETA, LANES, MLA_NOPE)
    mk_tabs = _rope_tables(tab_pos, MLA_ROPE, MLA_THETA, LANES, 0)
    n_keys = start + t
    t_keys = -(-n_keys // tk) * tk

    ab_rows, mla_rows = [], []
    for i in range(depth):
        g = W['norms'][i][:, None, :]
        j = i // 2
        x = _ffn_block(x, g[0], g[1], W['ffn1_gu'][i], W['ffn1_dn'][i], tm, tf)
        if i % 2 == 0:
            (sbq, sbk, sbv, dfq, dfk, dfv, sbk_f, sbv_f, dfk_f, dfv_f) = _ab_in_proj(
                x, g[2], W['ab_in'][j], df_tabs, tm)
            ab_rows.append((sbk_f.reshape(b, t, SB_HEADS, SB_HD), sbv_f.reshape(b, t, SB_HEADS, SB_HD),
                            dfk_f.reshape(b, t, DF_HEADS, 2, DF_HD), dfv_f.reshape(b, t, DF_HEADS, 2 * DF_HD)))
            three = lambda a: a.reshape(b, t, a.shape[-1])
            if caches is None:
                keys = [three(a) for a in (sbk, sbv, dfk, dfv)]
            else:
                keys = [_with_cache(c[j].reshape(b, start, -1), n.reshape(b, t, -1), t_keys, BF16)
                        for c, n in zip(caches[:4], (sbk_f, sbv_f, dfk_f, dfv_f))]
            sb_out = _attention(
                functools.partial(_sb_kernel, tq=sb_tq, tk=sb_tk, q_start=start),
                three(sbq), keys[0], keys[1], [], q_lanes=LANES, k_lanes=LANES, tq=sb_tq, tk=sb_tk,
                name="sb_attn")
            lam_init = 0.8 - 0.6 * math.exp(-0.3 * i)
            df_out = _attention(
                functools.partial(_df_kernel, tq=tq, tk=tk, q_start=start, n_keys=n_keys, lam_init=lam_init),
                three(dfq), keys[2], keys[3], [W['df_lambda'][j], W['df_subln'][j][None, :]],
                q_lanes=LANES, k_lanes=LANES, tq=tq, tk=tk, name="df_attn",
                score_chains=2 * df_groups, v_groups=df_groups)
            parts = [sb_out.reshape(m, SB_W), df_out.reshape(m, DF_W)]
            w_out = W['ab_out'][j]
        else:
            q, lat, kr = _mla_in_proj(x, g[2], W['mla_in'][j], W['mla_q_norm'][j][None, :],
                                      W['mla_kv_norm'][j][None, :], W['mla_uq'][j], mq_tabs, mk_tabs, tm)
            mla_rows.append((lat.reshape(b, t, KV_LORA), kr.reshape(b, t, MLA_ROPE)))
            if caches is None:
                k_all, v_all = _mla_kv_expand(lat, kr, W['mla_uk'][j], W['mla_place'], W['mla_uv'][j], tm)
                out = _attention(
                    functools.partial(_mla_kernel, tq=tq, tk=tk, q_start=start, n_keys=n_keys),
                    q.reshape(b, t, MLA_QW), k_all.reshape(b, -1, MLA_QW), v_all.reshape(b, -1, MLA_VW), [],
                    q_lanes=2 * LANES, k_lanes=2 * LANES, tq=tq, tk=tk, name="mla_attn",
                    score_chains=2 * mla_groups, v_groups=mla_groups)
            else:
                out = _mla_cached_attention(
                    q.reshape(b, t, MLA_QW), caches[4][j], caches[5][j], lat.reshape(b, t, KV_LORA),
                    kr.reshape(b, t, MLA_ROPE), W['mla_uk'][j], W['mla_place'][:, :LANES], W['mla_uv'][j], start)
            parts = [out.reshape(m, MLA_VW)]
            w_out = W['mla_out'][j]
        x = _out_proj(x, g[3], w_out, parts, tm)
        x = _ffn_block(x, g[4], g[5], W['ffn2_gu'][i], W['ffn2_dn'][i], tm, tf)
        x = _ple_block(x, p[i].reshape(m, -1), g[6], g[7], W['ple_gate'][i], W['ple_in'][i], tm)
    sb_k, sb_v, df_k, df_v = [jnp.stack(r, axis=0) for r in zip(*ab_rows)]
    lat, kr = [jnp.stack(r, axis=0) for r in zip(*mla_rows)]
    return x.reshape(b, t, d), sb_k, sb_v, df_k, df_v, lat, kr


def _pick(n, candidates):
    for c in candidates:
        if n % c == 0:
            return c
    return n


def kernel(x_prompt, x_sample, p_prompt, p_sample, cache_sb_k, cache_sb_v, cache_df_k, cache_df_v,
           cache_mla_latent, cache_mla_krope, norms, ffn1_gu, ffn1_dn, ffn2_gu, ffn2_dn, ple_in, ple_gate,
           ab_in, ab_out, df_lambda, df_subln, mla_in, mla_q_norm, mla_kv_norm, mla_uq, mla_uk, mla_uv, mla_out):
    bf = lambda a: a.astype(BF16)
    n_odd = mla_in.shape[0]
    place = jnp.zeros((MLA_ROPE, MLA_HEADS, LANES), F32)
    place = place.at[jnp.arange(MLA_ROPE), :, MLA_NOPE + jnp.arange(MLA_ROPE)].set(1.0)
    W = dict(
        norms=norms, ffn1_gu=bf(ffn1_gu), ffn1_dn=bf(ffn1_dn), ffn2_gu=bf(ffn2_gu), ffn2_dn=bf(ffn2_dn),
        ple_in=bf(ple_in), ple_gate=bf(ple_gate), ab_in=bf(ab_in), ab_out=bf(ab_out),
        df_lambda=df_lambda, df_subln=df_subln,
        mla_in=bf(jnp.pad(mla_in, ((0, 0), (0, 0), (0, MLA_IN_PAD - mla_in.shape[-1])))),
        mla_q_norm=mla_q_norm, mla_kv_norm=mla_kv_norm,
        mla_uq=bf(jnp.stack([_pad_heads(mla_uq[j], MLA_HEADS, MLA_QD) for j in range(n_odd)])),
        mla_uk=bf(jnp.stack([_pad_heads(mla_uk[j], MLA_HEADS, MLA_NOPE) for j in range(n_odd)])),
        mla_uv=bf(mla_uv), mla_out=bf(mla_out),
        mla_place=bf(place.reshape(MLA_ROPE, MLA_QW)),
    )
    d_ff = ffn1_dn.shape[1]
    tf = _pick(d_ff, (1408, 256, 128))

    t_p = x_prompt.shape[1]
    m_p = x_prompt.shape[0] * t_p
    tk_p = _pick(t_p, (512, 256, 128, 64))
    tq_p = _pick(tk_p, (256, 128, 64))
    out_p = _trunk(x_prompt, p_prompt, 0, None, W, tm=_pick(m_p, (512, 256, 128)), tf=tf, tq=tq_p, tk=tk_p,
                   sb_tq=tq_p, sb_tk=tq_p, df_groups=2, mla_groups=2)

    caches = (cache_sb_k, cache_sb_v, cache_df_k, cache_df_v, cache_mla_latent, cache_mla_krope)
    past = cache_sb_k.shape[2]
    t_s = x_sample.shape[1]
    m_s = x_sample.shape[0] * t_s
    tk_s = _pick(past, (256, 128, 64))
    out_s = _trunk(x_sample, p_sample, past, caches, W, tm=_pick(m_s, (128,)), tf=tf, tq=t_s, tk=tk_s,
                   sb_tq=t_s, sb_tk=tk_s, df_groups=DF_HEADS, mla_groups=MLA_HEADS // 2)

    (y_p, sb_k_p, sb_v_p, df_k_p, df_v_p, lat_p, kr_p) = out_p
    (y_s, sb_k_s, sb_v_s, df_k_s, df_v_s, lat_s, kr_s) = out_s
    return (y_p, y_s, sb_k_p, sb_v_p, df_k_p, df_v_p, lat_p, kr_p,
            sb_k_s, sb_v_s, df_k_s, df_v_s, lat_s, kr_s)
```

```python
import functools
import math

import jax
import jax.numpy as jnp
from jax import lax
from jax.experimental import pallas as pl
from jax.experimental.pallas import tpu as pltpu

F32 = jnp.float32
BF16 = jnp.bfloat16

EPS = 1e-6
CHUNK = 64
SB_HEADS, SB_HD = 8, 64
DF_HEADS, DF_HD = 4, 64
DF_ROT = DF_HD // 4
ROPE_THETA = 500000.0
MLA_HEADS, MLA_NOPE, MLA_ROPE, MLA_VD = 16, 64, 32, 64
Q_LORA, KV_LORA = 384, 256
MLA_THETA = 10000.0
MLA_QD = MLA_NOPE + MLA_ROPE
LOG2E = math.log2(math.e)
SB_UNDERFLOW = 151.0
BUCKET = 1024

LANES = 128
SB_W = SB_HEADS * SB_HD
DF_W = DF_HEADS * 2 * DF_HD
MLA_QW = MLA_HEADS * LANES
MLA_VW = MLA_HEADS * MLA_VD
MLA_IN_PAD = 768

VMEM_LIMIT = 56 * 1024 * 1024


def _cparams(sem):
    return pltpu.CompilerParams(dimension_semantics=sem, vmem_limit_bytes=VMEM_LIMIT)


def _rms(x, g):
    return x * lax.rsqrt(jnp.mean(x * x, axis=-1, keepdims=True) + EPS) * g


def _dot(a, b):
    return jnp.dot(a, b, preferred_element_type=F32)


def _dot_t(a, b):
    return lax.dot_general(a, b, (((1,), (1,)), ((), ())), preferred_element_type=F32)


def _rope_slab(x, cos, sin_up, sin_dn, shift):
    n = x.shape[-1]
    return x * cos + pltpu.roll(x, n - shift, 1) * sin_up + pltpu.roll(x, shift, 1) * sin_dn


def _ffn_kernel(x_ref, gin_ref, gout_ref, wg_ref, wu_ref, wd_ref, o_ref, h_ref, acc_ref):
    j = pl.program_id(1)

    @pl.when(j == 0)
    def _():
        h_ref[...] = _rms(x_ref[...], gin_ref[...]).astype(BF16)
        acc_ref[...] = jnp.zeros_like(acc_ref)

    h = h_ref[...]
    g = _dot(h, wg_ref[...])
    u = _dot(h, wu_ref[...])
    a = (g * jax.nn.sigmoid(g) * u).astype(BF16)
    acc_ref[...] += _dot(a, wd_ref[...])

    @pl.when(j == pl.num_programs(1) - 1)
    def _():
        o_ref[...] = x_ref[...] + 0.5 * _rms(acc_ref[...], gout_ref[...])


def _ffn_block(x, g_in, g_out, w_gu, w_dn, tm, tf):
    m, d = x.shape
    d_ff = w_dn.shape[0]
    nf = d_ff // tf
    row = lambda i, j: (i, 0)
    const = lambda i, j: (0, 0)
    return pl.pallas_call(
        _ffn_kernel,
        grid=(m // tm, nf),
        in_specs=[
            pl.BlockSpec((tm, d), row),
            pl.BlockSpec((1, d), const),
            pl.BlockSpec((1, d), const),
            pl.BlockSpec((d, tf), lambda i, j: (0, j)),
            pl.BlockSpec((d, tf), lambda i, j: (0, j + nf)),
            pl.BlockSpec((tf, d), lambda i, j: (j, 0)),
        ],
        out_specs=pl.BlockSpec((tm, d), row),
        out_shape=jax.ShapeDtypeStruct((m, d), F32),
        scratch_shapes=[pltpu.VMEM((tm, d), BF16), pltpu.VMEM((tm, d), F32)],
        compiler_params=_cparams(("parallel", "arbitrary")),
        name="ffn_block",
    )(x, g_in, g_out, w_gu, w_gu, w_dn)


def _ab_in_kernel(x_ref, g_ref, w_ref, cos_ref, sup_ref, sdn_ref,
                  sbq_ref, sbk_ref, sbv_ref, dfq_ref, dfk_ref, dfv_ref,
                  sbk_f_ref, sbv_f_ref, dfk_f_ref, dfv_f_ref):
    h = _rms(x_ref[...], g_ref[...]).astype(BF16)
    cos, sup, sdn = cos_ref[...], sup_ref[...], sdn_ref[...]

    def proj(c):
        return _dot(h, w_ref[:, c * SB_W:(c + 1) * SB_W])

    def store_rows(out_f, x, first):
        for j in range(x.shape[-1] // SB_HD):
            out_f[pl.ds(first + j, x.shape[0], stride=SB_W // SB_HD), :] = x[:, j * SB_HD:(j + 1) * SB_HD]

    def rope(x, scale, out_b, out_f):
        for s in range(x.shape[-1] // LANES):
            sl = slice(s * LANES, (s + 1) * LANES)
            r = _rope_slab(x[:, sl], cos, sup, sdn, DF_ROT // 2)
            if out_f is not None:
                store_rows(out_f, r, s * (LANES // SB_HD))
            out_b[:, sl] = (r * scale).astype(BF16)

    sbq_ref[...] = (proj(0) * (SB_HD ** -0.5 * LOG2E)).astype(BF16)
    sbk = proj(1)
    store_rows(sbk_f_ref, sbk, 0)
    sbk_ref[...] = sbk.astype(BF16)
    sbv = proj(2)
    store_rows(sbv_f_ref, sbv, 0)
    sbv_ref[...] = sbv.astype(BF16)
    rope(proj(3), DF_HD ** -0.5 * LOG2E, dfq_ref, None)
    rope(proj(4), 1.0, dfk_ref, dfk_f_ref)
    dfv = proj(5)
    dfv_f_ref[...] = dfv
    dfv_ref[...] = dfv.astype(BF16)


def _ab_in_proj(x, g, w_in, tabs, tm):
    m, d = x.shape
    n_tab = tabs[0].shape[0] // tm
    row = lambda i: (i, 0)
    const = lambda i: (0, 0)
    tab = lambda i: (i % n_tab, 0)
    wide = pl.BlockSpec((tm, SB_W), row)
    bf = jax.ShapeDtypeStruct((m, SB_W), BF16)
    f32 = jax.ShapeDtypeStruct((m, SB_W), F32)
    n_rows = SB_W // SB_HD
    tall = pl.BlockSpec((tm * n_rows, SB_HD), row)
    f32_tall = jax.ShapeDtypeStruct((m * n_rows, SB_HD), F32)
    return pl.pallas_call(
        _ab_in_kernel,
        grid=(m // tm,),
        in_specs=[
            pl.BlockSpec((tm, d), row),
            pl.BlockSpec((1, d), const),
            pl.BlockSpec(w_in.shape, const),
            pl.BlockSpec((tm, LANES), tab),
            pl.BlockSpec((tm, LANES), tab),
            pl.BlockSpec((tm, LANES), tab),
        ],
        out_specs=[wide] * 6 + [tall] * 3 + [wide],
        out_shape=[bf] * 6 + [f32_tall] * 3 + [f32],
        compiler_params=_cparams(("parallel",)),
        name="ab_in_proj",
    )(x, g, w_in, *tabs)


def _post_mixer_kernel(*refs, n_parts):
    (x_ref, p_ref, g_ref, wo_ref, wg_ref, wu_ref, wd_ref, wgate_ref, wple_ref) = refs[:9]
    parts = refs[9:9 + n_parts]
    o_ref, x1_ref, h_ref, acc_ref = refs[9 + n_parts:]
    j = pl.program_id(1)
    g_mix, g_in, g_out, g_gate, g_ple = (g_ref[i] for i in range(5))

    @pl.when(j == 0)
    def _():
        off = 0
        mix = None
        for p in parts:
            k = p.shape[-1]
            t = _dot(p[...], wo_ref[off:off + k, :])
            mix = t if mix is None else mix + t
            off += k
        x1 = x_ref[...] + _rms(mix, g_mix)
        x1_ref[...] = x1
        h_ref[...] = _rms(x1, g_in).astype(BF16)
        acc_ref[...] = jnp.zeros_like(acc_ref)

    h = h_ref[...]
    gt = _dot(h, wg_ref[...])
    up = _dot(h, wu_ref[...])
    acc_ref[...] += _dot((gt * jax.nn.sigmoid(gt) * up).astype(BF16), wd_ref[...])

    @pl.when(j == pl.num_programs(1) - 1)
    def _():
        x2 = x1_ref[...] + 0.5 * _rms(acc_ref[...], g_out)
        gate = jax.nn.sigmoid(_dot(_rms(x2, g_gate).astype(BF16), wgate_ref[...]))
        emb = _dot(p_ref[...].astype(BF16), wple_ref[...])
        o_ref[...] = x2 + _rms(gate * emb, g_ple)


def _post_mixer(x, p, gains, w_out, w_gu, w_dn, w_gate, w_ple, parts, tm, tf):
    m, d = x.shape
    nf = w_dn.shape[0] // tf
    row = lambda i, j: (i, 0)
    whole = lambda a: pl.BlockSpec(a.shape, lambda i, j: (0,) * a.ndim, pipeline_mode=pl.Buffered(1))
    return pl.pallas_call(
        functools.partial(_post_mixer_kernel, n_parts=len(parts)),
        grid=(m // tm, nf),
        in_specs=[
            pl.BlockSpec((tm, d), row),
            pl.BlockSpec((tm, p.shape[-1]), row),
            whole(gains),
            whole(w_out),
            pl.BlockSpec((d, tf), lambda i, j: (0, j)),
            pl.BlockSpec((d, tf), lambda i, j: (0, j + nf)),
            pl.BlockSpec((tf, d), lambda i, j: (j, 0)),
            whole(w_gate),
            whole(w_ple),
        ] + [pl.BlockSpec((tm, a.shape[-1]), row) for a in parts],
        out_specs=pl.BlockSpec((tm, d), row),
        out_shape=jax.ShapeDtypeStruct((m, d), F32),
        scratch_shapes=[pltpu.VMEM((tm, d), F32), pltpu.VMEM((tm, d), BF16), pltpu.VMEM((tm, d), F32)],
        compiler_params=_cparams(("parallel", "arbitrary")),
        name="post_mixer",
    )(x, p, gains, w_out, w_gu, w_gu, w_dn, w_gate, w_ple, *parts)


def _mla_in_kernel(x_ref, g_ref, w_ref, qn_ref, kvn_ref, wuq_ref,
                   qcos_ref, qsup_ref, qsdn_ref, kcos_ref, ksup_ref, ksdn_ref,
                   q_ref, lat_ref, kr_ref):
    h = _rms(x_ref[...], g_ref[...]).astype(BF16)
    proj = _dot(h, w_ref[...])
    c_q = _rms(proj[:, :Q_LORA], qn_ref[...]).astype(BF16)
    lat_ref[...] = _rms(proj[:, Q_LORA:Q_LORA + KV_LORA], kvn_ref[...])
    kr = _rope_slab(proj[:, Q_LORA + KV_LORA:], kcos_ref[...], ksup_ref[...], ksdn_ref[...],
                    MLA_ROPE // 2)
    kr_ref[...] = kr[:, :MLA_ROPE]
    q = _dot(c_q, wuq_ref[...])
    cos, sup, sdn = qcos_ref[...], qsup_ref[...], qsdn_ref[...]
    for hd in range(MLA_HEADS):
        sl = slice(hd * LANES, (hd + 1) * LANES)
        r = _rope_slab(q[:, sl], cos, sup, sdn, MLA_ROPE // 2)
        q_ref[:, sl] = (r * (MLA_QD ** -0.5 * LOG2E)).astype(BF16)


def _mla_in_proj(x, g, w_in, q_norm, kv_norm, w_uq, qtabs, ktabs, tm):
    m, d = x.shape
    n_tab = qtabs[0].shape[0] // tm
    row = lambda i: (i, 0)
    const = lambda i: (0, 0)
    tab = lambda i: (i % n_tab, 0)
    tspec = pl.BlockSpec((tm, LANES), tab)
    return pl.pallas_call(
        _mla_in_kernel,
        grid=(m // tm,),
        in_specs=[
            pl.BlockSpec((tm, d), row),
            pl.BlockSpec((1, d), const),
            pl.BlockSpec(w_in.shape, const),
            pl.BlockSpec((1, Q_LORA), const),
            pl.BlockSpec((1, KV_LORA), const),
            pl.BlockSpec(w_uq.shape, const),
        ] + [tspec] * 6,
        out_specs=[
            pl.BlockSpec((tm, MLA_QW), row),
            pl.BlockSpec((tm, KV_LORA), row),
            pl.BlockSpec((tm, MLA_ROPE), row),
        ],
        out_shape=[
            jax.ShapeDtypeStruct((m, MLA_QW), BF16),
            jax.ShapeDtypeStruct((m, KV_LORA), F32),
            jax.ShapeDtypeStruct((m, MLA_ROPE), F32),
        ],
        compiler_params=_cparams(("parallel",)),
        name="mla_in_proj",
    )(x, g, w_in, q_norm, kv_norm, w_uq, *qtabs, *ktabs)


def _mla_kv_kernel(lat_ref, kr_ref, wuk_ref, place_ref, wuv_ref, k_ref, v_ref):
    lat = lat_ref[...].astype(BF16)
    k_ref[...] = (_dot(lat, wuk_ref[...]) + _dot(kr_ref[...].astype(BF16), place_ref[...])).astype(BF16)
    v_ref[...] = _dot(lat, wuv_ref[...]).astype(BF16)


def _mla_kv_expand(lat, kr, w_uk, place, w_uv, tm):
    m = lat.shape[0]
    row = lambda i: (i, 0)
    const = lambda i: (0, 0)
    return pl.pallas_call(
        _mla_kv_kernel,
        grid=(m // tm,),
        in_specs=[
            pl.BlockSpec((tm, KV_LORA), row),
            pl.BlockSpec((tm, MLA_ROPE), row),
            pl.BlockSpec(w_uk.shape, const),
            pl.BlockSpec(place.shape, const),
            pl.BlockSpec(w_uv.shape, const),
        ],
        out_specs=[pl.BlockSpec((tm, MLA_QW), row), pl.BlockSpec((tm, MLA_VW), row)],
        out_shape=[jax.ShapeDtypeStruct((m, MLA_QW), BF16), jax.ShapeDtypeStruct((m, MLA_VW), BF16)],
        compiler_params=_cparams(("parallel",)),
        name="mla_kv_expand",
    )(lat, kr, w_uk, place, w_uv)


def _positions(q0, kb, tq, tk):
    t = q0 + lax.broadcasted_iota(jnp.int32, (tq, tk), 0)
    s = kb * tk + lax.broadcasted_iota(jnp.int32, (tq, tk), 1)
    return t, s


def _sb_kernel(q_ref, k_ref, v_ref, o_ref, *, tq, tk, q_start):
    q0 = q_start + pl.program_id(2) * tq
    n_full = q0 // tk
    q = q_ref[0]
    lane = lax.broadcasted_iota(jnp.int32, (tq, LANES), 1)
    q_heads = (jnp.where(lane < SB_HD, q, jnp.zeros_like(q)),
               jnp.where(lane >= SB_HD, q, jnp.zeros_like(q)))
    later = (lax.broadcasted_iota(jnp.int32, (tk, tk), 0)
             > lax.broadcasted_iota(jnp.int32, (tk, tk), 1)).astype(BF16)

    def load(kb):
        start = pl.multiple_of(kb * tk, tk)
        return k_ref[0, pl.ds(start, tk), :], v_ref[0, pl.ds(start, tk), :]

    def miss_cost(z):
        cost = jnp.maximum(z, 0.0) + jnp.log2(1.0 + jnp.exp2(-jnp.abs(z)))
        return cost, z - cost

    def cost_after(cost):
        hi = cost.astype(BF16)
        lo = (cost - hi.astype(F32)).astype(BF16)
        return _dot(hi, later) + _dot(lo, later)

    def block_total(cost, after):
        return after[:, 0:1] + cost[:, 0:1]

    lower = jnp.maximum(n_full - 1, 0)
    (k_lo, v_lo), (k_up, v_up) = load(lower), load(lower + 1)
    t, s_lo = _positions(q0, lower, tq, tk)
    earlier_lo = s_lo < t
    earlier_up = s_lo + tk < t
    state = []
    for qh in q_heads:
        cost_up, hit_up = miss_cost(_dot_t(qh, k_up))
        cost_lo, hit_lo = miss_cost(_dot_t(qh, k_lo))
        cost_up = jnp.where(earlier_up, cost_up, 0.0)
        cost_lo = jnp.where(earlier_lo, cost_lo, 0.0)
        after_up, after_lo = cost_after(cost_up), cost_after(cost_lo)
        run_up = block_total(cost_up, after_up)
        w_up = jnp.where(earlier_up, jnp.exp2(hit_up - after_up), 0.0)
        w_lo = jnp.where(earlier_lo, jnp.exp2(hit_lo - after_lo - run_up), 0.0)
        acc = _dot(w_up.astype(BF16), v_up) + _dot(w_lo.astype(BF16), v_lo)
        state.append((run_up + block_total(cost_lo, after_lo), acc))

    def least_run(st):
        return jnp.min(jnp.minimum(st[0][0], st[1][0]))

    def more(c):
        kb, least, _ = c
        return (kb >= 0) & (least < SB_UNDERFLOW)

    def block(c):
        kb, _, st = c
        ks, vs = load(kb)
        new = []
        for qh, (run, acc) in zip(q_heads, st):
            cost, hit = miss_cost(_dot_t(qh, ks))
            after = cost_after(cost)
            acc = acc + _dot(jnp.exp2(hit - after - run).astype(BF16), vs)
            new.append((run + block_total(cost, after), acc))
        return kb - 1, least_run(new), tuple(new)

    _, _, state = lax.while_loop(more, block, (lower - 1, least_run(state), tuple(state)))
    o_ref[0] = jnp.where(lane < SB_HD, state[0][1], state[1][1]).astype(o_ref.dtype)


def _softmax_bucketed(chains, k_ref, v_ref, q0, emit, *, tq, width, q_range, n_keys):
    bucket = q0 // width
    for j in range(q_range[0] // width, (q_range[1] - 1) // width + 1):
        @pl.when(bucket == j)
        def _(j=j):
            length = (j + 1) * width
            t = q0 + lax.broadcasted_iota(jnp.int32, (tq, width), 0)
            s_pos = length - width + lax.broadcasted_iota(jnp.int32, (tq, width), 1)
            vis = ((s_pos // CHUNK) <= (t // CHUNK)) & (s_pos < n_keys)
            outs = []
            for qc, k_sl, v_sl in chains:
                s = _dot_t(qc, k_ref[0, 0:length, k_sl])
                tail = jnp.where(vis, s[:, length - width:], -jnp.inf)
                s = tail if length == width else jnp.concatenate([s[:, :length - width], tail], axis=1)
                p = jnp.exp2(s - jnp.max(s, axis=-1, keepdims=True))
                l = jnp.sum(p, axis=-1, keepdims=True)
                outs.append(_dot(p.astype(BF16), v_ref[0, 0:length, v_sl]) / l)
            emit(outs)


def _lane_slab(i):
    return slice(i * LANES, (i + 1) * LANES)


def _df_kernel(lam_ref, sub_ref, q_ref, k_ref, v_ref, o_ref, *, tq, width, q_range, n_keys, lam_init):
    q0 = q_range[0] + pl.program_id(2) * tq
    q = q_ref[0]
    heads = q.shape[-1] // LANES
    lane = lax.broadcasted_iota(jnp.int32, (tq, LANES), 1)
    chains = []
    for h in range(heads):
        qh = q[:, _lane_slab(h)]
        chains.append((jnp.where(lane < DF_HD, qh, jnp.zeros_like(qh)), _lane_slab(h), _lane_slab(h)))
        chains.append((jnp.where(lane >= DF_HD, qh, jnp.zeros_like(qh)), _lane_slab(h), _lane_slab(h)))

    def emit(outs):
        lv = lam_ref[...]
        lam = (jnp.exp(jnp.sum(lv[0:1] * lv[1:2], axis=-1, keepdims=True))
               - jnp.exp(jnp.sum(lv[2:3] * lv[3:4], axis=-1, keepdims=True)) + lam_init)
        for h in range(heads):
            out = outs[2 * h] - lam * outs[2 * h + 1]
            o_ref[0, :, _lane_slab(h)] = (_rms(out, sub_ref[...]) * (1.0 - lam_init)).astype(o_ref.dtype)

    _softmax_bucketed(chains, k_ref, v_ref, q0, emit, tq=tq, width=width, q_range=q_range, n_keys=n_keys)


def _mla_kernel(q_ref, k_ref, v_ref, o_ref, *, tq, width, q_range, n_keys):
    q0 = q_range[0] + pl.program_id(2) * tq
    q = q_ref[0]
    heads = q.shape[-1] // LANES
    lane = lax.broadcasted_iota(jnp.int32, (tq, LANES), 1)
    chains = [(q[:, _lane_slab(h)], _lane_slab(h), _lane_slab(h // 2)) for h in range(heads)]

    def emit(outs):
        for g in range(heads // 2):
            o_ref[0, :, _lane_slab(g)] = jnp.where(
                lane < MLA_VD, outs[2 * g], outs[2 * g + 1]).astype(o_ref.dtype)

    _softmax_bucketed(chains, k_ref, v_ref, q0, emit, tq=tq, width=width, q_range=q_range, n_keys=n_keys)


def _mla_cached_kernel(q_ref, latc_ref, krc_ref, latn_ref, krn_ref, wuk_ref, place_ref, wuv_ref, o_ref,
                       *, q_start):
    q = q_ref[0]
    t = q.shape[0]
    slabs = [slice(h * LANES, (h + 1) * LANES) for h in range(MLA_HEADS)]
    q_all = jnp.concatenate([q[:, sl] for sl in slabs], axis=0)
    q_lat = jnp.concatenate([_dot_t(q[:, sl], wuk_ref[:, sl]) for sl in slabs], axis=0).astype(BF16)
    rows = MLA_HEADS * t
    pad = LANES - t

    def scores(lat, kr):
        kr_lanes = _dot(kr.astype(BF16), place_ref[...]).astype(BF16)
        return _dot_t(q_lat, lat) + _dot_t(q_all, kr_lanes)

    lat_c = latc_ref[0].astype(BF16)
    s_c = scores(lat_c, krc_ref[0])
    lat_n = jnp.concatenate([latn_ref[0], jnp.zeros((pad, KV_LORA), F32)], axis=0).astype(BF16)
    kr_n = jnp.concatenate([krn_ref[0], jnp.zeros((pad, MLA_ROPE), F32)], axis=0)
    t_pos = q_start + lax.broadcasted_iota(jnp.int32, (rows, LANES), 0) % t
    j = lax.broadcasted_iota(jnp.int32, (rows, LANES), 1)
    vis = (j < t) & (((q_start + j) // CHUNK) <= (t_pos // CHUNK))
    s_n = jnp.where(vis, scores(lat_n, kr_n), -jnp.inf)

    m = jnp.maximum(jnp.max(s_c, axis=-1, keepdims=True), jnp.max(s_n, axis=-1, keepdims=True))
    p_c, p_n = jnp.exp2(s_c - m), jnp.exp2(s_n - m)
    l = jnp.sum(p_c, axis=-1, keepdims=True) + jnp.sum(p_n, axis=-1, keepdims=True)
    o_lat = (_dot(p_c.astype(BF16), lat_c) + _dot(p_n.astype(BF16), lat_n)) / l
    full = _dot(o_lat.astype(BF16), wuv_ref[...])
    lane_head = lax.broadcasted_iota(jnp.int32, (t, MLA_VW), 1) // MLA_VD
    out = jnp.zeros((t, MLA_VW), F32)
    for h in range(MLA_HEADS):
        out = out + jnp.where(lane_head == h, full[h * t:(h + 1) * t, :], 0.0)
    o_ref[0] = out.astype(o_ref.dtype)


def _mla_cached_attention(q, lat_cache, kr_cache, lat_new, kr_new, w_uk, place, w_uv, q_start):
    b, t, _ = q.shape
    past = lat_cache.shape[1]
    assert t <= LANES and q_start == past
    per_b = lambda bi: (bi, 0, 0)
    const = lambda bi: (0, 0)
    return pl.pallas_call(
        functools.partial(_mla_cached_kernel, q_start=q_start),
        grid=(b,),
        in_specs=[
            pl.BlockSpec((1, t, MLA_QW), per_b),
            pl.BlockSpec((1, past, KV_LORA), per_b),
            pl.BlockSpec((1, past, MLA_ROPE), per_b),
            pl.BlockSpec((1, t, KV_LORA), per_b),
            pl.BlockSpec((1, t, MLA_ROPE), per_b),
            pl.BlockSpec(w_uk.shape, const),
            pl.BlockSpec(place.shape, const),
            pl.BlockSpec(w_uv.shape, const),
        ],
        out_specs=pl.BlockSpec((1, t, MLA_VW), per_b),
        out_shape=jax.ShapeDtypeStruct((b, t, MLA_VW), BF16),
        compiler_params=_cparams(("parallel",)),
        name="mla_cached_attn",
    )(q, lat_cache, kr_cache, lat_new, kr_new, w_uk, place, w_uv)


def _attention(kernel_fn, q, k, v, extra, *, q_lanes, k_lanes, tq, tk, name, v_groups=1):
    b, t_q, _ = q.shape
    t_k = k.shape[1]
    groups = v.shape[-1] // (LANES * v_groups)
    q_lanes, k_lanes, v_lanes = q_lanes * v_groups, k_lanes * v_groups, LANES * v_groups
    assert t_q % tq == 0 and t_k % tk == 0 and tk % tq == 0 and tk % CHUNK == 0
    const = lambda bi, g, qi: (0, 0)
    return pl.pallas_call(
        kernel_fn,
        grid=(b, groups, t_q // tq),
        in_specs=[pl.BlockSpec(e.shape, const) for e in extra] + [
            pl.BlockSpec((1, tq, q_lanes), lambda bi, g, qi: (bi, qi, g)),
            pl.BlockSpec((1, t_k, k_lanes), lambda bi, g, qi: (bi, 0, g)),
            pl.BlockSpec((1, t_k, v_lanes), lambda bi, g, qi: (bi, 0, g)),
        ],
        out_specs=pl.BlockSpec((1, tq, v_lanes), lambda bi, g, qi: (bi, qi, g)),
        out_shape=jax.ShapeDtypeStruct((b, t_q, groups * v_lanes), BF16),
        compiler_params=_cparams(("parallel", "parallel", "arbitrary")),
        name=name,
    )(*extra, q, k, v)


def _rope_tables(pos, rot, theta, period, offset):
    half = rot // 2
    inv = jnp.float32(theta) ** (-(jnp.arange(half, dtype=F32) * (2.0 / rot)))
    ang = pos.astype(F32)[:, None] * inv[None, :]
    cos, sin = jnp.cos(ang), jnp.sin(ang)
    lane = jnp.arange(LANES) % period - offset
    first = (lane >= 0) & (lane < half)
    second = (lane >= half) & (lane < rot)
    idx = jnp.clip(jnp.where(second, lane - half, lane), 0, half - 1)
    cos_l, sin_l = cos[:, idx], sin[:, idx]
    cos_t = jnp.where(first | second, cos_l, 1.0)
    sin_up = jnp.where(first, -sin_l, 0.0)
    sin_dn = jnp.where(second, sin_l, 0.0)
    return cos_t, sin_up, sin_dn


def _pad_heads(w, heads, width):
    k = w.shape[0]
    w = w.reshape(k, heads, width)
    return jnp.pad(w, ((0, 0), (0, 0), (0, LANES - width))).reshape(k, heads * LANES)


def _with_cache(cache, new, t_pad, dtype):
    b, past, w = cache.shape
    pad = jnp.zeros((b, t_pad - past - new.shape[1], w), dtype)
    return jnp.concatenate([cache.astype(dtype), new.astype(dtype), pad], axis=1)


def _trunk(x, p, start, caches, W, *, tm, tf, tq, width, sb_tq, sb_tk, df_groups, mla_groups):
    b, t, d = x.shape
    m = b * t
    depth = p.shape[0]
    x = x.reshape(m, d)
    pos = start + jnp.arange(t)
    if t % tm == 0:
        tab_pos = pos
    else:
        tab_pos = jnp.tile(pos, m // t)
    df_tabs = _rope_tables(tab_pos, DF_ROT, ROPE_THETA, DF_HD, 0)
    mq_tabs = _rope_tables(tab_pos, MLA_ROPE, MLA_THETA, LANES, MLA_NOPE)
    mk_tabs = _rope_tables(tab_pos, MLA_ROPE, MLA_THETA, LANES, 0)
    n_keys = start + t
    t_keys = -(-n_keys // width) * width
    assert t_keys % sb_tk == 0
    q_range = (start, start + t)

    ab_rows, mla_rows = [], []
    for i in range(depth):
        g = W['norms'][i][:, None, :]
        j = i // 2
        x = _ffn_block(x, g[0], g[1], W['ffn1_gu'][i], W['ffn1_dn'][i], tm, tf)
        if i % 2 == 0:
            (sbq, sbk, sbv, dfq, dfk, dfv, sbk_f, sbv_f, dfk_f, dfv_f) = _ab_in_proj(
                x, g[2], W['ab_in'][j], df_tabs, tm)
            ab_rows.append((sbk_f.reshape(b, t, SB_HEADS, SB_HD), sbv_f.reshape(b, t, SB_HEADS, SB_HD),
                            dfk_f.reshape(b, t, DF_HEADS, 2, DF_HD), dfv_f.reshape(b, t, DF_HEADS, 2 * DF_HD)))
            three = lambda a: a.reshape(b, t, a.shape[-1])
            if caches is None:
                keys = [three(a) for a in (sbk, sbv, dfk, dfv)]
            else:
                keys = [_with_cache(c[j].reshape(b, start, -1), n.reshape(b, t, -1), t_keys, BF16)
                        for c, n in zip(caches[:4], (sbk_f, sbv_f, dfk_f, dfv_f))]
            sb_out = _attention(
                functools.partial(_sb_kernel, tq=sb_tq, tk=sb_tk, q_start=start),
                three(sbq), keys[0], keys[1], [], q_lanes=LANES, k_lanes=LANES, tq=sb_tq, tk=sb_tk,
                name="sb_attn")
            lam_init = 0.8 - 0.6 * math.exp(-0.3 * i)
            df_out = _attention(
                functools.partial(_df_kernel, tq=tq, width=width, q_range=q_range, n_keys=n_keys,
                                  lam_init=lam_init),
                three(dfq), keys[2], keys[3], [W['df_lambda'][j], W['df_subln'][j][None, :]],
                q_lanes=LANES, k_lanes=LANES, tq=tq, tk=width, name="df_attn", v_groups=df_groups)
            parts = [sb_out.reshape(m, SB_W), df_out.reshape(m, DF_W)]
            w_out = W['ab_out'][j]
        else:
            q, lat, kr = _mla_in_proj(x, g[2], W['mla_in'][j], W['mla_q_norm'][j][None, :],
                                      W['mla_kv_norm'][j][None, :], W['mla_uq'][j], mq_tabs, mk_tabs, tm)
            mla_rows.append((lat.reshape(b, t, KV_LORA), kr.reshape(b, t, MLA_ROPE)))
            if caches is None:
                k_all, v_all = _mla_kv_expand(lat, kr, W['mla_uk'][j], W['mla_place'], W['mla_uv'][j], tm)
                out = _attention(
                    functools.partial(_mla_kernel, tq=tq, width=width, q_range=q_range, n_keys=n_keys),
                    q.reshape(b, t, MLA_QW), k_all.reshape(b, -1, MLA_QW), v_all.reshape(b, -1, MLA_VW), [],
                    q_lanes=2 * LANES, k_lanes=2 * LANES, tq=tq, tk=width, name="mla_attn",
                    v_groups=mla_groups)
            else:
                out = _mla_cached_attention(
                    q.reshape(b, t, MLA_QW), caches[4][j], caches[5][j], lat.reshape(b, t, KV_LORA),
                    kr.reshape(b, t, MLA_ROPE), W['mla_uk'][j], W['mla_place'][:, :LANES], W['mla_uv'][j], start)
            parts = [out.reshape(m, MLA_VW)]
            w_out = W['mla_out'][j]
        x = _post_mixer(x, p[i].reshape(m, -1), g[3:8], w_out, W['ffn2_gu'][i], W['ffn2_dn'][i],
                        W['ple_gate'][i], W['ple_in'][i], parts, tm, tf)
    sb_k, sb_v, df_k, df_v = [jnp.stack(r, axis=0) for r in zip(*ab_rows)]
    lat, kr = [jnp.stack(r, axis=0) for r in zip(*mla_rows)]
    return x.reshape(b, t, d), sb_k, sb_v, df_k, df_v, lat, kr


def _pick(n, candidates):
    for c in candidates:
        if n % c == 0:
            return c
    return n


def _cast_kernel(x_ref, o_ref):
    o_ref[...] = x_ref[...].astype(o_ref.dtype)


def _to_bf16(w):
    n_l, k, n = w.shape
    bk = _pick(k, (256, 128, 64, 32, 16))
    spec = pl.BlockSpec((1, bk, n), lambda l, i: (l, i, 0))
    return pl.pallas_call(
        _cast_kernel, grid=(n_l, k // bk), in_specs=[spec], out_specs=spec,
        out_shape=jax.ShapeDtypeStruct(w.shape, BF16),
        compiler_params=_cparams(("parallel", "parallel")), name="to_bf16",
    )(w)


def kernel(x_prompt, x_sample, p_prompt, p_sample, cache_sb_k, cache_sb_v, cache_df_k, cache_df_v,
           cache_mla_latent, cache_mla_krope, norms, ffn1_gu, ffn1_dn, ffn2_gu, ffn2_dn, ple_in, ple_gate,
           ab_in, ab_out, df_lambda, df_subln, mla_in, mla_q_norm, mla_kv_norm, mla_uq, mla_uk, mla_uv, mla_out):
    bf = lambda a: a.astype(BF16)
    n_odd = mla_in.shape[0]
    place = jnp.zeros((MLA_ROPE, MLA_HEADS, LANES), F32)
    place = place.at[jnp.arange(MLA_ROPE), :, MLA_NOPE + jnp.arange(MLA_ROPE)].set(1.0)
    W = dict(
        norms=norms, ffn1_gu=_to_bf16(ffn1_gu), ffn1_dn=_to_bf16(ffn1_dn), ffn2_gu=_to_bf16(ffn2_gu),
        ffn2_dn=_to_bf16(ffn2_dn), ple_in=bf(ple_in), ple_gate=_to_bf16(ple_gate), ab_in=_to_bf16(ab_in),
        ab_out=_to_bf16(ab_out),
        df_lambda=df_lambda, df_subln=df_subln,
        mla_in=bf(jnp.pad(mla_in, ((0, 0), (0, 0), (0, MLA_IN_PAD - mla_in.shape[-1])))),
        mla_q_norm=mla_q_norm, mla_kv_norm=mla_kv_norm,
        mla_uq=bf(jnp.stack([_pad_heads(mla_uq[j], MLA_HEADS, MLA_QD) for j in range(n_odd)])),
        mla_uk=bf(jnp.stack([_pad_heads(mla_uk[j], MLA_HEADS, MLA_NOPE) for j in range(n_odd)])),
        mla_uv=bf(mla_uv), mla_out=bf(mla_out),
        mla_place=bf(place.reshape(MLA_ROPE, MLA_QW)),
    )
    d_ff = ffn1_dn.shape[1]
    tf = _pick(d_ff, (1408, 256, 128))

    t_p = x_prompt.shape[1]
    m_p = x_prompt.shape[0] * t_p
    width_p = _pick(t_p, (BUCKET, 512, 256, 128, 64))
    tq_p = _pick(width_p, (256, 128, 64))
    out_p = _trunk(x_prompt, p_prompt, 0, None, W, tm=_pick(m_p, (512, 256, 128)), tf=tf, tq=tq_p, width=width_p,
                   sb_tq=tq_p, sb_tk=tq_p, df_groups=2, mla_groups=2)

    caches = (cache_sb_k, cache_sb_v, cache_df_k, cache_df_v, cache_mla_latent, cache_mla_krope)
    past = cache_sb_k.shape[2]
    t_s = x_sample.shape[1]
    m_s = x_sample.shape[0] * t_s
    tk_s = _pick(past, (256, 128, 64))
    out_s = _trunk(x_sample, p_sample, past, caches, W, tm=_pick(m_s, (128,)), tf=tf, tq=t_s, width=tk_s,
                   sb_tq=t_s, sb_tk=tk_s, df_groups=DF_HEADS, mla_groups=MLA_HEADS // 2)

    (y_p, sb_k_p, sb_v_p, df_k_p, df_v_p, lat_p, kr_p) = out_p
    (y_s, sb_k_s, sb_v_s, df_k_s, df_v_s, lat_s, kr_s) = out_s
    return (y_p, y_s, sb_k_p, sb_v_p, df_k_p, df_v_p, lat_p, kr_p,
            sb_k_s, sb_v_s, df_k_s, df_v_s, lat_s, kr_s)
```

```python
import functools
import math

import jax
import jax.numpy as jnp
from jax import lax
from jax.experimental import pallas as pl
from jax.experimental.pallas import tpu as pltpu

F32 = jnp.float32
BF16 = jnp.bfloat16

EPS = 1e-6
CHUNK = 64
SB_HEADS, SB_HD = 8, 64
DF_HEADS, DF_HD = 4, 64
DF_ROT = DF_HD // 4
ROPE_THETA = 500000.0
MLA_HEADS, MLA_NOPE, MLA_ROPE, MLA_VD = 16, 64, 32, 64
Q_LORA, KV_LORA = 384, 256
MLA_THETA = 10000.0
MLA_QD = MLA_NOPE + MLA_ROPE
LOG2E = math.log2(math.e)
SB_UNDERFLOW = 151.0
BUCKET = 1024

LANES = 128
SB_W = SB_HEADS * SB_HD
DF_W = DF_HEADS * 2 * DF_HD
MLA_QW = MLA_HEADS * LANES
MLA_VW = MLA_HEADS * MLA_VD
MLA_IN_PAD = 768

VMEM_LIMIT = 56 * 1024 * 1024


def _cparams(sem):
    return pltpu.CompilerParams(dimension_semantics=sem, vmem_limit_bytes=VMEM_LIMIT)


def _rms(x, g):
    return x * lax.rsqrt(jnp.mean(x * x, axis=-1, keepdims=True) + EPS) * g


def _dot(a, b):
    return jnp.dot(a, b, preferred_element_type=F32)


def _dot_t(a, b):
    return lax.dot_general(a, b, (((1,), (1,)), ((), ())), preferred_element_type=F32)


def _rope_slab(x, cos, sin_up, sin_dn, shift):
    n = x.shape[-1]
    return x * cos + pltpu.roll(x, n - shift, 1) * sin_up + pltpu.roll(x, shift, 1) * sin_dn


def _swiglu(h, wgu_ref, wd_ref, chunk):
    d_ff = wd_ref.shape[0]
    acc = None
    for c in range(d_ff // chunk):
        g = _dot(h, wgu_ref[:, c * chunk:(c + 1) * chunk])
        u = _dot(h, wgu_ref[:, d_ff + c * chunk:d_ff + (c + 1) * chunk])
        t = _dot((g * jax.nn.sigmoid(g) * u).astype(BF16), wd_ref[c * chunk:(c + 1) * chunk, :])
        acc = t if acc is None else acc + t
    return acc


def _whole(a, n_grid):
    return pl.BlockSpec(a.shape, lambda *_: (0,) * a.ndim, pipeline_mode=pl.Buffered(1))


def _ffn_kernel(x_ref, gin_ref, gout_ref, wgu_ref, wd_ref, o_ref, *, chunk):
    x = x_ref[...]
    h = _rms(x, gin_ref[...]).astype(BF16)
    o_ref[...] = x + 0.5 * _rms(_swiglu(h, wgu_ref, wd_ref, chunk), gout_ref[...])


def _ffn_block(x, g_in, g_out, w_gu, w_dn, tm, tf):
    m, d = x.shape
    row = lambda i: (i, 0)
    return pl.pallas_call(
        functools.partial(_ffn_kernel, chunk=tf),
        grid=(m // tm,),
        in_specs=[pl.BlockSpec((tm, d), row), _whole(g_in, 1), _whole(g_out, 1), _whole(w_gu, 1), _whole(w_dn, 1)],
        out_specs=pl.BlockSpec((tm, d), row),
        out_shape=jax.ShapeDtypeStruct((m, d), F32),
        compiler_params=_cparams(("parallel",)),
        name="ffn_block",
    )(x, g_in, g_out, w_gu, w_dn)


def _ab_in_kernel(x_ref, g_ref, w_ref, cos_ref, sup_ref, sdn_ref,
                  sbq_ref, sbk_ref, sbv_ref, dfq_ref, dfk_ref, dfv_ref,
                  sbk_f_ref, sbv_f_ref, dfk_f_ref, dfv_f_ref):
    h = _rms(x_ref[...], g_ref[...]).astype(BF16)
    cos, sup, sdn = cos_ref[...], sup_ref[...], sdn_ref[...]

    def proj(c):
        return _dot(h, w_ref[:, c * SB_W:(c + 1) * SB_W])

    def store_rows(out_f, x, first):
        for j in range(x.shape[-1] // SB_HD):
            out_f[pl.ds(first + j, x.shape[0], stride=SB_W // SB_HD), :] = x[:, j * SB_HD:(j + 1) * SB_HD]

    def rope(x, scale, out_b, out_f):
        for s in range(x.shape[-1] // LANES):
            sl = slice(s * LANES, (s + 1) * LANES)
            r = _rope_slab(x[:, sl], cos, sup, sdn, DF_ROT // 2)
            if out_f is not None:
                store_rows(out_f, r, s * (LANES // SB_HD))
            out_b[:, sl] = (r * scale).astype(BF16)

    sbq_ref[...] = (proj(0) * (SB_HD ** -0.5 * LOG2E)).astype(BF16)
    sbk = proj(1)
    store_rows(sbk_f_ref, sbk, 0)
    sbk_ref[...] = sbk.astype(BF16)
    sbv = proj(2)
    store_rows(sbv_f_ref, sbv, 0)
    sbv_ref[...] = sbv.astype(BF16)
    rope(proj(3), DF_HD ** -0.5 * LOG2E, dfq_ref, None)
    rope(proj(4), 1.0, dfk_ref, dfk_f_ref)
    dfv = proj(5)
    dfv_f_ref[...] = dfv
    dfv_ref[...] = dfv.astype(BF16)


def _ab_in_proj(x, g, w_in, tabs, tm):
    m, d = x.shape
    n_tab = tabs[0].shape[0] // tm
    row = lambda i: (i, 0)
    const = lambda i: (0, 0)
    tab = lambda i: (i % n_tab, 0)
    wide = pl.BlockSpec((tm, SB_W), row)
    bf = jax.ShapeDtypeStruct((m, SB_W), BF16)
    f32 = jax.ShapeDtypeStruct((m, SB_W), F32)
    n_rows = SB_W // SB_HD
    tall = pl.BlockSpec((tm * n_rows, SB_HD), row)
    f32_tall = jax.ShapeDtypeStruct((m * n_rows, SB_HD), F32)
    return pl.pallas_call(
        _ab_in_kernel,
        grid=(m // tm,),
        in_specs=[
            pl.BlockSpec((tm, d), row),
            pl.BlockSpec((1, d), const),
            pl.BlockSpec(w_in.shape, const),
            pl.BlockSpec((tm, LANES), tab),
            pl.BlockSpec((tm, LANES), tab),
            pl.BlockSpec((tm, LANES), tab),
        ],
        out_specs=[wide] * 6 + [tall] * 3 + [wide],
        out_shape=[bf] * 6 + [f32_tall] * 3 + [f32],
        compiler_params=_cparams(("parallel",)),
        name="ab_in_proj",
    )(x, g, w_in, *tabs)


def _post_mixer_kernel(*refs, n_parts, chunk):
    (x_ref, p_ref, g_ref, wo_ref, wgu_ref, wd_ref, wgate_ref, wple_ref) = refs[:8]
    parts = refs[8:8 + n_parts]
    o_ref = refs[8 + n_parts]
    g_mix, g_in, g_out, g_gate, g_ple = (g_ref[i] for i in range(5))
    off = 0
    mix = None
    for p in parts:
        k = p.shape[-1]
        t = _dot(p[...], wo_ref[off:off + k, :])
        mix = t if mix is None else mix + t
        off += k
    x1 = x_ref[...] + _rms(mix, g_mix)
    x2 = x1 + 0.5 * _rms(_swiglu(_rms(x1, g_in).astype(BF16), wgu_ref, wd_ref, chunk), g_out)
    gate = jax.nn.sigmoid(_dot(_rms(x2, g_gate).astype(BF16), wgate_ref[...]))
    emb = _dot(p_ref[...].astype(BF16), wple_ref[...])
    o_ref[...] = x2 + _rms(gate * emb, g_ple)


def _post_mixer(x, p, layer, gains, w_out, w_gu, w_dn, w_gate, w_ple, parts, tm, tf):
    m, d = x.shape
    row = lambda i: (i, 0)
    return pl.pallas_call(
        functools.partial(_post_mixer_kernel, n_parts=len(parts), chunk=tf),
        grid=(m // tm,),
        in_specs=[
            pl.BlockSpec((tm, d), row),
            pl.BlockSpec((None, tm, p.shape[-1]), lambda i: (layer, i, 0)),
        ] + [_whole(a, 1) for a in (gains, w_out, w_gu, w_dn, w_gate, w_ple)]
        + [pl.BlockSpec((tm, a.shape[-1]), row) for a in parts],
        out_specs=pl.BlockSpec((tm, d), row),
        out_shape=jax.ShapeDtypeStruct((m, d), F32),
        compiler_params=_cparams(("parallel",)),
        name="post_mixer",
    )(x, p, gains, w_out, w_gu, w_dn, w_gate, w_ple, *parts)


def _mla_in_kernel(x_ref, g_ref, w_ref, qn_ref, kvn_ref, wuq_ref,
                   qcos_ref, qsup_ref, qsdn_ref, kcos_ref, ksup_ref, ksdn_ref,
                   q_ref, lat_ref, kr_ref):
    h = _rms(x_ref[...], g_ref[...]).astype(BF16)
    proj = _dot(h, w_ref[...])
    c_q = _rms(proj[:, :Q_LORA], qn_ref[...]).astype(BF16)
    lat_ref[...] = _rms(proj[:, Q_LORA:Q_LORA + KV_LORA], kvn_ref[...])
    kr = _rope_slab(proj[:, Q_LORA + KV_LORA:], kcos_ref[...], ksup_ref[...], ksdn_ref[...],
                    MLA_ROPE // 2)
    kr_ref[...] = kr[:, :MLA_ROPE]
    q = _dot(c_q, wuq_ref[...])
    cos, sup, sdn = qcos_ref[...], qsup_ref[...], qsdn_ref[...]
    for hd in range(MLA_HEADS):
        sl = slice(hd * LANES, (hd + 1) * LANES)
        r = _rope_slab(q[:, sl], cos, sup, sdn, MLA_ROPE // 2)
        q_ref[:, sl] = (r * (MLA_QD ** -0.5 * LOG2E)).astype(BF16)


def _mla_in_proj(x, g, w_in, q_norm, kv_norm, w_uq, qtabs, ktabs, tm):
    m, d = x.shape
    n_tab = qtabs[0].shape[0] // tm
    row = lambda i: (i, 0)
    const = lambda i: (0, 0)
    tab = lambda i: (i % n_tab, 0)
    tspec = pl.BlockSpec((tm, LANES), tab)
    return pl.pallas_call(
        _mla_in_kernel,
        grid=(m // tm,),
        in_specs=[
            pl.BlockSpec((tm, d), row),
            pl.BlockSpec((1, d), const),
            pl.BlockSpec(w_in.shape, const),
            pl.BlockSpec((1, Q_LORA), const),
            pl.BlockSpec((1, KV_LORA), const),
            pl.BlockSpec(w_uq.shape, const),
        ] + [tspec] * 6,
        out_specs=[
            pl.BlockSpec((tm, MLA_QW), row),
            pl.BlockSpec((tm, KV_LORA), row),
            pl.BlockSpec((tm, MLA_ROPE), row),
        ],
        out_shape=[
            jax.ShapeDtypeStruct((m, MLA_QW), BF16),
            jax.ShapeDtypeStruct((m, KV_LORA), F32),
            jax.ShapeDtypeStruct((m, MLA_ROPE), F32),
        ],
        compiler_params=_cparams(("parallel",)),
        name="mla_in_proj",
    )(x, g, w_in, q_norm, kv_norm, w_uq, *qtabs, *ktabs)


def _mla_kv_kernel(lat_ref, kr_ref, wuk_ref, place_ref, wuv_ref, k_ref, v_ref):
    lat = lat_ref[...].astype(BF16)
    k_ref[...] = (_dot(lat, wuk_ref[...]) + _dot(kr_ref[...].astype(BF16), place_ref[...])).astype(BF16)
    v_ref[...] = _dot(lat, wuv_ref[...]).astype(BF16)


def _mla_kv_expand(lat, kr, w_uk, place, w_uv, tm):
    m = lat.shape[0]
    row = lambda i: (i, 0)
    const = lambda i: (0, 0)
    return pl.pallas_call(
        _mla_kv_kernel,
        grid=(m // tm,),
        in_specs=[
            pl.BlockSpec((tm, KV_LORA), row),
            pl.BlockSpec((tm, MLA_ROPE), row),
            pl.BlockSpec(w_uk.shape, const),
            pl.BlockSpec(place.shape, const),
            pl.BlockSpec(w_uv.shape, const),
        ],
        out_specs=[pl.BlockSpec((tm, MLA_QW), row), pl.BlockSpec((tm, MLA_VW), row)],
        out_shape=[jax.ShapeDtypeStruct((m, MLA_QW), BF16), jax.ShapeDtypeStruct((m, MLA_VW), BF16)],
        compiler_params=_cparams(("parallel",)),
        name="mla_kv_expand",
    )(lat, kr, w_uk, place, w_uv)


def _positions(q0, kb, tq, tk):
    t = q0 + lax.broadcasted_iota(jnp.int32, (tq, tk), 0)
    s = kb * tk + lax.broadcasted_iota(jnp.int32, (tq, tk), 1)
    return t, s


def _sb_kernel(q_ref, k_ref, v_ref, o_ref, *, tq, tk, q_start):
    q0 = q_start + pl.program_id(2) * tq
    n_full = q0 // tk
    q = q_ref[0]
    groups = q.shape[-1] // LANES
    lane = lax.broadcasted_iota(jnp.int32, (tq, LANES), 1)
    q_heads = []
    for g in range(groups):
        qg = q[:, g * LANES:(g + 1) * LANES]
        q_heads += [jnp.where(lane < SB_HD, qg, jnp.zeros_like(qg)),
                    jnp.where(lane >= SB_HD, qg, jnp.zeros_like(qg))]
    group_of = lambda h: slice((h // 2) * LANES, (h // 2 + 1) * LANES)
    later = (lax.broadcasted_iota(jnp.int32, (tk, tk), 0)
             > lax.broadcasted_iota(jnp.int32, (tk, tk), 1)).astype(BF16)

    def load(kb):
        start = pl.multiple_of(kb * tk, tk)
        return k_ref[0, pl.ds(start, tk), :], v_ref[0, pl.ds(start, tk), :]

    def miss_cost(z):
        cost = jnp.maximum(z, 0.0) + jnp.log2(1.0 + jnp.exp2(-jnp.abs(z)))
        return cost, z - cost

    def cost_after(cost):
        hi = cost.astype(BF16)
        lo = (cost - hi.astype(F32)).astype(BF16)
        return _dot(hi, later) + _dot(lo, later)

    def block_total(cost, after):
        return after[:, 0:1] + cost[:, 0:1]

    lower = jnp.maximum(n_full - 1, 0)
    (k_lo, v_lo), (k_up, v_up) = load(lower), load(lower + 1)
    t, s_lo = _positions(q0, lower, tq, tk)
    earlier_lo = s_lo < t
    earlier_up = s_lo + tk < t
    state = []
    for h, qh in enumerate(q_heads):
        sl = group_of(h)
        cost_up, hit_up = miss_cost(_dot_t(qh, k_up[:, sl]))
        cost_lo, hit_lo = miss_cost(_dot_t(qh, k_lo[:, sl]))
        cost_up = jnp.where(earlier_up, cost_up, 0.0)
        cost_lo = jnp.where(earlier_lo, cost_lo, 0.0)
        after_up, after_lo = cost_after(cost_up), cost_after(cost_lo)
        run_up = block_total(cost_up, after_up)
        w_up = jnp.where(earlier_up, jnp.exp2(hit_up - after_up), 0.0)
        w_lo = jnp.where(earlier_lo, jnp.exp2(hit_lo - after_lo - run_up), 0.0)
        acc = _dot(w_up.astype(BF16), v_up[:, sl]) + _dot(w_lo.astype(BF16), v_lo[:, sl])
        state.append((run_up + block_total(cost_lo, after_lo), acc))

    def least_run(st):
        least = st[0][0]
        for run, _ in st[1:]:
            least = jnp.minimum(least, run)
        return jnp.min(least)

    def more(c):
        kb, least, _ = c
        return (kb >= 0) & (least < SB_UNDERFLOW)

    def block(c):
        kb, _, st = c
        ks, vs = load(kb)
        new = []
        for h, (qh, (run, acc)) in enumerate(zip(q_heads, st)):
            cost, hit = miss_cost(_dot_t(qh, ks[:, group_of(h)]))
            after = cost_after(cost)
            acc = acc + _dot(jnp.exp2(hit - after - run).astype(BF16), vs[:, group_of(h)])
            new.append((run + block_total(cost, after), acc))
        return kb - 1, least_run(new), tuple(new)

    _, _, state = lax.while_loop(more, block, (lower - 1, least_run(state), tuple(state)))
    for g in range(groups):
        o_ref[0, :, g * LANES:(g + 1) * LANES] = jnp.where(
            lane < SB_HD, state[2 * g][1], state[2 * g + 1][1]).astype(o_ref.dtype)


def _softmax_bucketed(chains, k_ref, v_ref, q0, emit, *, tq, width, q_range, n_keys):
    bucket = q0 // width
    for j in range(q_range[0] // width, (q_range[1] - 1) // width + 1):
        @pl.when(bucket == j)
        def _(j=j):
            length = (j + 1) * width
            t = q0 + lax.broadcasted_iota(jnp.int32, (tq, width), 0)
            s_pos = length - width + lax.broadcasted_iota(jnp.int32, (tq, width), 1)
            vis = ((s_pos // CHUNK) <= (t // CHUNK)) & (s_pos < n_keys)
            outs = []
            for qc, k_sl, v_sl in chains:
                s = _dot_t(qc, k_ref[0, 0:length, k_sl])
                tail = jnp.where(vis, s[:, length - width:], -jnp.inf)
                s = tail if length == width else jnp.concatenate([s[:, :length - width], tail], axis=1)
                p = jnp.exp2(s - jnp.max(s, axis=-1, keepdims=True))
                l = jnp.sum(p, axis=-1, keepdims=True)
                outs.append(_dot(p.astype(BF16), v_ref[0, 0:length, v_sl]) / l)
            emit(outs)


def _lane_slab(i):
    return slice(i * LANES, (i + 1) * LANES)


def _df_kernel(lam_ref, sub_ref, q_ref, k_ref, v_ref, o_ref, *, tq, width, q_range, n_keys, lam_init):
    q0 = q_range[0] + pl.program_id(2) * tq
    q = q_ref[0]
    heads = q.shape[-1] // LANES
    lane = lax.broadcasted_iota(jnp.int32, (tq, LANES), 1)
    chains = []
    for h in range(heads):
        qh = q[:, _lane_slab(h)]
        chains.append((jnp.where(lane < DF_HD, qh, jnp.zeros_like(qh)), _lane_slab(h), _lane_slab(h)))
        chains.append((jnp.where(lane >= DF_HD, qh, jnp.zeros_like(qh)), _lane_slab(h), _lane_slab(h)))

    def emit(outs):
        lv = lam_ref[...]
        lam = (jnp.exp(jnp.sum(lv[0:1] * lv[1:2], axis=-1, keepdims=True))
               - jnp.exp(jnp.sum(lv[2:3] * lv[3:4], axis=-1, keepdims=True)) + lam_init)
        for h in range(heads):
            out = outs[2 * h] - lam * outs[2 * h + 1]
            o_ref[0, :, _lane_slab(h)] = (_rms(out, sub_ref[...]) * (1.0 - lam_init)).astype(o_ref.dtype)

    _softmax_bucketed(chains, k_ref, v_ref, q0, emit, tq=tq, width=width, q_range=q_range, n_keys=n_keys)


def _mla_kernel(q_ref, k_ref, v_ref, o_ref, *, tq, width, q_range, n_keys):
    q0 = q_range[0] + pl.program_id(2) * tq
    q = q_ref[0]
    heads = q.shape[-1] // LANES
    lane = lax.broadcasted_iota(jnp.int32, (tq, LANES), 1)
    chains = [(q[:, _lane_slab(h)], _lane_slab(h), _lane_slab(h // 2)) for h in range(heads)]

    def emit(outs):
        for g in range(heads // 2):
            o_ref[0, :, _lane_slab(g)] = jnp.where(
                lane < MLA_VD, outs[2 * g], outs[2 * g + 1]).astype(o_ref.dtype)

    _softmax_bucketed(chains, k_ref, v_ref, q0, emit, tq=tq, width=width, q_range=q_range, n_keys=n_keys)


def _mla_cached_kernel(q_ref, latc_ref, krc_ref, latn_ref, krn_ref, wuk_ref, place_ref, wuv_ref, o_ref,
                       *, q_start):
    q = q_ref[0]
    t = q.shape[0]
    slabs = [slice(h * LANES, (h + 1) * LANES) for h in range(MLA_HEADS)]
    q_all = jnp.concatenate([q[:, sl] for sl in slabs], axis=0)
    q_lat = jnp.concatenate([_dot_t(q[:, sl], wuk_ref[:, sl]) for sl in slabs], axis=0).astype(BF16)
    rows = MLA_HEADS * t
    pad = LANES - t

    def scores(lat, kr):
        kr_lanes = _dot(kr.astype(BF16), place_ref[...]).astype(BF16)
        return _dot_t(q_lat, lat) + _dot_t(q_all, kr_lanes)

    lat_c = latc_ref[0].astype(BF16)
    s_c = scores(lat_c, krc_ref[0])
    lat_n = jnp.concatenate([latn_ref[0], jnp.zeros((pad, KV_LORA), F32)], axis=0).astype(BF16)
    kr_n = jnp.concatenate([krn_ref[0], jnp.zeros((pad, MLA_ROPE), F32)], axis=0)
    t_pos = q_start + lax.broadcasted_iota(jnp.int32, (rows, LANES), 0) % t
    j = lax.broadcasted_iota(jnp.int32, (rows, LANES), 1)
    vis = (j < t) & (((q_start + j) // CHUNK) <= (t_pos // CHUNK))
    s_n = jnp.where(vis, scores(lat_n, kr_n), -jnp.inf)

    m = jnp.maximum(jnp.max(s_c, axis=-1, keepdims=True), jnp.max(s_n, axis=-1, keepdims=True))
    p_c, p_n = jnp.exp2(s_c - m), jnp.exp2(s_n - m)
    l = jnp.sum(p_c, axis=-1, keepdims=True) + jnp.sum(p_n, axis=-1, keepdims=True)
    o_lat = (_dot(p_c.astype(BF16), lat_c) + _dot(p_n.astype(BF16), lat_n)) / l
    full = _dot(o_lat.astype(BF16), wuv_ref[...])
    lane_head = lax.broadcasted_iota(jnp.int32, (t, MLA_VW), 1) // MLA_VD
    out = jnp.zeros((t, MLA_VW), F32)
    for h in range(MLA_HEADS):
        out = out + jnp.where(lane_head == h, full[h * t:(h + 1) * t, :], 0.0)
    o_ref[0] = out.astype(o_ref.dtype)


def _mla_cached_attention(q, lat_cache, kr_cache, lat_new, kr_new, w_uk, place, w_uv, q_start):
    b, t, _ = q.shape
    past = lat_cache.shape[1]
    assert t <= LANES and q_start == past
    per_b = lambda bi: (bi, 0, 0)
    const = lambda bi: (0, 0)
    return pl.pallas_call(
        functools.partial(_mla_cached_kernel, q_start=q_start),
        grid=(b,),
        in_specs=[
            pl.BlockSpec((1, t, MLA_QW), per_b),
            pl.BlockSpec((1, past, KV_LORA), per_b),
            pl.BlockSpec((1, past, MLA_ROPE), per_b),
            pl.BlockSpec((1, t, KV_LORA), per_b),
            pl.BlockSpec((1, t, MLA_ROPE), per_b),
            pl.BlockSpec(w_uk.shape, const),
            pl.BlockSpec(place.shape, const),
            pl.BlockSpec(w_uv.shape, const),
        ],
        out_specs=pl.BlockSpec((1, t, MLA_VW), per_b),
        out_shape=jax.ShapeDtypeStruct((b, t, MLA_VW), BF16),
        compiler_params=_cparams(("parallel",)),
        name="mla_cached_attn",
    )(q, lat_cache, kr_cache, lat_new, kr_new, w_uk, place, w_uv)


def _attention(kernel_fn, q, k, v, extra, *, q_lanes, k_lanes, tq, tk, name, v_groups=1):
    b, t_q, _ = q.shape
    t_k = k.shape[1]
    groups = v.shape[-1] // (LANES * v_groups)
    q_lanes, k_lanes, v_lanes = q_lanes * v_groups, k_lanes * v_groups, LANES * v_groups
    assert t_q % tq == 0 and t_k % tk == 0 and tk % tq == 0 and tk % CHUNK == 0
    const = lambda bi, g, qi: (0, 0)
    return pl.pallas_call(
        kernel_fn,
        grid=(b, groups, t_q // tq),
        in_specs=[pl.BlockSpec(e.shape, const) for e in extra] + [
            pl.BlockSpec((1, tq, q_lanes), lambda bi, g, qi: (bi, qi, g)),
            pl.BlockSpec((1, t_k, k_lanes), lambda bi, g, qi: (bi, 0, g)),
            pl.BlockSpec((1, t_k, v_lanes), lambda bi, g, qi: (bi, 0, g)),
        ],
        out_specs=pl.BlockSpec((1, tq, v_lanes), lambda bi, g, qi: (bi, qi, g)),
        out_shape=jax.ShapeDtypeStruct((b, t_q, groups * v_lanes), BF16),
        compiler_params=_cparams(("parallel", "parallel", "arbitrary")),
        name=name,
    )(*extra, q, k, v)


def _rope_tables(pos, rot, theta, period, offset):
    half = rot // 2
    inv = jnp.float32(theta) ** (-(jnp.arange(half, dtype=F32) * (2.0 / rot)))
    ang = pos.astype(F32)[:, None] * inv[None, :]
    cos, sin = jnp.cos(ang), jnp.sin(ang)
    lane = jnp.arange(LANES) % period - offset
    first = (lane >= 0) & (lane < half)
    second = (lane >= half) & (lane < rot)
    idx = jnp.clip(jnp.where(second, lane - half, lane), 0, half - 1)
    cos_l, sin_l = cos[:, idx], sin[:, idx]
    cos_t = jnp.where(first | second, cos_l, 1.0)
    sin_up = jnp.where(first, -sin_l, 0.0)
    sin_dn = jnp.where(second, sin_l, 0.0)
    return cos_t, sin_up, sin_dn


def _pad_heads(w, heads, width):
    k = w.shape[0]
    w = w.reshape(k, heads, width)
    return jnp.pad(w, ((0, 0), (0, 0), (0, LANES - width))).reshape(k, heads * LANES)


def _with_cache(cache, new, t_pad, dtype):
    b, past, w = cache.shape
    pad = jnp.zeros((b, t_pad - past - new.shape[1], w), dtype)
    return jnp.concatenate([cache.astype(dtype), new.astype(dtype), pad], axis=1)


def _trunk(x, p, start, caches, W, *, tm, tf, tq, width, sb_tq, sb_tk, sb_groups, df_groups, mla_groups):
    b, t, d = x.shape
    m = b * t
    depth = p.shape[0]
    x = x.reshape(m, d)
    pos = start + jnp.arange(t)
    if t % tm == 0:
        tab_pos = pos
    else:
        tab_pos = jnp.tile(pos, m // t)
    df_tabs = _rope_tables(tab_pos, DF_ROT, ROPE_THETA, DF_HD, 0)
    mq_tabs = _rope_tables(tab_pos, MLA_ROPE, MLA_THETA, LANES, MLA_NOPE)
    mk_tabs = _rope_tables(tab_pos, MLA_ROPE, MLA_THETA, LANES, 0)
    n_keys = start + t
    t_keys = -(-n_keys // width) * width
    assert t_keys % sb_tk == 0
    q_range = (start, start + t)

    ab_rows, mla_rows = [], []
    for i in range(depth):
        g = W['norms'][i][:, None, :]
        j = i // 2
        x = _ffn_block(x, g[0], g[1], W['ffn1_gu'][i], W['ffn1_dn'][i], tm, tf)
        if i % 2 == 0:
            (sbq, sbk, sbv, dfq, dfk, dfv, sbk_f, sbv_f, dfk_f, dfv_f) = _ab_in_proj(
                x, g[2], W['ab_in'][j], df_tabs, tm)
            ab_rows.append((sbk_f.reshape(b, t, SB_HEADS, SB_HD), sbv_f.reshape(b, t, SB_HEADS, SB_HD),
                            dfk_f.reshape(b, t, DF_HEADS, 2, DF_HD), dfv_f.reshape(b, t, DF_HEADS, 2 * DF_HD)))
            three = lambda a: a.reshape(b, t, a.shape[-1])
            if caches is None:
                keys = [three(a) for a in (sbk, sbv, dfk, dfv)]
            else:
                keys = [_with_cache(c[j].reshape(b, start, -1), n.reshape(b, t, -1), t_keys, BF16)
                        for c, n in zip(caches[:4], (sbk_f, sbv_f, dfk_f, dfv_f))]
            sb_out = _attention(
                functools.partial(_sb_kernel, tq=sb_tq, tk=sb_tk, q_start=start),
                three(sbq), keys[0], keys[1], [], q_lanes=LANES, k_lanes=LANES, tq=sb_tq, tk=sb_tk,
                name="sb_attn", v_groups=sb_groups)
            lam_init = 0.8 - 0.6 * math.exp(-0.3 * i)
            df_out = _attention(
                functools.partial(_df_kernel, tq=tq, width=width, q_range=q_range, n_keys=n_keys,
                                  lam_init=lam_init),
                three(dfq), keys[2], keys[3], [W['df_lambda'][j], W['df_subln'][j][None, :]],
                q_lanes=LANES, k_lanes=LANES, tq=tq, tk=width, name="df_attn", v_groups=df_groups)
            parts = [sb_out.reshape(m, SB_W), df_out.reshape(m, DF_W)]
            w_out = W['ab_out'][j]
        else:
            q, lat, kr = _mla_in_proj(x, g[2], W['mla_in'][j], W['mla_q_norm'][j][None, :],
                                      W['mla_kv_norm'][j][None, :], W['mla_uq'][j], mq_tabs, mk_tabs, tm)
            mla_rows.append((lat.reshape(b, t, KV_LORA), kr.reshape(b, t, MLA_ROPE)))
            if caches is None:
                k_all, v_all = _mla_kv_expand(lat, kr, W['mla_uk'][j], W['mla_place'], W['mla_uv'][j], tm)
                out = _attention(
                    functools.partial(_mla_kernel, tq=tq, width=width, q_range=q_range, n_keys=n_keys),
                    q.reshape(b, t, MLA_QW), k_all.reshape(b, -1, MLA_QW), v_all.reshape(b, -1, MLA_VW), [],
                    q_lanes=2 * LANES, k_lanes=2 * LANES, tq=tq, tk=width, name="mla_attn",
                    v_groups=mla_groups)
            else:
                out = _mla_cached_attention(
                    q.reshape(b, t, MLA_QW), caches[4][j], caches[5][j], lat.reshape(b, t, KV_LORA),
                    kr.reshape(b, t, MLA_ROPE), W['mla_uk'][j], W['mla_place'][:, :LANES], W['mla_uv'][j], start)
            parts = [out.reshape(m, MLA_VW)]
            w_out = W['mla_out'][j]
        x = _post_mixer(x, p.reshape(depth, m, -1), i, g[3:8], w_out, W['ffn2_gu'][i], W['ffn2_dn'][i],
                        W['ple_gate'][i], W['ple_in'][i], parts, tm, tf)
    sb_k, sb_v, df_k, df_v = [jnp.stack(r, axis=0) for r in zip(*ab_rows)]
    lat, kr = [jnp.stack(r, axis=0) for r in zip(*mla_rows)]
    return x.reshape(b, t, d), sb_k, sb_v, df_k, df_v, lat, kr


def _pick(n, candidates):
    for c in candidates:
        if n % c == 0:
            return c
    return n


def _cast_kernel(x_ref, o_ref):
    o_ref[...] = x_ref[...].astype(o_ref.dtype)


def _to_bf16(w):
    n_l, k, n = w.shape
    bk = _pick(k, (256, 128, 64, 32, 16))
    spec = pl.BlockSpec((1, bk, n), lambda l, i: (l, i, 0))
    return pl.pallas_call(
        _cast_kernel, grid=(n_l, k // bk), in_specs=[spec], out_specs=spec,
        out_shape=jax.ShapeDtypeStruct(w.shape, BF16),
        compiler_params=_cparams(("parallel", "parallel")), name="to_bf16",
    )(w)


def kernel(x_prompt, x_sample, p_prompt, p_sample, cache_sb_k, cache_sb_v, cache_df_k, cache_df_v,
           cache_mla_latent, cache_mla_krope, norms, ffn1_gu, ffn1_dn, ffn2_gu, ffn2_dn, ple_in, ple_gate,
           ab_in, ab_out, df_lambda, df_subln, mla_in, mla_q_norm, mla_kv_norm, mla_uq, mla_uk, mla_uv, mla_out):
    bf = lambda a: a.astype(BF16)
    n_odd = mla_in.shape[0]
    place = jnp.zeros((MLA_ROPE, MLA_HEADS, LANES), F32)
    place = place.at[jnp.arange(MLA_ROPE), :, MLA_NOPE + jnp.arange(MLA_ROPE)].set(1.0)
    W = dict(
        norms=norms, ffn1_gu=_to_bf16(ffn1_gu), ffn1_dn=_to_bf16(ffn1_dn), ffn2_gu=_to_bf16(ffn2_gu),
        ffn2_dn=_to_bf16(ffn2_dn), ple_in=bf(ple_in), ple_gate=_to_bf16(ple_gate), ab_in=_to_bf16(ab_in),
        ab_out=_to_bf16(ab_out),
        df_lambda=df_lambda, df_subln=df_subln,
        mla_in=bf(jnp.pad(mla_in, ((0, 0), (0, 0), (0, MLA_IN_PAD - mla_in.shape[-1])))),
        mla_q_norm=mla_q_norm, mla_kv_norm=mla_kv_norm,
        mla_uq=bf(jnp.stack([_pad_heads(mla_uq[j], MLA_HEADS, MLA_QD) for j in range(n_odd)])),
        mla_uk=bf(jnp.stack([_pad_heads(mla_uk[j], MLA_HEADS, MLA_NOPE) for j in range(n_odd)])),
        mla_uv=bf(mla_uv), mla_out=bf(mla_out),
        mla_place=bf(place.reshape(MLA_ROPE, MLA_QW)),
    )
    d_ff = ffn1_dn.shape[1]
    tf = _pick(d_ff, (1408, 256, 128))

    t_p = x_prompt.shape[1]
    m_p = x_prompt.shape[0] * t_p
    width_p = _pick(t_p, (BUCKET, 512, 256, 128, 64))
    tq_p = _pick(width_p, (256, 128, 64))
    out_p = _trunk(x_prompt, p_prompt, 0, None, W, tm=_pick(m_p, (512, 256, 128)), tf=tf, tq=tq_p, width=width_p,
                   sb_tq=tq_p, sb_tk=tq_p, sb_groups=2, df_groups=2, mla_groups=2)

    caches = (cache_sb_k, cache_sb_v, cache_df_k, cache_df_v, cache_mla_latent, cache_mla_krope)
    past = cache_sb_k.shape[2]
    t_s = x_sample.shape[1]
    m_s = x_sample.shape[0] * t_s
    tk_s = _pick(past, (256, 128, 64))
    out_s = _trunk(x_sample, p_sample, past, caches, W, tm=_pick(m_s, (128,)), tf=tf, tq=t_s, width=tk_s,
                   sb_tq=t_s, sb_tk=tk_s, sb_groups=SB_HEADS // 2, df_groups=DF_HEADS, mla_groups=MLA_HEADS // 2)

    (y_p, sb_k_p, sb_v_p, df_k_p, df_v_p, lat_p, kr_p) = out_p
    (y_s, sb_k_s, sb_v_s, df_k_s, df_v_s, lat_s, kr_s) = out_s
    return (y_p, y_s, sb_k_p, sb_v_p, df_k_p, df_v_p, lat_p, kr_p,
            sb_k_s, sb_v_s, df_k_s, df_v_s, lat_s, kr_s)
```

```python
import functools
import math

import jax
import jax.numpy as jnp
from jax import lax
from jax.experimental import pallas as pl
from jax.experimental.pallas import tpu as pltpu

F32 = jnp.float32
BF16 = jnp.bfloat16

EPS = 1e-6
CHUNK = 64
SB_HEADS, SB_HD = 8, 64
DF_HEADS, DF_HD = 4, 64
DF_ROT = DF_HD // 4
ROPE_THETA = 500000.0
MLA_HEADS, MLA_NOPE, MLA_ROPE, MLA_VD = 16, 64, 32, 64
Q_LORA, KV_LORA = 384, 256
MLA_THETA = 10000.0
MLA_QD = MLA_NOPE + MLA_ROPE
LOG2E = math.log2(math.e)
SB_UNDERFLOW = 151.0
BUCKET = 512

LANES = 128
SB_W = SB_HEADS * SB_HD
DF_W = DF_HEADS * 2 * DF_HD
MLA_QW = MLA_HEADS * LANES
MLA_VW = MLA_HEADS * MLA_VD
MLA_IN_PAD = 768

VMEM_LIMIT = 56 * 1024 * 1024


def _cparams(sem):
    return pltpu.CompilerParams(dimension_semantics=sem, vmem_limit_bytes=VMEM_LIMIT)


def _rms(x, g):
    return x * lax.rsqrt(jnp.mean(x * x, axis=-1, keepdims=True) + EPS) * g


def _dot(a, b):
    return jnp.dot(a, b, preferred_element_type=F32)


def _dot_t(a, b):
    return lax.dot_general(a, b, (((1,), (1,)), ((), ())), preferred_element_type=F32)


def _rope_slab(x, cos, sin_up, sin_dn, shift):
    n = x.shape[-1]
    return x * cos + pltpu.roll(x, n - shift, 1) * sin_up + pltpu.roll(x, shift, 1) * sin_dn


def _swiglu(h, wgu_ref, wd_ref, chunk):
    d_ff = wd_ref.shape[0]
    acc = None
    for c in range(d_ff // chunk):
        g = _dot(h, wgu_ref[:, c * chunk:(c + 1) * chunk])
        u = _dot(h, wgu_ref[:, d_ff + c * chunk:d_ff + (c + 1) * chunk])
        t = _dot((g * jax.nn.sigmoid(g) * u).astype(BF16), wd_ref[c * chunk:(c + 1) * chunk, :])
        acc = t if acc is None else acc + t
    return acc


def _whole(a, n_grid):
    return pl.BlockSpec(a.shape, lambda *_: (0,) * a.ndim, pipeline_mode=pl.Buffered(1))


def _ffn_kernel(x_ref, gin_ref, gout_ref, wgu_ref, wd_ref, o_ref, *, chunk):
    x = x_ref[...]
    h = _rms(x, gin_ref[...]).astype(BF16)
    o_ref[...] = x + 0.5 * _rms(_swiglu(h, wgu_ref, wd_ref, chunk), gout_ref[...])


def _ffn_block(x, g_in, g_out, w_gu, w_dn, tm, tf):
    m, d = x.shape
    row = lambda i: (i, 0)
    return pl.pallas_call(
        functools.partial(_ffn_kernel, chunk=tf),
        grid=(m // tm,),
        in_specs=[pl.BlockSpec((tm, d), row), _whole(g_in, 1), _whole(g_out, 1), _whole(w_gu, 1), _whole(w_dn, 1)],
        out_specs=pl.BlockSpec((tm, d), row),
        out_shape=jax.ShapeDtypeStruct((m, d), F32),
        compiler_params=_cparams(("parallel",)),
        name="ffn_block",
    )(x, g_in, g_out, w_gu, w_dn)


def _ab_in_kernel(x_ref, g_ref, w_ref, cos_ref, sup_ref, sdn_ref,
                  sbq_ref, sbk_ref, sbv_ref, dfq_ref, dfk_ref, dfv_ref,
                  sbk_f_ref, sbv_f_ref, dfk_f_ref, dfv_f_ref):
    h = _rms(x_ref[...], g_ref[...]).astype(BF16)
    cos, sup, sdn = cos_ref[...], sup_ref[...], sdn_ref[...]

    def proj(c):
        return _dot(h, w_ref[:, c * SB_W:(c + 1) * SB_W])

    def store_rows(out_f, x, first):
        for j in range(x.shape[-1] // SB_HD):
            out_f[pl.ds(first + j, x.shape[0], stride=SB_W // SB_HD), :] = x[:, j * SB_HD:(j + 1) * SB_HD]

    def rope(x, scale, out_b, out_f):
        for s in range(x.shape[-1] // LANES):
            sl = slice(s * LANES, (s + 1) * LANES)
            r = _rope_slab(x[:, sl], cos, sup, sdn, DF_ROT // 2)
            if out_f is not None:
                store_rows(out_f, r, s * (LANES // SB_HD))
            out_b[:, sl] = (r * scale).astype(BF16)

    sbq_ref[...] = (proj(0) * (SB_HD ** -0.5 * LOG2E)).astype(BF16)
    sbk = proj(1)
    store_rows(sbk_f_ref, sbk, 0)
    sbk_ref[...] = sbk.astype(BF16)
    sbv = proj(2)
    store_rows(sbv_f_ref, sbv, 0)
    sbv_ref[...] = sbv.astype(BF16)
    rope(proj(3), DF_HD ** -0.5 * LOG2E, dfq_ref, None)
    rope(proj(4), 1.0, dfk_ref, dfk_f_ref)
    dfv = proj(5)
    dfv_f_ref[...] = dfv
    dfv_ref[...] = dfv.astype(BF16)


def _ab_in_proj(x, g, w_in, tabs, tm):
    m, d = x.shape
    n_tab = tabs[0].shape[0] // tm
    row = lambda i: (i, 0)
    const = lambda i: (0, 0)
    tab = lambda i: (i % n_tab, 0)
    wide = pl.BlockSpec((tm, SB_W), row)
    bf = jax.ShapeDtypeStruct((m, SB_W), BF16)
    f32 = jax.ShapeDtypeStruct((m, SB_W), F32)
    n_rows = SB_W // SB_HD
    tall = pl.BlockSpec((tm * n_rows, SB_HD), row)
    f32_tall = jax.ShapeDtypeStruct((m * n_rows, SB_HD), F32)
    return pl.pallas_call(
        _ab_in_kernel,
        grid=(m // tm,),
        in_specs=[
            pl.BlockSpec((tm, d), row),
            pl.BlockSpec((1, d), const),
            pl.BlockSpec(w_in.shape, const),
            pl.BlockSpec((tm, LANES), tab),
            pl.BlockSpec((tm, LANES), tab),
            pl.BlockSpec((tm, LANES), tab),
        ],
        out_specs=[wide] * 6 + [tall] * 3 + [wide],
        out_shape=[bf] * 6 + [f32_tall] * 3 + [f32],
        compiler_params=_cparams(("parallel",)),
        name="ab_in_proj",
    )(x, g, w_in, *tabs)


def _post_mixer_kernel(*refs, n_parts, chunk):
    (x_ref, p_ref, g_ref, wo_ref, wgu_ref, wd_ref, wgate_ref, wple_ref) = refs[:8]
    parts = refs[8:8 + n_parts]
    o_ref = refs[8 + n_parts]
    g_mix, g_in, g_out, g_gate, g_ple = (g_ref[i] for i in range(5))
    off = 0
    mix = None
    for p in parts:
        k = p.shape[-1]
        t = _dot(p[...], wo_ref[off:off + k, :])
        mix = t if mix is None else mix + t
        off += k
    x1 = x_ref[...] + _rms(mix, g_mix)
    x2 = x1 + 0.5 * _rms(_swiglu(_rms(x1, g_in).astype(BF16), wgu_ref, wd_ref, chunk), g_out)
    gate = jax.nn.sigmoid(_dot(_rms(x2, g_gate).astype(BF16), wgate_ref[...]))
    emb = _dot(p_ref[...].astype(BF16), wple_ref[...])
    o_ref[...] = x2 + _rms(gate * emb, g_ple)


def _post_mixer(x, p, layer, gains, w_out, w_gu, w_dn, w_gate, w_ple, parts, tm, tf):
    m, d = x.shape
    row = lambda i: (i, 0)
    return pl.pallas_call(
        functools.partial(_post_mixer_kernel, n_parts=len(parts), chunk=tf),
        grid=(m // tm,),
        in_specs=[
            pl.BlockSpec((tm, d), row),
            pl.BlockSpec((None, tm, p.shape[-1]), lambda i: (layer, i, 0)),
        ] + [_whole(a, 1) for a in (gains, w_out, w_gu, w_dn, w_gate, w_ple)]
        + [pl.BlockSpec((tm, a.shape[-1]), row) for a in parts],
        out_specs=pl.BlockSpec((tm, d), row),
        out_shape=jax.ShapeDtypeStruct((m, d), F32),
        compiler_params=_cparams(("parallel",)),
        name="post_mixer",
    )(x, p, gains, w_out, w_gu, w_dn, w_gate, w_ple, *parts)


def _mla_in_kernel(x_ref, g_ref, w_ref, qn_ref, kvn_ref, wuq_ref,
                   qcos_ref, qsin_ref, kcos_ref, ksup_ref, ksdn_ref,
                   q_ref, lat_ref, kr_ref):
    h = _rms(x_ref[...], g_ref[...]).astype(BF16)
    proj = _dot(h, w_ref[...])
    c_q = _rms(proj[:, :Q_LORA], qn_ref[...]).astype(BF16)
    lat_ref[...] = _rms(proj[:, Q_LORA:Q_LORA + KV_LORA], kvn_ref[...])
    kr = _rope_slab(proj[:, Q_LORA + KV_LORA:], kcos_ref[...], ksup_ref[...], ksdn_ref[...],
                    MLA_ROPE // 2)
    kr_ref[...] = kr[:, :MLA_ROPE]
    q2 = _dot(c_q, wuq_ref[...])
    cos, sin = qcos_ref[...], qsin_ref[...]
    for hd in range(MLA_HEADS):
        sl = slice(hd * LANES, (hd + 1) * LANES)
        rot = slice(MLA_QW + hd * LANES, MLA_QW + (hd + 1) * LANES)
        r = q2[:, sl] * cos + q2[:, rot] * sin
        q_ref[:, sl] = (r * (MLA_QD ** -0.5 * LOG2E)).astype(BF16)


def _mla_in_proj(x, g, w_in, q_norm, kv_norm, w_uq, qtabs, ktabs, tm):
    m, d = x.shape
    n_tab = qtabs[0].shape[0] // tm
    row = lambda i: (i, 0)
    const = lambda i: (0, 0)
    tab = lambda i: (i % n_tab, 0)
    tspec = pl.BlockSpec((tm, LANES), tab)
    return pl.pallas_call(
        _mla_in_kernel,
        grid=(m // tm,),
        in_specs=[
            pl.BlockSpec((tm, d), row),
            pl.BlockSpec((1, d), const),
            pl.BlockSpec(w_in.shape, const),
            pl.BlockSpec((1, Q_LORA), const),
            pl.BlockSpec((1, KV_LORA), const),
            pl.BlockSpec(w_uq.shape, const),
        ] + [tspec] * 5,
        out_specs=[
            pl.BlockSpec((tm, MLA_QW), row),
            pl.BlockSpec((tm, KV_LORA), row),
            pl.BlockSpec((tm, MLA_ROPE), row),
        ],
        out_shape=[
            jax.ShapeDtypeStruct((m, MLA_QW), BF16),
            jax.ShapeDtypeStruct((m, KV_LORA), F32),
            jax.ShapeDtypeStruct((m, MLA_ROPE), F32),
        ],
        compiler_params=_cparams(("parallel",)),
        name="mla_in_proj",
    )(x, g, w_in, q_norm, kv_norm, w_uq, *qtabs, *ktabs)


def _mla_kv_kernel(lat_ref, kr_ref, wuk_ref, place_ref, wuv_ref, k_ref, v_ref):
    lat = lat_ref[...].astype(BF16)
    k_ref[...] = (_dot(lat, wuk_ref[...]) + _dot(kr_ref[...].astype(BF16), place_ref[...])).astype(BF16)
    v_ref[...] = _dot(lat, wuv_ref[...]).astype(BF16)


def _mla_kv_expand(lat, kr, w_uk, place, w_uv, tm):
    m = lat.shape[0]
    row = lambda i: (i, 0)
    const = lambda i: (0, 0)
    return pl.pallas_call(
        _mla_kv_kernel,
        grid=(m // tm,),
        in_specs=[
            pl.BlockSpec((tm, KV_LORA), row),
            pl.BlockSpec((tm, MLA_ROPE), row),
            pl.BlockSpec(w_uk.shape, const),
            pl.BlockSpec(place.shape, const),
            pl.BlockSpec(w_uv.shape, const),
        ],
        out_specs=[pl.BlockSpec((tm, MLA_QW), row), pl.BlockSpec((tm, MLA_VW), row)],
        out_shape=[jax.ShapeDtypeStruct((m, MLA_QW), BF16), jax.ShapeDtypeStruct((m, MLA_VW), BF16)],
        compiler_params=_cparams(("parallel",)),
        name="mla_kv_expand",
    )(lat, kr, w_uk, place, w_uv)


def _positions(q0, kb, tq, tk):
    t = q0 + lax.broadcasted_iota(jnp.int32, (tq, tk), 0)
    s = kb * tk + lax.broadcasted_iota(jnp.int32, (tq, tk), 1)
    return t, s


def _sb_kernel(q_ref, k_ref, v_ref, o_ref, *, tq, tk, q_start):
    q0 = q_start + pl.program_id(2) * tq
    n_full = q0 // tk
    q = q_ref[0]
    groups = q.shape[-1] // LANES
    lane = lax.broadcasted_iota(jnp.int32, (tq, LANES), 1)
    q_heads = []
    for g in range(groups):
        qg = q[:, g * LANES:(g + 1) * LANES]
        q_heads += [jnp.where(lane < SB_HD, qg, jnp.zeros_like(qg)),
                    jnp.where(lane >= SB_HD, qg, jnp.zeros_like(qg))]
    group_of = lambda h: slice((h // 2) * LANES, (h // 2 + 1) * LANES)
    later = (lax.broadcasted_iota(jnp.int32, (tk, tk), 0)
             > lax.broadcasted_iota(jnp.int32, (tk, tk), 1)).astype(BF16)

    def load(kb):
        start = pl.multiple_of(kb * tk, tk)
        return k_ref[0, pl.ds(start, tk), :], v_ref[0, pl.ds(start, tk), :]

    def miss_cost(z):
        cost = jnp.maximum(z, 0.0) + jnp.log2(1.0 + jnp.exp2(-jnp.abs(z)))
        return cost, z - cost

    def cost_after(cost):
        hi = cost.astype(BF16)
        lo = (cost - hi.astype(F32)).astype(BF16)
        return _dot(hi, later) + _dot(lo, later)

    def block_total(cost, after):
        return after[:, 0:1] + cost[:, 0:1]

    lower = jnp.maximum(n_full - 1, 0)
    (k_lo, v_lo), (k_up, v_up) = load(lower), load(lower + 1)
    t, s_lo = _positions(q0, lower, tq, tk)
    earlier_lo = s_lo < t
    earlier_up = s_lo + tk < t
    state = []
    for h, qh in enumerate(q_heads):
        sl = group_of(h)
        cost_up, hit_up = miss_cost(_dot_t(qh, k_up[:, sl]))
        cost_lo, hit_lo = miss_cost(_dot_t(qh, k_lo[:, sl]))
        cost_up = jnp.where(earlier_up, cost_up, 0.0)
        cost_lo = jnp.where(earlier_lo, cost_lo, 0.0)
        after_up, after_lo = cost_after(cost_up), cost_after(cost_lo)
        run_up = block_total(cost_up, after_up)
        w_up = jnp.where(earlier_up, jnp.exp2(hit_up - after_up), 0.0)
        w_lo = jnp.where(earlier_lo, jnp.exp2(hit_lo - after_lo - run_up), 0.0)
        acc = _dot(w_up.astype(BF16), v_up[:, sl]) + _dot(w_lo.astype(BF16), v_lo[:, sl])
        state.append((run_up + block_total(cost_lo, after_lo), acc))

    def least_run(st):
        least = st[0][0]
        for run, _ in st[1:]:
            least = jnp.minimum(least, run)
        return jnp.min(least)

    def more(c):
        kb, least, _ = c
        return (kb >= 0) & (least < SB_UNDERFLOW)

    def block(c):
        kb, _, st = c
        ks, vs = load(kb)
        new = []
        for h, (qh, (run, acc)) in enumerate(zip(q_heads, st)):
            cost, hit = miss_cost(_dot_t(qh, ks[:, group_of(h)]))
            after = cost_after(cost)
            acc = acc + _dot(jnp.exp2(hit - after - run).astype(BF16), vs[:, group_of(h)])
            new.append((run + block_total(cost, after), acc))
        return kb - 1, least_run(new), tuple(new)

    _, _, state = lax.while_loop(more, block, (lower - 1, least_run(state), tuple(state)))
    for g in range(groups):
        o_ref[0, :, g * LANES:(g + 1) * LANES] = jnp.where(
            lane < SB_HD, state[2 * g][1], state[2 * g + 1][1]).astype(o_ref.dtype)


def _softmax_bucketed(chains, k_ref, v_ref, q0, emit, *, tq, width, q_range, n_keys):
    bucket = q0 // width
    for j in range(q_range[0] // width, (q_range[1] - 1) // width + 1):
        @pl.when(bucket == j)
        def _(j=j):
            length = (j + 1) * width
            t_chunk = (q0 + lax.broadcasted_iota(jnp.int32, (tq, 1), 0)) // CHUNK
            s_pos = length - width + lax.broadcasted_iota(jnp.int32, (1, width), 1)
            s_chunk = jnp.where(s_pos < n_keys, s_pos // CHUNK, jnp.iinfo(jnp.int32).max)
            vis = s_chunk <= t_chunk
            outs = []
            for qc, k_sl, v_sl in chains:
                s = _dot_t(qc, k_ref[0, 0:length, k_sl])
                tail = jnp.where(vis, s[:, length - width:], -jnp.inf)
                s = tail if length == width else jnp.concatenate([s[:, :length - width], tail], axis=1)
                p = jnp.exp2(s - jnp.max(s, axis=-1, keepdims=True))
                l = jnp.sum(p, axis=-1, keepdims=True)
                outs.append(_dot(p.astype(BF16), v_ref[0, 0:length, v_sl]) / l)
            emit(outs)


def _lane_slab(i):
    return slice(i * LANES, (i + 1) * LANES)


def _df_kernel(lam_ref, sub_ref, q_ref, k_ref, v_ref, o_ref, *, tq, width, q_range, n_keys, lam_init):
    q0 = q_range[0] + pl.program_id(2) * tq
    q = q_ref[0]
    heads = q.shape[-1] // LANES
    lane = lax.broadcasted_iota(jnp.int32, (tq, LANES), 1)
    chains = []
    for h in range(heads):
        qh = q[:, _lane_slab(h)]
        chains.append((jnp.where(lane < DF_HD, qh, jnp.zeros_like(qh)), _lane_slab(h), _lane_slab(h)))
        chains.append((jnp.where(lane >= DF_HD, qh, jnp.zeros_like(qh)), _lane_slab(h), _lane_slab(h)))

    def emit(outs):
        lv = lam_ref[...]
        lam = (jnp.exp(jnp.sum(lv[0:1] * lv[1:2], axis=-1, keepdims=True))
               - jnp.exp(jnp.sum(lv[2:3] * lv[3:4], axis=-1, keepdims=True)) + lam_init)
        for h in range(heads):
            out = outs[2 * h] - lam * outs[2 * h + 1]
            o_ref[0, :, _lane_slab(h)] = (_rms(out, sub_ref[...]) * (1.0 - lam_init)).astype(o_ref.dtype)

    _softmax_bucketed(chains, k_ref, v_ref, q0, emit, tq=tq, width=width, q_range=q_range, n_keys=n_keys)


def _mla_kernel(q_ref, k_ref, v_ref, o_ref, *, tq, width, q_range, n_keys):
    q0 = q_range[0] + pl.program_id(2) * tq
    q = q_ref[0]
    heads = q.shape[-1] // LANES
    lane = lax.broadcasted_iota(jnp.int32, (tq, LANES), 1)
    chains = [(q[:, _lane_slab(h)], _lane_slab(h), _lane_slab(h // 2)) for h in range(heads)]

    def emit(outs):
        for g in range(heads // 2):
            o_ref[0, :, _lane_slab(g)] = jnp.where(
                lane < MLA_VD, outs[2 * g], outs[2 * g + 1]).astype(o_ref.dtype)

    _softmax_bucketed(chains, k_ref, v_ref, q0, emit, tq=tq, width=width, q_range=q_range, n_keys=n_keys)


def _mla_cached_kernel(q_ref, latc_ref, krc_ref, latn_ref, krn_ref, wuk_ref, place_ref, wuv_ref, o_ref,
                       *, q_start):
    q = q_ref[0]
    t = q.shape[0]
    slabs = [slice(h * LANES, (h + 1) * LANES) for h in range(MLA_HEADS)]
    q_all = jnp.concatenate([q[:, sl] for sl in slabs], axis=0)
    q_lat = jnp.concatenate([_dot_t(q[:, sl], wuk_ref[:, sl]) for sl in slabs], axis=0).astype(BF16)
    rows = MLA_HEADS * t
    pad = LANES - t

    def scores(lat, kr):
        kr_lanes = _dot(kr.astype(BF16), place_ref[...]).astype(BF16)
        return _dot_t(q_lat, lat) + _dot_t(q_all, kr_lanes)

    lat_c = latc_ref[0].astype(BF16)
    s_c = scores(lat_c, krc_ref[0])
    lat_n = jnp.concatenate([latn_ref[0], jnp.zeros((pad, KV_LORA), F32)], axis=0).astype(BF16)
    kr_n = jnp.concatenate([krn_ref[0], jnp.zeros((pad, MLA_ROPE), F32)], axis=0)
    t_pos = q_start + lax.broadcasted_iota(jnp.int32, (rows, LANES), 0) % t
    j = lax.broadcasted_iota(jnp.int32, (rows, LANES), 1)
    vis = (j < t) & (((q_start + j) // CHUNK) <= (t_pos // CHUNK))
    s_n = jnp.where(vis, scores(lat_n, kr_n), -jnp.inf)

    m = jnp.maximum(jnp.max(s_c, axis=-1, keepdims=True), jnp.max(s_n, axis=-1, keepdims=True))
    p_c, p_n = jnp.exp2(s_c - m), jnp.exp2(s_n - m)
    l = jnp.sum(p_c, axis=-1, keepdims=True) + jnp.sum(p_n, axis=-1, keepdims=True)
    o_lat = (_dot(p_c.astype(BF16), lat_c) + _dot(p_n.astype(BF16), lat_n)) / l
    full = _dot(o_lat.astype(BF16), wuv_ref[...])
    lane_head = lax.broadcasted_iota(jnp.int32, (t, MLA_VW), 1) // MLA_VD
    out = jnp.zeros((t, MLA_VW), F32)
    for h in range(MLA_HEADS):
        out = out + jnp.where(lane_head == h, full[h * t:(h + 1) * t, :], 0.0)
    o_ref[0] = out.astype(o_ref.dtype)


def _mla_cached_attention(q, lat_cache, kr_cache, lat_new, kr_new, w_uk, place, w_uv, q_start):
    b, t, _ = q.shape
    past = lat_cache.shape[1]
    assert t <= LANES and q_start == past
    per_b = lambda bi: (bi, 0, 0)
    const = lambda bi: (0, 0)
    return pl.pallas_call(
        functools.partial(_mla_cached_kernel, q_start=q_start),
        grid=(b,),
        in_specs=[
            pl.BlockSpec((1, t, MLA_QW), per_b),
            pl.BlockSpec((1, past, KV_LORA), per_b),
            pl.BlockSpec((1, past, MLA_ROPE), per_b),
            pl.BlockSpec((1, t, KV_LORA), per_b),
            pl.BlockSpec((1, t, MLA_ROPE), per_b),
            pl.BlockSpec(w_uk.shape, const),
            pl.BlockSpec(place.shape, const),
            pl.BlockSpec(w_uv.shape, const),
        ],
        out_specs=pl.BlockSpec((1, t, MLA_VW), per_b),
        out_shape=jax.ShapeDtypeStruct((b, t, MLA_VW), BF16),
        compiler_params=_cparams(("parallel",)),
        name="mla_cached_attn",
    )(q, lat_cache, kr_cache, lat_new, kr_new, w_uk, place, w_uv)


def _attention(kernel_fn, q, k, v, extra, *, q_lanes, k_lanes, tq, tk, name, v_groups=1):
    b, t_q, _ = q.shape
    t_k = k.shape[1]
    groups = v.shape[-1] // (LANES * v_groups)
    q_lanes, k_lanes, v_lanes = q_lanes * v_groups, k_lanes * v_groups, LANES * v_groups
    assert t_q % tq == 0 and t_k % tk == 0 and tk % tq == 0 and tk % CHUNK == 0
    const = lambda bi, g, qi: (0, 0)
    return pl.pallas_call(
        kernel_fn,
        grid=(b, groups, t_q // tq),
        in_specs=[pl.BlockSpec(e.shape, const) for e in extra] + [
            pl.BlockSpec((1, tq, q_lanes), lambda bi, g, qi: (bi, qi, g)),
            pl.BlockSpec((1, t_k, k_lanes), lambda bi, g, qi: (bi, 0, g)),
            pl.BlockSpec((1, t_k, v_lanes), lambda bi, g, qi: (bi, 0, g)),
        ],
        out_specs=pl.BlockSpec((1, tq, v_lanes), lambda bi, g, qi: (bi, qi, g)),
        out_shape=jax.ShapeDtypeStruct((b, t_q, groups * v_lanes), BF16),
        compiler_params=_cparams(("parallel", "parallel", "arbitrary")),
        name=name,
    )(*extra, q, k, v)


def _rope_tables(pos, rot, theta, period, offset):
    half = rot // 2
    inv = jnp.float32(theta) ** (-(jnp.arange(half, dtype=F32) * (2.0 / rot)))
    ang = pos.astype(F32)[:, None] * inv[None, :]
    cos, sin = jnp.cos(ang), jnp.sin(ang)
    lane = jnp.arange(LANES) % period - offset
    first = (lane >= 0) & (lane < half)
    second = (lane >= half) & (lane < rot)
    idx = jnp.clip(jnp.where(second, lane - half, lane), 0, half - 1)
    cos_l, sin_l = cos[:, idx], sin[:, idx]
    cos_t = jnp.where(first | second, cos_l, 1.0)
    sin_up = jnp.where(first, -sin_l, 0.0)
    sin_dn = jnp.where(second, sin_l, 0.0)
    return cos_t, sin_up, sin_dn


def _pad_heads(w, heads, width):
    k = w.shape[0]
    w = w.reshape(k, heads, width)
    return jnp.pad(w, ((0, 0), (0, 0), (0, LANES - width))).reshape(k, heads * LANES)


def _with_rotation_partner(w):
    k = w.shape[0]
    half = MLA_ROPE // 2
    w3 = w.reshape(k, MLA_HEADS, LANES)
    x1 = w3[..., MLA_NOPE:MLA_NOPE + half]
    x2 = w3[..., MLA_NOPE + half:MLA_NOPE + MLA_ROPE]
    rot = jnp.concatenate([jnp.zeros_like(w3[..., :MLA_NOPE]), -x2, x1,
                           jnp.zeros_like(w3[..., MLA_QD:])], axis=-1)
    return jnp.concatenate([w, rot.reshape(k, MLA_HEADS * LANES)], axis=-1)


def _with_cache(cache, new, t_pad, dtype):
    b, past, w = cache.shape
    pad = jnp.zeros((b, t_pad - past - new.shape[1], w), dtype)
    return jnp.concatenate([cache.astype(dtype), new.astype(dtype), pad], axis=1)


def _trunk(x, p, start, caches, W, *, tm, tf, tq, width, sb_tq, sb_tk, sb_groups, df_groups, mla_groups):
    b, t, d = x.shape
    m = b * t
    depth = p.shape[0]
    x = x.reshape(m, d)
    pos = start + jnp.arange(t)
    if t % tm == 0:
        tab_pos = pos
    else:
        tab_pos = jnp.tile(pos, m // t)
    df_tabs = _rope_tables(tab_pos, DF_ROT, ROPE_THETA, DF_HD, 0)
    q_cos, q_sin_up, q_sin_dn = _rope_tables(tab_pos, MLA_ROPE, MLA_THETA, LANES, MLA_NOPE)
    mq_tabs = (q_cos, q_sin_dn - q_sin_up)
    mk_tabs = _rope_tables(tab_pos, MLA_ROPE, MLA_THETA, LANES, 0)
    n_keys = start + t
    t_keys = -(-n_keys // width) * width
    assert t_keys % sb_tk == 0
    q_range = (start, start + t)

    ab_rows, mla_rows = [], []
    for i in range(depth):
        g = W['norms'][i][:, None, :]
        j = i // 2
        x = _ffn_block(x, g[0], g[1], W['ffn1_gu'][i], W['ffn1_dn'][i], tm, tf)
        if i % 2 == 0:
            (sbq, sbk, sbv, dfq, dfk, dfv, sbk_f, sbv_f, dfk_f, dfv_f) = _ab_in_proj(
                x, g[2], W['ab_in'][j], df_tabs, tm)
            ab_rows.append((sbk_f.reshape(b, t, SB_HEADS, SB_HD), sbv_f.reshape(b, t, SB_HEADS, SB_HD),
                            dfk_f.reshape(b, t, DF_HEADS, 2, DF_HD), dfv_f.reshape(b, t, DF_HEADS, 2 * DF_HD)))
            three = lambda a: a.reshape(b, t, a.shape[-1])
            if caches is None:
                keys = [three(a) for a in (sbk, sbv, dfk, dfv)]
            else:
                keys = [_with_cache(c[j].reshape(b, start, -1), n.reshape(b, t, -1), t_keys, BF16)
                        for c, n in zip(caches[:4], (sbk_f, sbv_f, dfk_f, dfv_f))]
            sb_out = _attention(
                functools.partial(_sb_kernel, tq=sb_tq, tk=sb_tk, q_start=start),
                three(sbq), keys[0], keys[1], [], q_lanes=LANES, k_lanes=LANES, tq=sb_tq, tk=sb_tk,
                name="sb_attn", v_groups=sb_groups)
            lam_init = 0.8 - 0.6 * math.exp(-0.3 * i)
            df_out = _attention(
                functools.partial(_df_kernel, tq=tq, width=width, q_range=q_range, n_keys=n_keys,
                                  lam_init=lam_init),
                three(dfq), keys[2], keys[3], [W['df_lambda'][j], W['df_subln'][j][None, :]],
                q_lanes=LANES, k_lanes=LANES, tq=tq, tk=width, name="df_attn", v_groups=df_groups)
            parts = [sb_out.reshape(m, SB_W), df_out.reshape(m, DF_W)]
            w_out = W['ab_out'][j]
        else:
            q, lat, kr = _mla_in_proj(x, g[2], W['mla_in'][j], W['mla_q_norm'][j][None, :],
                                      W['mla_kv_norm'][j][None, :], W['mla_uq'][j], mq_tabs, mk_tabs, tm)
            mla_rows.append((lat.reshape(b, t, KV_LORA), kr.reshape(b, t, MLA_ROPE)))
            if caches is None:
                k_all, v_all = _mla_kv_expand(lat, kr, W['mla_uk'][j], W['mla_place'], W['mla_uv'][j], tm)
                out = _attention(
                    functools.partial(_mla_kernel, tq=tq, width=width, q_range=q_range, n_keys=n_keys),
                    q.reshape(b, t, MLA_QW), k_all.reshape(b, -1, MLA_QW), v_all.reshape(b, -1, MLA_VW), [],
                    q_lanes=2 * LANES, k_lanes=2 * LANES, tq=tq, tk=width, name="mla_attn",
                    v_groups=mla_groups)
            else:
                out = _mla_cached_attention(
                    q.reshape(b, t, MLA_QW), caches[4][j], caches[5][j], lat.reshape(b, t, KV_LORA),
                    kr.reshape(b, t, MLA_ROPE), W['mla_uk'][j], W['mla_place'][:, :LANES], W['mla_uv'][j], start)
            parts = [out.reshape(m, MLA_VW)]
            w_out = W['mla_out'][j]
        x = _post_mixer(x, p.reshape(depth, m, -1), i, g[3:8], w_out, W['ffn2_gu'][i], W['ffn2_dn'][i],
                        W['ple_gate'][i], W['ple_in'][i], parts, tm, tf)
    sb_k, sb_v, df_k, df_v = [jnp.stack(r, axis=0) for r in zip(*ab_rows)]
    lat, kr = [jnp.stack(r, axis=0) for r in zip(*mla_rows)]
    return x.reshape(b, t, d), sb_k, sb_v, df_k, df_v, lat, kr


def _pick(n, candidates):
    for c in candidates:
        if n % c == 0:
            return c
    return n


def _cast_kernel(x_ref, o_ref):
    o_ref[...] = x_ref[...].astype(o_ref.dtype)


def _to_bf16(w):
    n_l, k, n = w.shape
    bk = _pick(k, (256, 128, 64, 32, 16))
    spec = pl.BlockSpec((1, bk, n), lambda l, i: (l, i, 0))
    return pl.pallas_call(
        _cast_kernel, grid=(n_l, k // bk), in_specs=[spec], out_specs=spec,
        out_shape=jax.ShapeDtypeStruct(w.shape, BF16),
        compiler_params=_cparams(("parallel", "parallel")), name="to_bf16",
    )(w)


def kernel(x_prompt, x_sample, p_prompt, p_sample, cache_sb_k, cache_sb_v, cache_df_k, cache_df_v,
           cache_mla_latent, cache_mla_krope, norms, ffn1_gu, ffn1_dn, ffn2_gu, ffn2_dn, ple_in, ple_gate,
           ab_in, ab_out, df_lambda, df_subln, mla_in, mla_q_norm, mla_kv_norm, mla_uq, mla_uk, mla_uv, mla_out):
    bf = lambda a: a.astype(BF16)
    n_odd = mla_in.shape[0]
    place = jnp.zeros((MLA_ROPE, MLA_HEADS, LANES), F32)
    place = place.at[jnp.arange(MLA_ROPE), :, MLA_NOPE + jnp.arange(MLA_ROPE)].set(1.0)
    W = dict(
        norms=norms, ffn1_gu=_to_bf16(ffn1_gu), ffn1_dn=_to_bf16(ffn1_dn), ffn2_gu=_to_bf16(ffn2_gu),
        ffn2_dn=_to_bf16(ffn2_dn), ple_in=bf(ple_in), ple_gate=_to_bf16(ple_gate), ab_in=_to_bf16(ab_in),
        ab_out=_to_bf16(ab_out),
        df_lambda=df_lambda, df_subln=df_subln,
        mla_in=bf(jnp.pad(mla_in, ((0, 0), (0, 0), (0, MLA_IN_PAD - mla_in.shape[-1])))),
        mla_q_norm=mla_q_norm, mla_kv_norm=mla_kv_norm,
        mla_uq=bf(jnp.stack([_with_rotation_partner(_pad_heads(mla_uq[j], MLA_HEADS, MLA_QD))
                             for j in range(n_odd)])),
        mla_uk=bf(jnp.stack([_pad_heads(mla_uk[j], MLA_HEADS, MLA_NOPE) for j in range(n_odd)])),
        mla_uv=bf(mla_uv), mla_out=bf(mla_out),
        mla_place=bf(place.reshape(MLA_ROPE, MLA_QW)),
    )
    d_ff = ffn1_dn.shape[1]
    tf = _pick(d_ff, (1408, 256, 128))

    t_p = x_prompt.shape[1]
    m_p = x_prompt.shape[0] * t_p
    width_p = _pick(t_p, (BUCKET, 512, 256, 128, 64))
    tq_p = _pick(width_p, (256, 128, 64))
    out_p = _trunk(x_prompt, p_prompt, 0, None, W, tm=_pick(m_p, (512, 256, 128)), tf=tf, tq=tq_p, width=width_p,
                   sb_tq=tq_p, sb_tk=tq_p, sb_groups=2, df_groups=2, mla_groups=2)

    caches = (cache_sb_k, cache_sb_v, cache_df_k, cache_df_v, cache_mla_latent, cache_mla_krope)
    past = cache_sb_k.shape[2]
    t_s = x_sample.shape[1]
    m_s = x_sample.shape[0] * t_s
    tk_s = _pick(past, (256, 128, 64))
    out_s = _trunk(x_sample, p_sample, past, caches, W, tm=_pick(m_s, (128,)), tf=tf, tq=t_s, width=tk_s,
                   sb_tq=t_s, sb_tk=tk_s, sb_groups=SB_HEADS // 2, df_groups=DF_HEADS, mla_groups=MLA_HEADS // 2)

    (y_p, sb_k_p, sb_v_p, df_k_p, df_v_p, lat_p, kr_p) = out_p
    (y_s, sb_k_s, sb_v_s, df_k_s, df_v_s, lat_s, kr_s) = out_s
    return (y_p, y_s, sb_k_p, sb_v_p, df_k_p, df_v_p, lat_p, kr_p,
            sb_k_s, sb_v_s, df_k_s, df_v_s, lat_s, kr_s)
```

```python
import functools
import math

import jax
import jax.numpy as jnp
from jax import lax
from jax.experimental import pallas as pl
from jax.experimental.pallas import tpu as pltpu

F32 = jnp.float32
BF16 = jnp.bfloat16

EPS = 1e-6
CHUNK = 64
SB_HEADS, SB_HD = 8, 64
DF_HEADS, DF_HD = 4, 64
DF_ROT = DF_HD // 4
ROPE_THETA = 500000.0
MLA_HEADS, MLA_NOPE, MLA_ROPE, MLA_VD = 16, 64, 32, 64
Q_LORA, KV_LORA = 384, 256
MLA_THETA = 10000.0
MLA_QD = MLA_NOPE + MLA_ROPE
LOG2E = math.log2(math.e)
SB_UNDERFLOW = 151.0
SB_LINEAR = 64.0
BUCKET = 512

LANES = 128
SB_W = SB_HEADS * SB_HD
DF_W = DF_HEADS * 2 * DF_HD
MLA_QW = MLA_HEADS * LANES
MLA_VW = MLA_HEADS * MLA_VD
MLA_IN_PAD = 768

VMEM_LIMIT = 56 * 1024 * 1024


def _cparams(sem):
    return pltpu.CompilerParams(dimension_semantics=sem, vmem_limit_bytes=VMEM_LIMIT)


def _rms(x, g):
    return x * lax.rsqrt(jnp.mean(x * x, axis=-1, keepdims=True) + EPS) * g


def _dot(a, b):
    return jnp.dot(a, b, preferred_element_type=F32)


def _dot_t(a, b):
    return lax.dot_general(a, b, (((1,), (1,)), ((), ())), preferred_element_type=F32)


def _rope_slab(x, cos, sin_up, sin_dn, shift):
    n = x.shape[-1]
    return x * cos + pltpu.roll(x, n - shift, 1) * sin_up + pltpu.roll(x, shift, 1) * sin_dn


def _swiglu(h, wgu_ref, wd_ref, chunk):
    d_ff = wd_ref.shape[0]
    acc = None
    for c in range(d_ff // chunk):
        g = _dot(h, wgu_ref[:, c * chunk:(c + 1) * chunk])
        u = _dot(h, wgu_ref[:, d_ff + c * chunk:d_ff + (c + 1) * chunk])
        t = _dot((g * jax.nn.sigmoid(g) * u).astype(BF16), wd_ref[c * chunk:(c + 1) * chunk, :])
        acc = t if acc is None else acc + t
    return acc


def _whole(a):
    return pl.BlockSpec(a.shape, lambda *_: (0,) * a.ndim, pipeline_mode=pl.Buffered(1))


def _layer(stack, layer):
    return pl.BlockSpec((None,) + stack.shape[1:], lambda *_: (layer,) + (0,) * (stack.ndim - 1),
                        pipeline_mode=pl.Buffered(1))


def _ffn_kernel(x_ref, gin_ref, gout_ref, wgu_ref, wd_ref, o_ref, *, chunk):
    x = x_ref[...]
    h = _rms(x, gin_ref[...]).astype(BF16)
    o_ref[...] = x + 0.5 * _rms(_swiglu(h, wgu_ref, wd_ref, chunk), gout_ref[...])


def _ffn_block(x, g_in, g_out, w_gu, w_dn, layer, tm, tf):
    m, d = x.shape
    row = lambda i: (i, 0)
    return pl.pallas_call(
        functools.partial(_ffn_kernel, chunk=tf),
        grid=(m // tm,),
        in_specs=[pl.BlockSpec((tm, d), row), _whole(g_in), _whole(g_out), _layer(w_gu, layer), _layer(w_dn, layer)],
        out_specs=pl.BlockSpec((tm, d), row),
        out_shape=jax.ShapeDtypeStruct((m, d), F32),
        compiler_params=_cparams(("parallel",)),
        name="ffn_block",
    )(x, g_in, g_out, w_gu, w_dn)


def _ab_in_kernel(x_ref, g_ref, w_ref, cos_ref, sup_ref, sdn_ref,
                  sbq_ref, sbk_ref, sbv_ref, dfq_ref, dfk_ref, dfv_ref,
                  sbk_f_ref, sbv_f_ref, dfk_f_ref, dfv_f_ref):
    h = _rms(x_ref[...], g_ref[...]).astype(BF16)
    cos, sup, sdn = cos_ref[...], sup_ref[...], sdn_ref[...]

    def proj(c):
        return _dot(h, w_ref[:, c * SB_W:(c + 1) * SB_W])

    def store_rows(out_f, x, first):
        for j in range(x.shape[-1] // SB_HD):
            out_f[pl.ds(first + j, x.shape[0], stride=SB_W // SB_HD), :] = x[:, j * SB_HD:(j + 1) * SB_HD]

    def rope(x, scale, out_b, out_f):
        for s in range(x.shape[-1] // LANES):
            sl = slice(s * LANES, (s + 1) * LANES)
            r = _rope_slab(x[:, sl], cos, sup, sdn, DF_ROT // 2)
            if out_f is not None:
                store_rows(out_f, r, s * (LANES // SB_HD))
            out_b[:, sl] = (r * scale).astype(BF16)

    sbq_ref[...] = (proj(0) * (SB_HD ** -0.5 * LOG2E)).astype(BF16)
    sbk = proj(1)
    store_rows(sbk_f_ref, sbk, 0)
    sbk_ref[...] = sbk.astype(BF16)
    sbv = proj(2)
    store_rows(sbv_f_ref, sbv, 0)
    sbv_ref[...] = sbv.astype(BF16)
    rope(proj(3), DF_HD ** -0.5 * LOG2E, dfq_ref, None)
    rope(proj(4), 1.0, dfk_ref, dfk_f_ref)
    dfv = proj(5)
    for j in range(DF_HEADS):
        dfv_f_ref[pl.ds(j, dfv.shape[0], stride=DF_HEADS), :] = dfv[:, j * LANES:(j + 1) * LANES]
    dfv_ref[...] = dfv.astype(BF16)


def _ab_in_proj(x, g, w_in, tabs, tm):
    m, d = x.shape
    n_tab = tabs[0].shape[0] // tm
    row = lambda i: (i, 0)
    const = lambda i: (0, 0)
    tab = lambda i: (i % n_tab, 0)
    wide = pl.BlockSpec((tm, SB_W), row)
    bf = jax.ShapeDtypeStruct((m, SB_W), BF16)
    f32 = jax.ShapeDtypeStruct((m, SB_W), F32)
    n_rows = SB_W // SB_HD
    tall = pl.BlockSpec((tm * n_rows, SB_HD), row)
    f32_tall = jax.ShapeDtypeStruct((m * n_rows, SB_HD), F32)
    return pl.pallas_call(
        _ab_in_kernel,
        grid=(m // tm,),
        in_specs=[
            pl.BlockSpec((tm, d), row),
            pl.BlockSpec((1, d), const),
            pl.BlockSpec(w_in.shape, const),
            pl.BlockSpec((tm, LANES), tab),
            pl.BlockSpec((tm, LANES), tab),
            pl.BlockSpec((tm, LANES), tab),
        ],
        out_specs=[wide] * 6 + [tall] * 3 + [pl.BlockSpec((tm * DF_HEADS, LANES), row)],
        out_shape=[bf] * 6 + [f32_tall] * 3 + [jax.ShapeDtypeStruct((m * DF_HEADS, LANES), F32)],
        compiler_params=_cparams(("parallel",)),
        name="ab_in_proj",
    )(x, g, w_in, *tabs)


def _post_mixer_kernel(*refs, n_parts, chunk):
    (x_ref, p_ref, g_ref, wo_ref, wgu_ref, wd_ref, wgate_ref, wple_ref) = refs[:8]
    parts = refs[8:8 + n_parts]
    o_ref = refs[8 + n_parts]
    g_mix, g_in, g_out, g_gate, g_ple = (g_ref[i] for i in range(5))
    off = 0
    mix = None
    for p in parts:
        k = p.shape[-1]
        t = _dot(p[...], wo_ref[off:off + k, :])
        mix = t if mix is None else mix + t
        off += k
    x1 = x_ref[...] + _rms(mix, g_mix)
    x2 = x1 + 0.5 * _rms(_swiglu(_rms(x1, g_in).astype(BF16), wgu_ref, wd_ref, chunk), g_out)
    gate = jax.nn.sigmoid(_dot(_rms(x2, g_gate).astype(BF16), wgate_ref[...]))
    emb = _dot(p_ref[...].astype(BF16), wple_ref[...])
    o_ref[...] = x2 + _rms(gate * emb, g_ple)


def _post_mixer(x, p, layer, gains, w_out, w_gu, w_dn, w_gate, w_ple, parts, tm, tf):
    m, d = x.shape
    row = lambda i: (i, 0)
    return pl.pallas_call(
        functools.partial(_post_mixer_kernel, n_parts=len(parts), chunk=tf),
        grid=(m // tm,),
        in_specs=[
            pl.BlockSpec((tm, d), row),
            pl.BlockSpec((None, tm, p.shape[-1]), lambda i: (layer, i, 0)),
        ] + [_whole(gains), _whole(w_out)] + [_layer(a, layer) for a in (w_gu, w_dn, w_gate, w_ple)]
        + [pl.BlockSpec((tm, a.shape[-1]), row) for a in parts],
        out_specs=pl.BlockSpec((tm, d), row),
        out_shape=jax.ShapeDtypeStruct((m, d), F32),
        compiler_params=_cparams(("parallel",)),
        name="post_mixer",
    )(x, p, gains, w_out, w_gu, w_dn, w_gate, w_ple, *parts)


def _mla_in_kernel(x_ref, g_ref, w_ref, qn_ref, kvn_ref, wuq_ref,
                   qcos_ref, qsin_ref, kcos_ref, ksup_ref, ksdn_ref,
                   q_ref, lat_ref, kr_ref):
    h = _rms(x_ref[...], g_ref[...]).astype(BF16)
    proj = _dot(h, w_ref[...])
    c_q = _rms(proj[:, :Q_LORA], qn_ref[...]).astype(BF16)
    lat_ref[...] = _rms(proj[:, Q_LORA:Q_LORA + KV_LORA], kvn_ref[...])
    kr = _rope_slab(proj[:, Q_LORA + KV_LORA:], kcos_ref[...], ksup_ref[...], ksdn_ref[...],
                    MLA_ROPE // 2)
    kr_ref[...] = kr[:, :MLA_ROPE]
    q2 = _dot(c_q, wuq_ref[...])
    cos, sin = qcos_ref[...], qsin_ref[...]
    for hd in range(MLA_HEADS):
        sl = slice(hd * LANES, (hd + 1) * LANES)
        rot = slice(MLA_QW + hd * LANES, MLA_QW + (hd + 1) * LANES)
        r = q2[:, sl] * cos + q2[:, rot] * sin
        q_ref[:, sl] = (r * (MLA_QD ** -0.5 * LOG2E)).astype(BF16)


def _mla_in_proj(x, g, w_in, q_norm, kv_norm, w_uq, qtabs, ktabs, tm):
    m, d = x.shape
    n_tab = qtabs[0].shape[0] // tm
    row = lambda i: (i, 0)
    const = lambda i: (0, 0)
    tab = lambda i: (i % n_tab, 0)
    tspec = pl.BlockSpec((tm, LANES), tab)
    return pl.pallas_call(
        _mla_in_kernel,
        grid=(m // tm,),
        in_specs=[
            pl.BlockSpec((tm, d), row),
            pl.BlockSpec((1, d), const),
            pl.BlockSpec(w_in.shape, const),
            pl.BlockSpec((1, Q_LORA), const),
            pl.BlockSpec((1, KV_LORA), const),
            pl.BlockSpec(w_uq.shape, const),
        ] + [tspec] * 5,
        out_specs=[
            pl.BlockSpec((tm, MLA_QW), row),
            pl.BlockSpec((tm, KV_LORA), row),
            pl.BlockSpec((tm, MLA_ROPE), row),
        ],
        out_shape=[
            jax.ShapeDtypeStruct((m, MLA_QW), BF16),
            jax.ShapeDtypeStruct((m, KV_LORA), F32),
            jax.ShapeDtypeStruct((m, MLA_ROPE), F32),
        ],
        compiler_params=_cparams(("parallel",)),
        name="mla_in_proj",
    )(x, g, w_in, q_norm, kv_norm, w_uq, *qtabs, *ktabs)


def _mla_kv_kernel(lat_ref, kr_ref, wuk_ref, place_ref, wuv_ref, k_ref, v_ref):
    lat = lat_ref[...].astype(BF16)
    k_ref[...] = (_dot(lat, wuk_ref[...]) + _dot(kr_ref[...].astype(BF16), place_ref[...])).astype(BF16)
    v_ref[...] = _dot(lat, wuv_ref[...]).astype(BF16)


def _mla_kv_expand(lat, kr, w_uk, place, w_uv, tm):
    m = lat.shape[0]
    row = lambda i: (i, 0)
    const = lambda i: (0, 0)
    return pl.pallas_call(
        _mla_kv_kernel,
        grid=(m // tm,),
        in_specs=[
            pl.BlockSpec((tm, KV_LORA), row),
            pl.BlockSpec((tm, MLA_ROPE), row),
            pl.BlockSpec(w_uk.shape, const),
            pl.BlockSpec(place.shape, const),
            pl.BlockSpec(w_uv.shape, const),
        ],
        out_specs=[pl.BlockSpec((tm, MLA_QW), row), pl.BlockSpec((tm, MLA_VW), row)],
        out_shape=[jax.ShapeDtypeStruct((m, MLA_QW), BF16), jax.ShapeDtypeStruct((m, MLA_VW), BF16)],
        compiler_params=_cparams(("parallel",)),
        name="mla_kv_expand",
    )(lat, kr, w_uk, place, w_uv)


def _positions(q0, kb, tq, tk):
    t = q0 + lax.broadcasted_iota(jnp.int32, (tq, tk), 0)
    s = kb * tk + lax.broadcasted_iota(jnp.int32, (tq, tk), 1)
    return t, s


def _sb_kernel(q_ref, k_ref, v_ref, o_ref, *, tq, tk, q_start):
    q0 = q_start + pl.program_id(2) * tq
    n_full = q0 // tk
    q = q_ref[0]
    groups = q.shape[-1] // LANES
    lane = lax.broadcasted_iota(jnp.int32, (tq, LANES), 1)
    q_heads = []
    for g in range(groups):
        qg = q[:, g * LANES:(g + 1) * LANES]
        q_heads += [jnp.where(lane < SB_HD, qg, jnp.zeros_like(qg)),
                    jnp.where(lane >= SB_HD, qg, jnp.zeros_like(qg))]
    group_of = lambda h: slice((h // 2) * LANES, (h // 2 + 1) * LANES)
    later = (lax.broadcasted_iota(jnp.int32, (tk, tk), 0)
             > lax.broadcasted_iota(jnp.int32, (tk, tk), 1)).astype(BF16)

    def load(kb):
        start = pl.multiple_of(kb * tk, tk)
        return k_ref[0, pl.ds(start, tk), :], v_ref[0, pl.ds(start, tk), :]

    def miss_cost(z):
        cost = jnp.maximum(jnp.log2(1.0 + jnp.exp2(jnp.minimum(z, SB_LINEAR))), z)
        return cost, z - cost

    def cost_after(cost):
        hi = cost.astype(BF16)
        lo = (cost - hi.astype(F32)).astype(BF16)
        return _dot(hi, later) + _dot(lo, later)

    def block_total(cost, after):
        return after[:, 0:1] + cost[:, 0:1]

    lower = jnp.maximum(n_full - 1, 0)
    (k_lo, v_lo), (k_up, v_up) = load(lower), load(lower + 1)
    t, s_lo = _positions(q0, lower, tq, tk)
    earlier_lo = s_lo < t
    earlier_up = s_lo + tk < t
    state = []
    for h, qh in enumerate(q_heads):
        sl = group_of(h)
        cost_up, hit_up = miss_cost(_dot_t(qh, k_up[:, sl]))
        cost_lo, hit_lo = miss_cost(_dot_t(qh, k_lo[:, sl]))
        cost_up = jnp.where(earlier_up, cost_up, 0.0)
        cost_lo = jnp.where(earlier_lo, cost_lo, 0.0)
        after_up, after_lo = cost_after(cost_up), cost_after(cost_lo)
        run_up = block_total(cost_up, after_up)
        w_up = jnp.where(earlier_up, jnp.exp2(hit_up - after_up), 0.0)
        w_lo = jnp.where(earlier_lo, jnp.exp2(hit_lo - after_lo - run_up), 0.0)
        acc = _dot(w_up.astype(BF16), v_up[:, sl]) + _dot(w_lo.astype(BF16), v_lo[:, sl])
        state.append((run_up + block_total(cost_lo, after_lo), acc))

    def least_run(st):
        least = st[0][0]
        for run, _ in st[1:]:
            least = jnp.minimum(least, run)
        return jnp.min(least)

    def more(c):
        kb, least, _ = c
        return (kb >= 0) & (least < SB_UNDERFLOW)

    def block(c):
        kb, _, st = c
        ks, vs = load(kb)
        new = []
        for h, (qh, (run, acc)) in enumerate(zip(q_heads, st)):
            cost, hit = miss_cost(_dot_t(qh, ks[:, group_of(h)]))
            after = cost_after(cost)
            acc = acc + _dot(jnp.exp2(hit - after - run).astype(BF16), vs[:, group_of(h)])
            new.append((run + block_total(cost, after), acc))
        return kb - 1, least_run(new), tuple(new)

    _, _, state = lax.while_loop(more, block, (lower - 1, least_run(state), tuple(state)))
    for g in range(groups):
        o_ref[0, :, g * LANES:(g + 1) * LANES] = jnp.where(
            lane < SB_HD, state[2 * g][1], state[2 * g + 1][1]).astype(o_ref.dtype)


def _softmax_bucketed(chains, k_ref, v_ref, q0, emit, *, tq, width, q_range, n_keys):
    bucket = q0 // width
    for j in range(q_range[0] // width, (q_range[1] - 1) // width + 1):
        @pl.when(bucket == j)
        def _(j=j):
            length = (j + 1) * width
            t_chunk = (q0 + lax.broadcasted_iota(jnp.int32, (tq, 1), 0)) // CHUNK
            s_pos = length - width + lax.broadcasted_iota(jnp.int32, (1, width), 1)
            s_chunk = jnp.where(s_pos < n_keys, s_pos // CHUNK, jnp.iinfo(jnp.int32).max)
            vis = s_chunk <= t_chunk
            outs = []
            for qc, k_sl, v_sl in chains:
                s = _dot_t(qc, k_ref[0, 0:length, k_sl])
                tail = jnp.where(vis, s[:, length - width:], -jnp.inf)
                s = tail if length == width else jnp.concatenate([s[:, :length - width], tail], axis=1)
                p = jnp.exp2(s - jnp.max(s, axis=-1, keepdims=True))
                l = jnp.sum(p, axis=-1, keepdims=True)
                outs.append(_dot(p.astype(BF16), v_ref[0, 0:length, v_sl]) / l)
            emit(outs)


def _lane_slab(i):
    return slice(i * LANES, (i + 1) * LANES)


def _df_kernel(lam_ref, sub_ref, q_ref, k_ref, v_ref, o_ref, *, tq, width, q_range, n_keys, lam_init):
    q0 = q_range[0] + pl.program_id(2) * tq
    q = q_ref[0]
    heads = q.shape[-1] // LANES
    lane = lax.broadcasted_iota(jnp.int32, (tq, LANES), 1)
    chains = []
    for h in range(heads):
        qh = q[:, _lane_slab(h)]
        chains.append((jnp.where(lane < DF_HD, qh, jnp.zeros_like(qh)), _lane_slab(h), _lane_slab(h)))
        chains.append((jnp.where(lane >= DF_HD, qh, jnp.zeros_like(qh)), _lane_slab(h), _lane_slab(h)))

    def emit(outs):
        lv = lam_ref[...]
        lam = (jnp.exp(jnp.sum(lv[0:1] * lv[1:2], axis=-1, keepdims=True))
               - jnp.exp(jnp.sum(lv[2:3] * lv[3:4], axis=-1, keepdims=True)) + lam_init)
        for h in range(heads):
            out = outs[2 * h] - lam * outs[2 * h + 1]
            o_ref[0, :, _lane_slab(h)] = (_rms(out, sub_ref[...]) * (1.0 - lam_init)).astype(o_ref.dtype)

    _softmax_bucketed(chains, k_ref, v_ref, q0, emit, tq=tq, width=width, q_range=q_range, n_keys=n_keys)


def _mla_kernel(q_ref, k_ref, v_ref, o_ref, *, tq, width, q_range, n_keys):
    q0 = q_range[0] + pl.program_id(2) * tq
    q = q_ref[0]
    heads = q.shape[-1] // LANES
    lane = lax.broadcasted_iota(jnp.int32, (tq, LANES), 1)
    chains = [(q[:, _lane_slab(h)], _lane_slab(h), _lane_slab(h // 2)) for h in range(heads)]

    def emit(outs):
        for g in range(heads // 2):
            o_ref[0, :, _lane_slab(g)] = jnp.where(
                lane < MLA_VD, outs[2 * g], outs[2 * g + 1]).astype(o_ref.dtype)

    _softmax_bucketed(chains, k_ref, v_ref, q0, emit, tq=tq, width=width, q_range=q_range, n_keys=n_keys)


def _mla_cached_kernel(q_ref, latc_ref, krc_ref, latn_ref, krn_ref, wuk_ref, place_ref, wuv_ref, o_ref,
                       *, q_start):
    q = q_ref[0]
    t = q.shape[0]
    slabs = [slice(h * LANES, (h + 1) * LANES) for h in range(MLA_HEADS)]
    q_all = jnp.concatenate([q[:, sl] for sl in slabs], axis=0)
    q_lat = jnp.concatenate([_dot_t(q[:, sl], wuk_ref[:, sl]) for sl in slabs], axis=0).astype(BF16)
    rows = MLA_HEADS * t
    pad = LANES - t

    def scores(lat, kr):
        kr_lanes = _dot(kr.astype(BF16), place_ref[...]).astype(BF16)
        return _dot_t(q_lat, lat) + _dot_t(q_all, kr_lanes)

    lat_c = latc_ref[0].astype(BF16)
    s_c = scores(lat_c, krc_ref[0])
    lat_n = jnp.concatenate([latn_ref[0], jnp.zeros((pad, KV_LORA), F32)], axis=0).astype(BF16)
    kr_n = jnp.concatenate([krn_ref[0], jnp.zeros((pad, MLA_ROPE), F32)], axis=0)
    t_pos = q_start + lax.broadcasted_iota(jnp.int32, (rows, LANES), 0) % t
    j = lax.broadcasted_iota(jnp.int32, (rows, LANES), 1)
    vis = (j < t) & (((q_start + j) // CHUNK) <= (t_pos // CHUNK))
    s_n = jnp.where(vis, scores(lat_n, kr_n), -jnp.inf)

    m = jnp.maximum(jnp.max(s_c, axis=-1, keepdims=True), jnp.max(s_n, axis=-1, keepdims=True))
    p_c, p_n = jnp.exp2(s_c - m), jnp.exp2(s_n - m)
    l = jnp.sum(p_c, axis=-1, keepdims=True) + jnp.sum(p_n, axis=-1, keepdims=True)
    o_lat = (_dot(p_c.astype(BF16), lat_c) + _dot(p_n.astype(BF16), lat_n)) / l
    full = _dot(o_lat.astype(BF16), wuv_ref[...])
    lane_head = lax.broadcasted_iota(jnp.int32, (t, MLA_VW), 1) // MLA_VD
    out = jnp.zeros((t, MLA_VW), F32)
    for h in range(MLA_HEADS):
        out = out + jnp.where(lane_head == h, full[h * t:(h + 1) * t, :], 0.0)
    o_ref[0] = out.astype(o_ref.dtype)


def _mla_cached_attention(q, lat_cache, kr_cache, lat_new, kr_new, w_uk, place, w_uv, q_start):
    b, t, _ = q.shape
    past = lat_cache.shape[1]
    assert t <= LANES and q_start == past
    per_b = lambda bi: (bi, 0, 0)
    const = lambda bi: (0, 0)
    return pl.pallas_call(
        functools.partial(_mla_cached_kernel, q_start=q_start),
        grid=(b,),
        in_specs=[
            pl.BlockSpec((1, t, MLA_QW), per_b),
            pl.BlockSpec((1, past, KV_LORA), per_b),
            pl.BlockSpec((1, past, MLA_ROPE), per_b),
            pl.BlockSpec((1, t, KV_LORA), per_b),
            pl.BlockSpec((1, t, MLA_ROPE), per_b),
            pl.BlockSpec(w_uk.shape, const),
            pl.BlockSpec(place.shape, const),
            pl.BlockSpec(w_uv.shape, const),
        ],
        out_specs=pl.BlockSpec((1, t, MLA_VW), per_b),
        out_shape=jax.ShapeDtypeStruct((b, t, MLA_VW), BF16),
        compiler_params=_cparams(("parallel",)),
        name="mla_cached_attn",
    )(q, lat_cache, kr_cache, lat_new, kr_new, w_uk, place, w_uv)


def _attention(kernel_fn, q, k, v, extra, *, q_lanes, k_lanes, tq, tk, name, v_groups=1):
    b, t_q, _ = q.shape
    t_k = k.shape[1]
    groups = v.shape[-1] // (LANES * v_groups)
    q_lanes, k_lanes, v_lanes = q_lanes * v_groups, k_lanes * v_groups, LANES * v_groups
    assert t_q % tq == 0 and t_k % tk == 0 and tk % tq == 0 and tk % CHUNK == 0
    const = lambda bi, g, qi: (0, 0)
    return pl.pallas_call(
        kernel_fn,
        grid=(b, groups, t_q // tq),
        in_specs=[pl.BlockSpec(e.shape, const) for e in extra] + [
            pl.BlockSpec((1, tq, q_lanes), lambda bi, g, qi: (bi, qi, g)),
            pl.BlockSpec((1, t_k, k_lanes), lambda bi, g, qi: (bi, 0, g)),
            pl.BlockSpec((1, t_k, v_lanes), lambda bi, g, qi: (bi, 0, g)),
        ],
        out_specs=pl.BlockSpec((1, tq, v_lanes), lambda bi, g, qi: (bi, qi, g)),
        out_shape=jax.ShapeDtypeStruct((b, t_q, groups * v_lanes), BF16),
        compiler_params=_cparams(("parallel", "parallel", "arbitrary")),
        name=name,
    )(*extra, q, k, v)


def _rope_tables(pos, rot, theta, period, offset):
    half = rot // 2
    inv = jnp.float32(theta) ** (-(jnp.arange(half, dtype=F32) * (2.0 / rot)))
    ang = pos.astype(F32)[:, None] * inv[None, :]
    cos, sin = jnp.cos(ang), jnp.sin(ang)
    lane = jnp.arange(LANES) % period - offset
    first = (lane >= 0) & (lane < half)
    second = (lane >= half) & (lane < rot)
    idx = jnp.clip(jnp.where(second, lane - half, lane), 0, half - 1)
    cos_l, sin_l = cos[:, idx], sin[:, idx]
    cos_t = jnp.where(first | second, cos_l, 1.0)
    sin_up = jnp.where(first, -sin_l, 0.0)
    sin_dn = jnp.where(second, sin_l, 0.0)
    return cos_t, sin_up, sin_dn


def _pad_heads(w, heads, width):
    k = w.shape[0]
    w = w.reshape(k, heads, width)
    return jnp.pad(w, ((0, 0), (0, 0), (0, LANES - width))).reshape(k, heads * LANES)


def _with_rotation_partner(w):
    k = w.shape[0]
    half = MLA_ROPE // 2
    w3 = w.reshape(k, MLA_HEADS, LANES)
    x1 = w3[..., MLA_NOPE:MLA_NOPE + half]
    x2 = w3[..., MLA_NOPE + half:MLA_NOPE + MLA_ROPE]
    rot = jnp.concatenate([jnp.zeros_like(w3[..., :MLA_NOPE]), -x2, x1,
                           jnp.zeros_like(w3[..., MLA_QD:])], axis=-1)
    return jnp.concatenate([w, rot.reshape(k, MLA_HEADS * LANES)], axis=-1)


def _with_cache(cache, new, t_pad, dtype):
    b, past, w = cache.shape
    pad = jnp.zeros((b, t_pad - past - new.shape[1], w), dtype)
    return jnp.concatenate([cache.astype(dtype), new.astype(dtype), pad], axis=1)


def _trunk(x, p, start, caches, W, *, tm, tf, tq, width, sb_tq, sb_tk, sb_groups, df_groups, mla_groups):
    b, t, d = x.shape
    m = b * t
    depth = p.shape[0]
    x = x.reshape(m, d)
    pos = start + jnp.arange(t)
    if t % tm == 0:
        tab_pos = pos
    else:
        tab_pos = jnp.tile(pos, m // t)
    df_tabs = _rope_tables(tab_pos, DF_ROT, ROPE_THETA, DF_HD, 0)
    q_cos, q_sin_up, q_sin_dn = _rope_tables(tab_pos, MLA_ROPE, MLA_THETA, LANES, MLA_NOPE)
    mq_tabs = (q_cos, q_sin_dn - q_sin_up)
    mk_tabs = _rope_tables(tab_pos, MLA_ROPE, MLA_THETA, LANES, 0)
    n_keys = start + t
    t_keys = -(-n_keys // width) * width
    assert t_keys % sb_tk == 0
    q_range = (start, start + t)

    ab_rows, mla_rows = [], []
    for i in range(depth):
        g = W['norms'][i][:, None, :]
        j = i // 2
        x = _ffn_block(x, g[0], g[1], W['ffn1_gu'], W['ffn1_dn'], i, tm, tf)
        if i % 2 == 0:
            (sbq, sbk, sbv, dfq, dfk, dfv, sbk_f, sbv_f, dfk_f, dfv_f) = _ab_in_proj(
                x, g[2], W['ab_in'][j], df_tabs, tm)
            ab_rows.append((sbk_f.reshape(b, t, SB_HEADS, SB_HD), sbv_f.reshape(b, t, SB_HEADS, SB_HD),
                            dfk_f.reshape(b, t, DF_HEADS, 2, DF_HD), dfv_f.reshape(b, t, DF_HEADS, 2 * DF_HD)))
            three = lambda a: a.reshape(b, t, a.shape[-1])
            if caches is None:
                keys = [three(a) for a in (sbk, sbv, dfk, dfv)]
            else:
                keys = [_with_cache(c[j].reshape(b, start, -1), n.reshape(b, t, -1), t_keys, BF16)
                        for c, n in zip(caches[:4], (sbk_f, sbv_f, dfk_f, dfv_f))]
            sb_out = _attention(
                functools.partial(_sb_kernel, tq=sb_tq, tk=sb_tk, q_start=start),
                three(sbq), keys[0], keys[1], [], q_lanes=LANES, k_lanes=LANES, tq=sb_tq, tk=sb_tk,
                name="sb_attn", v_groups=sb_groups)
            lam_init = 0.8 - 0.6 * math.exp(-0.3 * i)
            df_out = _attention(
                functools.partial(_df_kernel, tq=tq, width=width, q_range=q_range, n_keys=n_keys,
                                  lam_init=lam_init),
                three(dfq), keys[2], keys[3], [W['df_lambda'][j], W['df_subln'][j][None, :]],
                q_lanes=LANES, k_lanes=LANES, tq=tq, tk=width, name="df_attn", v_groups=df_groups)
            parts = [sb_out.reshape(m, SB_W), df_out.reshape(m, DF_W)]
            w_out = W['ab_out'][j]
        else:
            q, lat, kr = _mla_in_proj(x, g[2], W['mla_in'][j], W['mla_q_norm'][j][None, :],
                                      W['mla_kv_norm'][j][None, :], W['mla_uq'][j], mq_tabs, mk_tabs, tm)
            mla_rows.append((lat.reshape(b, t, KV_LORA), kr.reshape(b, t, MLA_ROPE)))
            if caches is None:
                k_all, v_all = _mla_kv_expand(lat, kr, W['mla_uk'][j], W['mla_place'], W['mla_uv'][j], tm)
                out = _attention(
                    functools.partial(_mla_kernel, tq=tq, width=width, q_range=q_range, n_keys=n_keys),
                    q.reshape(b, t, MLA_QW), k_all.reshape(b, -1, MLA_QW), v_all.reshape(b, -1, MLA_VW), [],
                    q_lanes=2 * LANES, k_lanes=2 * LANES, tq=tq, tk=width, name="mla_attn",
                    v_groups=mla_groups)
            else:
                out = _mla_cached_attention(
                    q.reshape(b, t, MLA_QW), caches[4][j], caches[5][j], lat.reshape(b, t, KV_LORA),
                    kr.reshape(b, t, MLA_ROPE), W['mla_uk'][j], W['mla_place'][:, :LANES], W['mla_uv'][j], start)
            parts = [out.reshape(m, MLA_VW)]
            w_out = W['mla_out'][j]
        x = _post_mixer(x, p.reshape(depth, m, -1), i, g[3:8], w_out, W['ffn2_gu'], W['ffn2_dn'],
                        W['ple_gate'], W['ple_in'], parts, tm, tf)
    sb_k, sb_v, df_k, df_v = [jnp.stack(r, axis=0) for r in zip(*ab_rows)]
    lat, kr = [jnp.stack(r, axis=0) for r in zip(*mla_rows)]
    return x.reshape(b, t, d), sb_k, sb_v, df_k, df_v, lat, kr


def _pick(n, candidates):
    for c in candidates:
        if n % c == 0:
            return c
    return n


def _cast_kernel(x_ref, o_ref):
    o_ref[...] = x_ref[...].astype(o_ref.dtype)


def _to_bf16(w):
    n_l, k, n = w.shape
    bk = _pick(k, (256, 128, 64, 32, 16))
    spec = pl.BlockSpec((1, bk, n), lambda l, i: (l, i, 0))
    return pl.pallas_call(
        _cast_kernel, grid=(n_l, k // bk), in_specs=[spec], out_specs=spec,
        out_shape=jax.ShapeDtypeStruct(w.shape, BF16),
        compiler_params=_cparams(("parallel", "parallel")), name="to_bf16",
    )(w)


def kernel(x_prompt, x_sample, p_prompt, p_sample, cache_sb_k, cache_sb_v, cache_df_k, cache_df_v,
           cache_mla_latent, cache_mla_krope, norms, ffn1_gu, ffn1_dn, ffn2_gu, ffn2_dn, ple_in, ple_gate,
           ab_in, ab_out, df_lambda, df_subln, mla_in, mla_q_norm, mla_kv_norm, mla_uq, mla_uk, mla_uv, mla_out):
    bf = lambda a: a.astype(BF16)
    n_odd = mla_in.shape[0]
    place = jnp.zeros((MLA_ROPE, MLA_HEADS, LANES), F32)
    place = place.at[jnp.arange(MLA_ROPE), :, MLA_NOPE + jnp.arange(MLA_ROPE)].set(1.0)
    W = dict(
        norms=norms, ffn1_gu=_to_bf16(ffn1_gu), ffn1_dn=_to_bf16(ffn1_dn), ffn2_gu=_to_bf16(ffn2_gu),
        ffn2_dn=_to_bf16(ffn2_dn), ple_in=bf(ple_in), ple_gate=_to_bf16(ple_gate), ab_in=_to_bf16(ab_in),
        ab_out=_to_bf16(ab_out),
        df_lambda=df_lambda, df_subln=df_subln,
        mla_in=bf(jnp.pad(mla_in, ((0, 0), (0, 0), (0, MLA_IN_PAD - mla_in.shape[-1])))),
        mla_q_norm=mla_q_norm, mla_kv_norm=mla_kv_norm,
        mla_uq=bf(jnp.stack([_with_rotation_partner(_pad_heads(mla_uq[j], MLA_HEADS, MLA_QD))
                             for j in range(n_odd)])),
        mla_uk=bf(jnp.stack([_pad_heads(mla_uk[j], MLA_HEADS, MLA_NOPE) for j in range(n_odd)])),
        mla_uv=bf(mla_uv), mla_out=bf(mla_out),
        mla_place=bf(place.reshape(MLA_ROPE, MLA_QW)),
    )
    d_ff = ffn1_dn.shape[1]
    tf = _pick(d_ff, (1408, 256, 128))

    t_p = x_prompt.shape[1]
    m_p = x_prompt.shape[0] * t_p
    width_p = _pick(t_p, (BUCKET, 512, 256, 128, 64))
    tq_p = _pick(width_p, (256, 128, 64))
    out_p = _trunk(x_prompt, p_prompt, 0, None, W, tm=_pick(m_p, (512, 256, 128)), tf=tf, tq=tq_p, width=width_p,
                   sb_tq=tq_p, sb_tk=tq_p, sb_groups=2, df_groups=2, mla_groups=2)

    caches = (cache_sb_k, cache_sb_v, cache_df_k, cache_df_v, cache_mla_latent, cache_mla_krope)
    past = cache_sb_k.shape[2]
    t_s = x_sample.shape[1]
    m_s = x_sample.shape[0] * t_s
    tk_s = _pick(past, (256, 128, 64))
    out_s = _trunk(x_sample, p_sample, past, caches, W, tm=_pick(m_s, (128,)), tf=tf, tq=t_s, width=tk_s,
                   sb_tq=t_s, sb_tk=tk_s, sb_groups=SB_HEADS // 2, df_groups=DF_HEADS, mla_groups=MLA_HEADS // 2)

    (y_p, sb_k_p, sb_v_p, df_k_p, df_v_p, lat_p, kr_p) = out_p
    (y_s, sb_k_s, sb_v_s, df_k_s, df_v_s, lat_s, kr_s) = out_s
    return (y_p, y_s, sb_k_p, sb_v_p, df_k_p, df_v_p, lat_p, kr_p,
            sb_k_s, sb_v_s, df_k_s, df_v_s, lat_s, kr_s)
```

```python
import functools
import math

import jax
import jax.numpy as jnp
from jax import lax
from jax.experimental import pallas as pl
from jax.experimental.pallas import tpu as pltpu

F32 = jnp.float32
BF16 = jnp.bfloat16

EPS = 1e-6
CHUNK = 64
SB_HEADS, SB_HD = 8, 64
DF_HEADS, DF_HD = 4, 64
DF_ROT = DF_HD // 4
ROPE_THETA = 500000.0
MLA_HEADS, MLA_NOPE, MLA_ROPE, MLA_VD = 16, 64, 32, 64
Q_LORA, KV_LORA = 384, 256
MLA_THETA = 10000.0
MLA_QD = MLA_NOPE + MLA_ROPE
LOG2E = math.log2(math.e)
SB_UNDERFLOW = 151.0
SB_LINEAR = 64.0
BUCKET = 512
ROW_TILE = 512
Q_TILE = 256

LANES = 128
SB_W = SB_HEADS * SB_HD
DF_W = DF_HEADS * 2 * DF_HD
MLA_QW = MLA_HEADS * LANES
MLA_VW = MLA_HEADS * MLA_VD
MLA_IN_PAD = -(-(Q_LORA + KV_LORA + MLA_ROPE) // LANES) * LANES

VMEM_LIMIT = 56 * 1024 * 1024


def _cparams(sem):
    return pltpu.CompilerParams(dimension_semantics=sem, vmem_limit_bytes=VMEM_LIMIT)


def _rms(x, g):
    return x * lax.rsqrt(jnp.mean(x * x, axis=-1, keepdims=True) + EPS) * g


def _dot(a, b):
    return jnp.dot(a, b, preferred_element_type=F32)


def _dot_t(a, b):
    return lax.dot_general(a, b, (((1,), (1,)), ((), ())), preferred_element_type=F32)


def _rope_slab(x, cos, sin_up, sin_dn, shift):
    n = x.shape[-1]
    return x * cos + pltpu.roll(x, n - shift, 1) * sin_up + pltpu.roll(x, shift, 1) * sin_dn


def _swiglu(h, wgu_ref, wd_ref, chunk):
    d_ff = wd_ref.shape[0]
    acc = None
    for c in range(d_ff // chunk):
        g = _dot(h, wgu_ref[:, c * chunk:(c + 1) * chunk])
        u = _dot(h, wgu_ref[:, d_ff + c * chunk:d_ff + (c + 1) * chunk])
        t = _dot((g * jax.nn.sigmoid(g) * u).astype(BF16), wd_ref[c * chunk:(c + 1) * chunk, :])
        acc = t if acc is None else acc + t
    return acc


def _whole(a):
    return pl.BlockSpec(a.shape, lambda *_: (0,) * a.ndim, pipeline_mode=pl.Buffered(1))


def _layer(stack, layer):
    return pl.BlockSpec((None,) + stack.shape[1:], lambda *_: (layer,) + (0,) * (stack.ndim - 1),
                        pipeline_mode=pl.Buffered(1))


def _ffn_kernel(x_ref, gin_ref, gout_ref, wgu_ref, wd_ref, o_ref, *, chunk):
    x = x_ref[...]
    h = _rms(x, gin_ref[...]).astype(BF16)
    o_ref[...] = x + 0.5 * _rms(_swiglu(h, wgu_ref, wd_ref, chunk), gout_ref[...])


def _ffn_block(x, g_in, g_out, w_gu, w_dn, layer, tm, tf):
    m, d = x.shape
    row = lambda i: (i, 0)
    return pl.pallas_call(
        functools.partial(_ffn_kernel, chunk=tf),
        grid=(m // tm,),
        in_specs=[pl.BlockSpec((tm, d), row), _whole(g_in), _whole(g_out), _layer(w_gu, layer),
                  _layer(w_dn, layer)],
        out_specs=pl.BlockSpec((tm, d), row),
        out_shape=jax.ShapeDtypeStruct((m, d), F32),
        compiler_params=_cparams(("parallel",)),
        name="ffn_block",
    )(x, g_in, g_out, w_gu, w_dn)


def _ab_in_kernel(x_ref, g_ref, w_ref, cos_ref, sup_ref, sdn_ref,
                  sbq_ref, sbk_ref, sbv_ref, dfq_ref, dfk_ref, dfv_ref,
                  sbk_f_ref, sbv_f_ref, dfk_f_ref, dfv_f_ref):
    h = _rms(x_ref[...], g_ref[...]).astype(BF16)
    cos, sup, sdn = cos_ref[...], sup_ref[...], sdn_ref[...]

    def proj(c):
        return _dot(h, w_ref[:, c * SB_W:(c + 1) * SB_W])

    def store_rows(out_f, x, first):
        for j in range(x.shape[-1] // SB_HD):
            out_f[pl.ds(first + j, x.shape[0], stride=SB_W // SB_HD), :] = x[:, j * SB_HD:(j + 1) * SB_HD]

    def rope(x, scale, out_b, out_f):
        for s in range(x.shape[-1] // LANES):
            sl = slice(s * LANES, (s + 1) * LANES)
            r = _rope_slab(x[:, sl], cos, sup, sdn, DF_ROT // 2)
            if out_f is not None:
                store_rows(out_f, r, s * (LANES // SB_HD))
            out_b[:, sl] = (r * scale).astype(BF16)

    sbq_ref[...] = (proj(0) * (SB_HD ** -0.5 * LOG2E)).astype(BF16)
    sbk = proj(1)
    store_rows(sbk_f_ref, sbk, 0)
    sbk_ref[...] = sbk.astype(BF16)
    sbv = proj(2)
    store_rows(sbv_f_ref, sbv, 0)
    sbv_ref[...] = sbv.astype(BF16)
    rope(proj(3), DF_HD ** -0.5 * LOG2E, dfq_ref, None)
    rope(proj(4), 1.0, dfk_ref, dfk_f_ref)
    dfv = proj(5)
    for j in range(DF_HEADS):
        dfv_f_ref[pl.ds(j, dfv.shape[0], stride=DF_HEADS), :] = dfv[:, j * LANES:(j + 1) * LANES]
    dfv_ref[...] = dfv.astype(BF16)


def _ab_in_proj(x, g, w_in, tabs, tm):
    m, d = x.shape
    n_tab = tabs[0].shape[0] // tm
    row = lambda i: (i, 0)
    const = lambda i: (0, 0)
    tab = lambda i: (i % n_tab, 0)
    wide = pl.BlockSpec((tm, SB_W), row)
    bf = jax.ShapeDtypeStruct((m, SB_W), BF16)
    n_rows = SB_W // SB_HD
    tall = pl.BlockSpec((tm * n_rows, SB_HD), row)
    f32_tall = jax.ShapeDtypeStruct((m * n_rows, SB_HD), F32)
    return pl.pallas_call(
        _ab_in_kernel,
        grid=(m // tm,),
        in_specs=[
            pl.BlockSpec((tm, d), row),
            pl.BlockSpec((1, d), const),
            pl.BlockSpec(w_in.shape, const),
            pl.BlockSpec((tm, LANES), tab),
            pl.BlockSpec((tm, LANES), tab),
            pl.BlockSpec((tm, LANES), tab),
        ],
        out_specs=[wide] * 6 + [tall] * 3 + [pl.BlockSpec((tm * DF_HEADS, LANES), row)],
        out_shape=[bf] * 6 + [f32_tall] * 3 + [jax.ShapeDtypeStruct((m * DF_HEADS, LANES), F32)],
        compiler_params=_cparams(("parallel",)),
        name="ab_in_proj",
    )(x, g, w_in, *tabs)


def _post_mixer_kernel(*refs, n_parts, chunk):
    (x_ref, p_ref, g_ref, wo_ref, wgu_ref, wd_ref, wgate_ref, wple_ref) = refs[:8]
    parts = refs[8:8 + n_parts]
    o_ref = refs[8 + n_parts]
    g_mix, g_in, g_out, g_gate, g_ple = (g_ref[i] for i in range(5))
    off = 0
    mix = None
    for p in parts:
        k = p.shape[-1]
        t = _dot(p[...], wo_ref[off:off + k, :])
        mix = t if mix is None else mix + t
        off += k
    x1 = x_ref[...] + _rms(mix, g_mix)
    x2 = x1 + 0.5 * _rms(_swiglu(_rms(x1, g_in).astype(BF16), wgu_ref, wd_ref, chunk), g_out)
    gate = jax.nn.sigmoid(_dot(_rms(x2, g_gate).astype(BF16), wgate_ref[...]))
    emb = _dot(p_ref[...].astype(BF16), wple_ref[...])
    o_ref[...] = x2 + _rms(gate * emb, g_ple)


def _post_mixer(x, p, layer, gains, w_out, w_gu, w_dn, w_gate, w_ple, parts, tm, tf):
    m, d = x.shape
    row = lambda i: (i, 0)
    return pl.pallas_call(
        functools.partial(_post_mixer_kernel, n_parts=len(parts), chunk=tf),
        grid=(m // tm,),
        in_specs=[
            pl.BlockSpec((tm, d), row),
            pl.BlockSpec((None, tm, p.shape[-1]), lambda i: (layer, i, 0)),
        ] + [_whole(gains), _whole(w_out)] + [_layer(a, layer) for a in (w_gu, w_dn, w_gate, w_ple)]
        + [pl.BlockSpec((tm, a.shape[-1]), row) for a in parts],
        out_specs=pl.BlockSpec((tm, d), row),
        out_shape=jax.ShapeDtypeStruct((m, d), F32),
        compiler_params=_cparams(("parallel",)),
        name="post_mixer",
    )(x, p, gains, w_out, w_gu, w_dn, w_gate, w_ple, *parts)


def _mla_in_kernel(x_ref, g_ref, w_ref, qn_ref, kvn_ref, wuq_ref,
                   qcos_ref, qsin_ref, kcos_ref, ksup_ref, ksdn_ref,
                   q_ref, lat_ref, kr_ref):
    h = _rms(x_ref[...], g_ref[...]).astype(BF16)
    proj = _dot(h, w_ref[...])
    c_q = _rms(proj[:, :Q_LORA], qn_ref[...]).astype(BF16)
    lat_ref[...] = _rms(proj[:, Q_LORA:Q_LORA + KV_LORA], kvn_ref[...])
    kr = _rope_slab(proj[:, Q_LORA + KV_LORA:], kcos_ref[...], ksup_ref[...], ksdn_ref[...],
                    MLA_ROPE // 2)
    kr_ref[...] = kr[:, :MLA_ROPE]
    q2 = _dot(c_q, wuq_ref[...])
    cos, sin = qcos_ref[...], qsin_ref[...]
    for hd in range(MLA_HEADS):
        sl = slice(hd * LANES, (hd + 1) * LANES)
        rot = slice(MLA_QW + hd * LANES, MLA_QW + (hd + 1) * LANES)
        r = q2[:, sl] * cos + q2[:, rot] * sin
        q_ref[:, sl] = (r * (MLA_QD ** -0.5 * LOG2E)).astype(BF16)


def _mla_in_proj(x, g, w_in, q_norm, kv_norm, w_uq, qtabs, ktabs, tm):
    m, d = x.shape
    n_tab = qtabs[0].shape[0] // tm
    row = lambda i: (i, 0)
    const = lambda i: (0, 0)
    tab = lambda i: (i % n_tab, 0)
    tspec = pl.BlockSpec((tm, LANES), tab)
    return pl.pallas_call(
        _mla_in_kernel,
        grid=(m // tm,),
        in_specs=[
            pl.BlockSpec((tm, d), row),
            pl.BlockSpec((1, d), const),
            pl.BlockSpec(w_in.shape, const),
            pl.BlockSpec((1, Q_LORA), const),
            pl.BlockSpec((1, KV_LORA), const),
            pl.BlockSpec(w_uq.shape, const),
        ] + [tspec] * 5,
        out_specs=[
            pl.BlockSpec((tm, MLA_QW), row),
            pl.BlockSpec((tm, KV_LORA), row),
            pl.BlockSpec((tm, MLA_ROPE), row),
        ],
        out_shape=[
            jax.ShapeDtypeStruct((m, MLA_QW), BF16),
            jax.ShapeDtypeStruct((m, KV_LORA), F32),
            jax.ShapeDtypeStruct((m, MLA_ROPE), F32),
        ],
        compiler_params=_cparams(("parallel",)),
        name="mla_in_proj",
    )(x, g, w_in, q_norm, kv_norm, w_uq, *qtabs, *ktabs)


def _mla_kv_kernel(lat_ref, kr_ref, wuk_ref, place_ref, wuv_ref, k_ref, v_ref):
    lat = lat_ref[...].astype(BF16)
    k_ref[...] = (_dot(lat, wuk_ref[...]) + _dot(kr_ref[...].astype(BF16), place_ref[...])).astype(BF16)
    v_ref[...] = _dot(lat, wuv_ref[...]).astype(BF16)


def _mla_kv_expand(lat, kr, w_uk, place, w_uv, tm):
    m = lat.shape[0]
    row = lambda i: (i, 0)
    const = lambda i: (0, 0)
    return pl.pallas_call(
        _mla_kv_kernel,
        grid=(m // tm,),
        in_specs=[
            pl.BlockSpec((tm, KV_LORA), row),
            pl.BlockSpec((tm, MLA_ROPE), row),
            pl.BlockSpec(w_uk.shape, const),
            pl.BlockSpec(place.shape, const),
            pl.BlockSpec(w_uv.shape, const),
        ],
        out_specs=[pl.BlockSpec((tm, MLA_QW), row), pl.BlockSpec((tm, MLA_VW), row)],
        out_shape=[jax.ShapeDtypeStruct((m, MLA_QW), BF16), jax.ShapeDtypeStruct((m, MLA_VW), BF16)],
        compiler_params=_cparams(("parallel",)),
        name="mla_kv_expand",
    )(lat, kr, w_uk, place, w_uv)


def _positions(q0, kb, tq, tk):
    t = q0 + lax.broadcasted_iota(jnp.int32, (tq, tk), 0)
    s = kb * tk + lax.broadcasted_iota(jnp.int32, (tq, tk), 1)
    return t, s


def _sb_phases(q_ref, k_ref, v_ref, q0, *, tq, tk):
    n_full = q0 // tk
    q = q_ref[0]
    groups = q.shape[-1] // LANES
    lane = lax.broadcasted_iota(jnp.int32, (tq, LANES), 1)
    q_heads = []
    for g in range(groups):
        qg = q[:, g * LANES:(g + 1) * LANES]
        q_heads += [jnp.where(lane < SB_HD, qg, jnp.zeros_like(qg)),
                    jnp.where(lane >= SB_HD, qg, jnp.zeros_like(qg))]
    group_of = lambda h: slice((h // 2) * LANES, (h // 2 + 1) * LANES)
    later = (lax.broadcasted_iota(jnp.int32, (tk, tk), 0)
             > lax.broadcasted_iota(jnp.int32, (tk, tk), 1)).astype(BF16)

    def load(kb):
        start = pl.multiple_of(kb * tk, tk)
        return k_ref[0, pl.ds(start, tk), :], v_ref[0, pl.ds(start, tk), :]

    def miss_cost(z):
        cost = jnp.maximum(jnp.log2(1.0 + jnp.exp2(jnp.minimum(z, SB_LINEAR))), z)
        return cost, z - cost

    def cost_after(cost):
        hi = cost.astype(BF16)
        lo = (cost - hi.astype(F32)).astype(BF16)
        return _dot(hi, later) + _dot(lo, later)

    def block_total(cost, after):
        return after[:, 0:1] + cost[:, 0:1]

    lower = jnp.maximum(n_full - 1, 0)

    def first_step():
        (k_lo, v_lo), (k_up, v_up) = load(lower), load(lower + 1)
        t, s_lo = _positions(q0, lower, tq, tk)
        earlier_lo = s_lo < t
        earlier_up = s_lo + tk < t
        state = []
        for h, qh in enumerate(q_heads):
            sl = group_of(h)
            cost_up, hit_up = miss_cost(_dot_t(qh, k_up[:, sl]))
            cost_lo, hit_lo = miss_cost(_dot_t(qh, k_lo[:, sl]))
            cost_up = jnp.where(earlier_up, cost_up, 0.0)
            cost_lo = jnp.where(earlier_lo, cost_lo, 0.0)
            after_up, after_lo = cost_after(cost_up), cost_after(cost_lo)
            run_up = block_total(cost_up, after_up)
            w_up = jnp.where(earlier_up, jnp.exp2(hit_up - after_up), 0.0)
            w_lo = jnp.where(earlier_lo, jnp.exp2(hit_lo - after_lo - run_up), 0.0)
            acc = _dot(w_up.astype(BF16), v_up[:, sl]) + _dot(w_lo.astype(BF16), v_lo[:, sl])
            state.append((run_up + block_total(cost_lo, after_lo), acc))
        return state

    def least_run(st):
        least = st[0][0]
        for run, _ in st[1:]:
            least = jnp.minimum(least, run)
        return jnp.min(least)

    def more(c):
        kb, least, _ = c
        return (kb >= 0) & (least < SB_UNDERFLOW)

    def block(c):
        kb, _, st = c
        ks, vs = load(kb)
        new = []
        for h, (qh, (run, acc)) in enumerate(zip(q_heads, st)):
            cost, hit = miss_cost(_dot_t(qh, ks[:, group_of(h)]))
            after = cost_after(cost)
            acc = acc + _dot(jnp.exp2(hit - after - run).astype(BF16), vs[:, group_of(h)])
            new.append((run + block_total(cost, after), acc))
        return kb - 1, least_run(new), tuple(new)

    def finish(state):
        _, _, state = lax.while_loop(more, block, (lower - 1, least_run(state), tuple(state)))
        return [jnp.where(lane < SB_HD, state[2 * g][1], state[2 * g + 1][1]) for g in range(groups)]

    return first_step, finish


def _sb_kernel(q_ref, k_ref, v_ref, o_ref, *, tq, tk, q_start):
    first_step, finish = _sb_phases(q_ref, k_ref, v_ref, q_start + pl.program_id(2) * tq, tq=tq, tk=tk)
    for g, out in enumerate(finish(first_step())):
        o_ref[0, :, _lane_slab(g)] = out.astype(o_ref.dtype)


def _softmax_bucketed(chains, k_ref, v_ref, q0, emit, *, tq, width, q_range, n_keys):
    bucket = q0 // width
    for j in range(q_range[0] // width, (q_range[1] - 1) // width + 1):
        @pl.when(bucket == j)
        def _(j=j):
            length = (j + 1) * width
            t_chunk = (q0 + lax.broadcasted_iota(jnp.int32, (tq, 1), 0)) // CHUNK
            s_pos = length - width + lax.broadcasted_iota(jnp.int32, (1, width), 1)
            s_chunk = jnp.where(s_pos < n_keys, s_pos // CHUNK, jnp.iinfo(jnp.int32).max)
            vis = s_chunk <= t_chunk
            outs = []
            for qc, k_sl, v_sl in chains:
                s = _dot_t(qc, k_ref[0, 0:length, k_sl])
                tail = jnp.where(vis, s[:, length - width:], -jnp.inf)
                s = tail if length == width else jnp.concatenate([s[:, :length - width], tail], axis=1)
                p = jnp.exp2(s - jnp.max(s, axis=-1, keepdims=True))
                l = jnp.sum(p, axis=-1, keepdims=True)
                outs.append(_dot(p.astype(BF16), v_ref[0, 0:length, v_sl]) / l)
            emit(outs)


def _lane_slab(i):
    return slice(i * LANES, (i + 1) * LANES)


def _df_kernel(lam_ref, sub_ref, q_ref, k_ref, v_ref, o_ref, *, tq, width, q_range, n_keys, lam_init):
    q0 = q_range[0] + pl.program_id(2) * tq
    q = q_ref[0]
    heads = q.shape[-1] // LANES
    lane = lax.broadcasted_iota(jnp.int32, (tq, LANES), 1)
    chains = []
    for h in range(heads):
        qh = q[:, _lane_slab(h)]
        chains.append((jnp.where(lane < DF_HD, qh, jnp.zeros_like(qh)), _lane_slab(h), _lane_slab(h)))
        chains.append((jnp.where(lane >= DF_HD, qh, jnp.zeros_like(qh)), _lane_slab(h), _lane_slab(h)))

    def emit(outs):
        lv = lam_ref[...]
        lam = (jnp.exp(jnp.sum(lv[0:1] * lv[1:2], axis=-1, keepdims=True))
               - jnp.exp(jnp.sum(lv[2:3] * lv[3:4], axis=-1, keepdims=True)) + lam_init)
        for h in range(heads):
            out = outs[2 * h] - lam * outs[2 * h + 1]
            o_ref[0, :, _lane_slab(h)] = (_rms(out, sub_ref[...]) * (1.0 - lam_init)).astype(o_ref.dtype)

    _softmax_bucketed(chains, k_ref, v_ref, q0, emit, tq=tq, width=width, q_range=q_range, n_keys=n_keys)


def _mla_kernel(q_ref, k_ref, v_ref, o_ref, *, tq, width, q_range, n_keys):
    q0 = q_range[0] + pl.program_id(2) * tq
    q = q_ref[0]
    heads = q.shape[-1] // LANES
    lane = lax.broadcasted_iota(jnp.int32, (tq, LANES), 1)
    chains = [(q[:, _lane_slab(h)], _lane_slab(h), _lane_slab(h // 2)) for h in range(heads)]

    def emit(outs):
        for g in range(heads // 2):
            o_ref[0, :, _lane_slab(g)] = jnp.where(
                lane < MLA_VD, outs[2 * g], outs[2 * g + 1]).astype(o_ref.dtype)

    _softmax_bucketed(chains, k_ref, v_ref, q0, emit, tq=tq, width=width, q_range=q_range, n_keys=n_keys)


def _mla_cached_kernel(q_ref, latc_ref, krc_ref, latn_ref, krn_ref, wuk_ref, place_ref, wuv_ref, o_ref,
                       *, q_start):
    q = q_ref[0]
    t = q.shape[0]
    slabs = [slice(h * LANES, (h + 1) * LANES) for h in range(MLA_HEADS)]
    q_all = jnp.concatenate([q[:, sl] for sl in slabs], axis=0)
    q_lat = jnp.concatenate([_dot_t(q[:, sl], wuk_ref[:, sl]) for sl in slabs], axis=0).astype(BF16)
    rows = MLA_HEADS * t
    pad = LANES - t

    def scores(lat, kr):
        kr_lanes = _dot(kr.astype(BF16), place_ref[...]).astype(BF16)
        return _dot_t(q_lat, lat) + _dot_t(q_all, kr_lanes)

    lat_c = latc_ref[0].astype(BF16)
    s_c = scores(lat_c, krc_ref[0])
    lat_n = jnp.concatenate([latn_ref[0], jnp.zeros((pad, KV_LORA), F32)], axis=0).astype(BF16)
    kr_n = jnp.concatenate([krn_ref[0], jnp.zeros((pad, MLA_ROPE), F32)], axis=0)
    t_pos = q_start + lax.broadcasted_iota(jnp.int32, (rows, LANES), 0) % t
    j = lax.broadcasted_iota(jnp.int32, (rows, LANES), 1)
    vis = (j < t) & (((q_start + j) // CHUNK) <= (t_pos // CHUNK))
    s_n = jnp.where(vis, scores(lat_n, kr_n), -jnp.inf)

    m = jnp.maximum(jnp.max(s_c, axis=-1, keepdims=True), jnp.max(s_n, axis=-1, keepdims=True))
    p_c, p_n = jnp.exp2(s_c - m), jnp.exp2(s_n - m)
    l = jnp.sum(p_c, axis=-1, keepdims=True) + jnp.sum(p_n, axis=-1, keepdims=True)
    o_lat = (_dot(p_c.astype(BF16), lat_c) + _dot(p_n.astype(BF16), lat_n)) / l
    full = _dot(o_lat.astype(BF16), wuv_ref[...])
    lane_head = lax.broadcasted_iota(jnp.int32, (t, MLA_VW), 1) // MLA_VD
    out = jnp.zeros((t, MLA_VW), F32)
    for h in range(MLA_HEADS):
        out = out + jnp.where(lane_head == h, full[h * t:(h + 1) * t, :], 0.0)
    o_ref[0] = out.astype(o_ref.dtype)


def _mla_cached_attention(q, lat_cache, kr_cache, lat_new, kr_new, w_uk, place, w_uv, q_start):
    b, t, _ = q.shape
    past = lat_cache.shape[1]
    assert t <= LANES and q_start == past
    per_b = lambda bi: (bi, 0, 0)
    const = lambda bi: (0, 0)
    return pl.pallas_call(
        functools.partial(_mla_cached_kernel, q_start=q_start),
        grid=(b,),
        in_specs=[
            pl.BlockSpec((1, t, MLA_QW), per_b),
            pl.BlockSpec((1, past, KV_LORA), per_b),
            pl.BlockSpec((1, past, MLA_ROPE), per_b),
            pl.BlockSpec((1, t, KV_LORA), per_b),
            pl.BlockSpec((1, t, MLA_ROPE), per_b),
            pl.BlockSpec(w_uk.shape, const),
            pl.BlockSpec(place.shape, const),
            pl.BlockSpec(w_uv.shape, const),
        ],
        out_specs=pl.BlockSpec((1, t, MLA_VW), per_b),
        out_shape=jax.ShapeDtypeStruct((b, t, MLA_VW), BF16),
        compiler_params=_cparams(("parallel",)),
        name="mla_cached_attn",
    )(q, lat_cache, kr_cache, lat_new, kr_new, w_uk, place, w_uv)


def _attention(kernel_fn, q, k, v, extra, *, q_lanes, k_lanes, tq, tk, name, v_groups=1):
    b, t_q, _ = q.shape
    t_k = k.shape[1]
    groups = v.shape[-1] // (LANES * v_groups)
    q_lanes, k_lanes, v_lanes = q_lanes * v_groups, k_lanes * v_groups, LANES * v_groups
    assert t_q % tq == 0 and t_k % tk == 0 and tk % tq == 0 and tk % CHUNK == 0
    const = lambda bi, g, qi: (0, 0)
    return pl.pallas_call(
        kernel_fn,
        grid=(b, groups, t_q // tq),
        in_specs=[pl.BlockSpec(e.shape, const) for e in extra] + [
            pl.BlockSpec((1, tq, q_lanes), lambda bi, g, qi: (bi, qi, g)),
            pl.BlockSpec((1, t_k, k_lanes), lambda bi, g, qi: (bi, 0, g)),
            pl.BlockSpec((1, t_k, v_lanes), lambda bi, g, qi: (bi, 0, g)),
        ],
        out_specs=pl.BlockSpec((1, tq, v_lanes), lambda bi, g, qi: (bi, qi, g)),
        out_shape=jax.ShapeDtypeStruct((b, t_q, groups * v_lanes), BF16),
        compiler_params=_cparams(("parallel", "parallel", "arbitrary")),
        name=name,
    )(*extra, q, k, v)


def _rope_tables(pos, rot, theta, period, offset):
    half = rot // 2
    inv = jnp.float32(theta) ** (-(jnp.arange(half, dtype=F32) * (2.0 / rot)))
    ang = pos.astype(F32)[:, None] * inv[None, :]
    cos, sin = jnp.cos(ang), jnp.sin(ang)
    lane = jnp.arange(LANES) % period - offset
    first = (lane >= 0) & (lane < half)
    second = (lane >= half) & (lane < rot)
    idx = jnp.clip(jnp.where(second, lane - half, lane), 0, half - 1)
    cos_l, sin_l = cos[:, idx], sin[:, idx]
    cos_t = jnp.where(first | second, cos_l, 1.0)
    sin_up = jnp.where(first, -sin_l, 0.0)
    sin_dn = jnp.where(second, sin_l, 0.0)
    return cos_t, sin_up, sin_dn


def _pad_heads(w, heads, width):
    k = w.shape[0]
    w = w.reshape(k, heads, width)
    return jnp.pad(w, ((0, 0), (0, 0), (0, LANES - width))).reshape(k, heads * LANES)


def _with_rotation_partner(w):
    k = w.shape[0]
    half = MLA_ROPE // 2
    w3 = w.reshape(k, MLA_HEADS, LANES)
    x1 = w3[..., MLA_NOPE:MLA_NOPE + half]
    x2 = w3[..., MLA_NOPE + half:MLA_NOPE + MLA_ROPE]
    rot = jnp.concatenate([jnp.zeros_like(w3[..., :MLA_NOPE]), -x2, x1,
                           jnp.zeros_like(w3[..., MLA_QD:])], axis=-1)
    return jnp.concatenate([w, rot.reshape(k, MLA_HEADS * LANES)], axis=-1)


def _with_cache(cache, new, t_pad, dtype):
    b, past, w = cache.shape
    pad = jnp.zeros((b, t_pad - past - new.shape[1], w), dtype)
    return jnp.concatenate([cache.astype(dtype), new.astype(dtype), pad], axis=1)


def _trunk(x, p, start, caches, W, *, tm, tf, tq, width, sb_tq, sb_tk, sb_groups, df_groups, mla_groups):
    b, t, d = x.shape
    m = b * t
    depth = p.shape[0]
    x = x.reshape(m, d)
    pos = start + jnp.arange(t)
    if t % tm == 0:
        tab_pos = pos
    else:
        tab_pos = jnp.tile(pos, m // t)
    df_tabs = _rope_tables(tab_pos, DF_ROT, ROPE_THETA, DF_HD, 0)
    q_cos, q_sin_up, q_sin_dn = _rope_tables(tab_pos, MLA_ROPE, MLA_THETA, LANES, MLA_NOPE)
    mq_tabs = (q_cos, q_sin_dn - q_sin_up)
    mk_tabs = _rope_tables(tab_pos, MLA_ROPE, MLA_THETA, LANES, 0)
    n_keys = start + t
    t_keys = -(-n_keys // width) * width
    assert t_keys % sb_tk == 0 and t_keys >= 2 * sb_tk and sb_tq <= sb_tk
    q_range = (start, start + t)

    ab_rows, mla_rows = [], []
    for i in range(depth):
        g = W['norms'][i][:, None, :]
        j = i // 2
        x = _ffn_block(x, g[0], g[1], W['ffn1_gu'], W['ffn1_dn'], i, tm, tf)
        if i % 2 == 0:
            (sbq, sbk, sbv, dfq, dfk, dfv, sbk_f, sbv_f, dfk_f, dfv_f) = _ab_in_proj(
                x, g[2], W['ab_in'][j], df_tabs, tm)
            ab_rows.append((sbk_f.reshape(b, t, SB_HEADS, SB_HD), sbv_f.reshape(b, t, SB_HEADS, SB_HD),
                            dfk_f.reshape(b, t, DF_HEADS, 2, DF_HD), dfv_f.reshape(b, t, DF_HEADS, 2 * DF_HD)))
            three = lambda a: a.reshape(b, t, a.shape[-1])
            if caches is None:
                keys = [three(a) for a in (sbk, sbv, dfk, dfv)]
            else:
                keys = [_with_cache(c[j].reshape(b, start, -1), n.reshape(b, t, -1), t_keys, BF16)
                        for c, n in zip(caches[:4], (sbk_f, sbv_f, dfk_f, dfv_f))]
            lam_init = 0.8 - 0.6 * math.exp(-0.3 * i)
            df_extra = [W['df_lambda'][j], W['df_subln'][j][None, :]]
            sb_out = _attention(
                functools.partial(_sb_kernel, tq=sb_tq, tk=sb_tk, q_start=start),
                three(sbq), keys[0], keys[1], [], q_lanes=LANES, k_lanes=LANES, tq=sb_tq, tk=sb_tk,
                name="sb_attn", v_groups=sb_groups)
            df_out = _attention(
                functools.partial(_df_kernel, tq=tq, width=width, q_range=q_range, n_keys=n_keys,
                                  lam_init=lam_init),
                three(dfq), keys[2], keys[3], df_extra,
                q_lanes=LANES, k_lanes=LANES, tq=tq, tk=width, name="df_attn", v_groups=df_groups)
            parts = [sb_out.reshape(m, SB_W), df_out.reshape(m, DF_W)]
            w_out = W['ab_out'][j]
        else:
            q, lat, kr = _mla_in_proj(x, g[2], W['mla_in'][j], W['mla_q_norm'][j][None, :],
                                      W['mla_kv_norm'][j][None, :], W['mla_uq'][j], mq_tabs, mk_tabs, tm)
            mla_rows.append((lat.reshape(b, t, KV_LORA), kr.reshape(b, t, MLA_ROPE)))
            if caches is None:
                k_all, v_all = _mla_kv_expand(lat, kr, W['mla_uk'][j], W['mla_place'], W['mla_uv'][j], tm)
                out = _attention(
                    functools.partial(_mla_kernel, tq=tq, width=width, q_range=q_range, n_keys=n_keys),
                    q.reshape(b, t, MLA_QW), k_all.reshape(b, -1, MLA_QW), v_all.reshape(b, -1, MLA_VW), [],
                    q_lanes=2 * LANES, k_lanes=2 * LANES, tq=tq, tk=width, name="mla_attn",
                    v_groups=mla_groups)
            else:
                out = _mla_cached_attention(
                    q.reshape(b, t, MLA_QW), caches[4][j], caches[5][j], lat.reshape(b, t, KV_LORA),
                    kr.reshape(b, t, MLA_ROPE), W['mla_uk'][j], W['mla_place'][:, :LANES], W['mla_uv'][j], start)
            parts = [out.reshape(m, MLA_VW)]
            w_out = W['mla_out'][j]
        x = _post_mixer(x, p.reshape(depth, m, -1), i, g[3:8], w_out, W['ffn2_gu'], W['ffn2_dn'],
                        W['ple_gate'], W['ple_in'], parts, tm, tf)
    sb_k, sb_v, df_k, df_v = [jnp.stack(r, axis=0) for r in zip(*ab_rows)]
    lat, kr = [jnp.stack(r, axis=0) for r in zip(*mla_rows)]
    return x.reshape(b, t, d), sb_k, sb_v, df_k, df_v, lat, kr


def _pick(n, candidates):
    for c in candidates:
        if n % c == 0:
            return c
    return n


def _cast_kernel(x_ref, o_ref):
    o_ref[...] = x_ref[...].astype(o_ref.dtype)


def _to_bf16(w):
    n_l, k, n = w.shape
    bk = _pick(k, (256, 128, 64, 32, 16))
    spec = pl.BlockSpec((1, bk, n), lambda l, i: (l, i, 0))
    return pl.pallas_call(
        _cast_kernel, grid=(n_l, k // bk), in_specs=[spec], out_specs=spec,
        out_shape=jax.ShapeDtypeStruct(w.shape, BF16),
        compiler_params=_cparams(("parallel", "parallel")), name="to_bf16",
    )(w)


def kernel(x_prompt, x_sample, p_prompt, p_sample, cache_sb_k, cache_sb_v, cache_df_k, cache_df_v,
           cache_mla_latent, cache_mla_krope, norms, ffn1_gu, ffn1_dn, ffn2_gu, ffn2_dn, ple_in, ple_gate,
           ab_in, ab_out, df_lambda, df_subln, mla_in, mla_q_norm, mla_kv_norm, mla_uq, mla_uk, mla_uv, mla_out):
    bf = lambda a: a.astype(BF16)
    n_odd = mla_in.shape[0]
    place = jnp.zeros((MLA_ROPE, MLA_HEADS, LANES), F32)
    place = place.at[jnp.arange(MLA_ROPE), :, MLA_NOPE + jnp.arange(MLA_ROPE)].set(1.0)
    W = dict(
        norms=norms, ffn1_gu=_to_bf16(ffn1_gu), ffn1_dn=_to_bf16(ffn1_dn), ffn2_gu=_to_bf16(ffn2_gu),
        ffn2_dn=_to_bf16(ffn2_dn), ple_in=bf(ple_in), ple_gate=_to_bf16(ple_gate), ab_in=_to_bf16(ab_in),
        ab_out=_to_bf16(ab_out),
        df_lambda=df_lambda, df_subln=df_subln,
        mla_in=bf(jnp.pad(mla_in, ((0, 0), (0, 0), (0, MLA_IN_PAD - mla_in.shape[-1])))),
        mla_q_norm=mla_q_norm, mla_kv_norm=mla_kv_norm,
        mla_uq=bf(jnp.stack([_with_rotation_partner(_pad_heads(mla_uq[j], MLA_HEADS, MLA_QD))
                             for j in range(n_odd)])),
        mla_uk=bf(jnp.stack([_pad_heads(mla_uk[j], MLA_HEADS, MLA_NOPE) for j in range(n_odd)])),
        mla_uv=bf(mla_uv), mla_out=bf(mla_out),
        mla_place=bf(place.reshape(MLA_ROPE, MLA_QW)),
    )
    d_ff = ffn1_dn.shape[1]
    tf = d_ff // 2 if d_ff % (2 * LANES) == 0 else d_ff

    t_p = x_prompt.shape[1]
    m_p = x_prompt.shape[0] * t_p
    width_p = _pick(t_p, (BUCKET, Q_TILE, 128, CHUNK))
    tq_p = _pick(width_p, (Q_TILE, 128, CHUNK))
    out_p = _trunk(x_prompt, p_prompt, 0, None, W, tm=_pick(m_p, (ROW_TILE, 256, 128)), tf=tf, tq=tq_p,
                   width=width_p, sb_tq=tq_p, sb_tk=tq_p, sb_groups=2, df_groups=2, mla_groups=2)

    caches = (cache_sb_k, cache_sb_v, cache_df_k, cache_df_v, cache_mla_latent, cache_mla_krope)
    past = cache_sb_k.shape[2]
    t_s = x_sample.shape[1]
    m_s = x_sample.shape[0] * t_s
    tk_s = _pick(past, (Q_TILE, 128, CHUNK))
    out_s = _trunk(x_sample, p_sample, past, caches, W, tm=_pick(m_s, (128,)), tf=tf, tq=t_s, width=tk_s,
                   sb_tq=t_s, sb_tk=tk_s, sb_groups=SB_HEADS // 2, df_groups=DF_HEADS, mla_groups=MLA_HEADS // 2)

    (y_p, sb_k_p, sb_v_p, df_k_p, df_v_p, lat_p, kr_p) = out_p
    (y_s, sb_k_s, sb_v_s, df_k_s, df_v_s, lat_s, kr_s) = out_s
    return (y_p, y_s, sb_k_p, sb_v_p, df_k_p, df_v_p, lat_p, kr_p,
            sb_k_s, sb_v_s, df_k_s, df_v_s, lat_s, kr_s)
```

```python
import functools
import math

import jax
import jax.numpy as jnp
from jax import lax
from jax.experimental import pallas as pl
from jax.experimental.pallas import tpu as pltpu

F32 = jnp.float32
BF16 = jnp.bfloat16

EPS = 1e-6
CHUNK = 64
SB_HEADS, SB_HD = 8, 64
DF_HEADS, DF_HD = 4, 64
DF_ROT = DF_HD // 4
ROPE_THETA = 500000.0
MLA_HEADS, MLA_NOPE, MLA_ROPE, MLA_VD = 16, 64, 32, 64
Q_LORA, KV_LORA = 384, 256
MLA_THETA = 10000.0
MLA_QD = MLA_NOPE + MLA_ROPE
LOG2E = math.log2(math.e)
SB_UNDERFLOW = 151.0
SB_LINEAR = 64.0
BUCKET = 256
ROW_TILE = 512
Q_TILE = 256

LANES = 128
SB_W = SB_HEADS * SB_HD
DF_W = DF_HEADS * 2 * DF_HD
MLA_QW = MLA_HEADS * LANES
MLA_VW = MLA_HEADS * MLA_VD
MLA_IN_PAD = -(-(Q_LORA + KV_LORA + MLA_ROPE) // LANES) * LANES

VMEM_LIMIT = 56 * 1024 * 1024


def _cparams(sem):
    return pltpu.CompilerParams(dimension_semantics=sem, vmem_limit_bytes=VMEM_LIMIT)


def _rms(x, g):
    return x * lax.rsqrt(jnp.mean(x * x, axis=-1, keepdims=True) + EPS) * g


def _dot(a, b):
    return jnp.dot(a, b, preferred_element_type=F32)


def _dot_t(a, b):
    return lax.dot_general(a, b, (((1,), (1,)), ((), ())), preferred_element_type=F32)


def _rope_slab(x, cos, sin_up, sin_dn, shift):
    n = x.shape[-1]
    return x * cos + pltpu.roll(x, n - shift, 1) * sin_up + pltpu.roll(x, shift, 1) * sin_dn


def _swiglu(h, wgu_ref, wd_ref, chunk):
    d_ff = wd_ref.shape[0]
    acc = None
    for c in range(d_ff // chunk):
        g = _dot(h, wgu_ref[:, c * chunk:(c + 1) * chunk])
        u = _dot(h, wgu_ref[:, d_ff + c * chunk:d_ff + (c + 1) * chunk])
        t = _dot((g * jax.nn.sigmoid(g) * u).astype(BF16), wd_ref[c * chunk:(c + 1) * chunk, :])
        acc = t if acc is None else acc + t
    return acc


def _whole(a):
    return pl.BlockSpec(a.shape, lambda *_: (0,) * a.ndim, pipeline_mode=pl.Buffered(1))


def _layer(stack, layer):
    return pl.BlockSpec((None,) + stack.shape[1:], lambda *_: (layer,) + (0,) * (stack.ndim - 1),
                        pipeline_mode=pl.Buffered(1))


def _ffn_kernel(x_ref, gin_ref, gout_ref, wgu_ref, wd_ref, o_ref, *, chunk):
    x = x_ref[...]
    h = _rms(x, gin_ref[...]).astype(BF16)
    o_ref[...] = x + 0.5 * _rms(_swiglu(h, wgu_ref, wd_ref, chunk), gout_ref[...])


def _ffn_block(x, g_in, g_out, w_gu, w_dn, layer, tm, tf):
    m, d = x.shape
    row = lambda i: (i, 0)
    return pl.pallas_call(
        functools.partial(_ffn_kernel, chunk=tf),
        grid=(m // tm,),
        in_specs=[pl.BlockSpec((tm, d), row), _whole(g_in), _whole(g_out), _layer(w_gu, layer),
                  _layer(w_dn, layer)],
        out_specs=pl.BlockSpec((tm, d), row),
        out_shape=jax.ShapeDtypeStruct((m, d), F32),
        compiler_params=_cparams(("parallel",)),
        name="ffn_block",
    )(x, g_in, g_out, w_gu, w_dn)


def _ab_in_kernel(x_ref, g_ref, w_ref, cos_ref, sup_ref, sdn_ref,
                  sbq_ref, sbk_ref, sbv_ref, dfq_ref, dfk_ref, dfv_ref,
                  sbk_f_ref, sbv_f_ref, dfk_f_ref, dfv_f_ref):
    h = _rms(x_ref[...], g_ref[...]).astype(BF16)
    cos, sup, sdn = cos_ref[...], sup_ref[...], sdn_ref[...]

    def proj(c):
        return _dot(h, w_ref[:, c * SB_W:(c + 1) * SB_W])

    def store_rows(out_f, x, first):
        for j in range(x.shape[-1] // SB_HD):
            out_f[pl.ds(first + j, x.shape[0], stride=SB_W // SB_HD), :] = x[:, j * SB_HD:(j + 1) * SB_HD]

    def rope(x, scale, out_b, out_f):
        for s in range(x.shape[-1] // LANES):
            sl = slice(s * LANES, (s + 1) * LANES)
            r = _rope_slab(x[:, sl], cos, sup, sdn, DF_ROT // 2)
            if out_f is not None:
                store_rows(out_f, r, s * (LANES // SB_HD))
            out_b[:, sl] = (r * scale).astype(BF16)

    sbq_ref[...] = (proj(0) * (SB_HD ** -0.5 * LOG2E)).astype(BF16)
    sbk = proj(1)
    store_rows(sbk_f_ref, sbk, 0)
    sbk_ref[...] = sbk.astype(BF16)
    sbv = proj(2)
    store_rows(sbv_f_ref, sbv, 0)
    sbv_ref[...] = sbv.astype(BF16)
    rope(proj(3), DF_HD ** -0.5 * LOG2E, dfq_ref, None)
    rope(proj(4), 1.0, dfk_ref, dfk_f_ref)
    dfv = proj(5)
    for j in range(DF_HEADS):
        dfv_f_ref[pl.ds(j, dfv.shape[0], stride=DF_HEADS), :] = dfv[:, j * LANES:(j + 1) * LANES]
    dfv_ref[...] = dfv.astype(BF16)


def _ab_in_proj(x, g, w_in, tabs, tm):
    m, d = x.shape
    n_tab = tabs[0].shape[0] // tm
    row = lambda i: (i, 0)
    const = lambda i: (0, 0)
    tab = lambda i: (i % n_tab, 0)
    wide = pl.BlockSpec((tm, SB_W), row)
    bf = jax.ShapeDtypeStruct((m, SB_W), BF16)
    n_rows = SB_W // SB_HD
    tall = pl.BlockSpec((tm * n_rows, SB_HD), row)
    f32_tall = jax.ShapeDtypeStruct((m * n_rows, SB_HD), F32)
    return pl.pallas_call(
        _ab_in_kernel,
        grid=(m // tm,),
        in_specs=[
            pl.BlockSpec((tm, d), row),
            pl.BlockSpec((1, d), const),
            pl.BlockSpec(w_in.shape, const),
            pl.BlockSpec((tm, LANES), tab),
            pl.BlockSpec((tm, LANES), tab),
            pl.BlockSpec((tm, LANES), tab),
        ],
        out_specs=[wide] * 6 + [tall] * 3 + [pl.BlockSpec((tm * DF_HEADS, LANES), row)],
        out_shape=[bf] * 6 + [f32_tall] * 3 + [jax.ShapeDtypeStruct((m * DF_HEADS, LANES), F32)],
        compiler_params=_cparams(("parallel",)),
        name="ab_in_proj",
    )(x, g, w_in, *tabs)


def _post_mixer_kernel(*refs, n_parts, chunk):
    (x_ref, p_ref, g_ref, wo_ref, wgu_ref, wd_ref, wgate_ref, wple_ref) = refs[:8]
    parts = refs[8:8 + n_parts]
    o_ref = refs[8 + n_parts]
    g_mix, g_in, g_out, g_gate, g_ple = (g_ref[i] for i in range(5))
    off = 0
    mix = None
    for p in parts:
        k = p.shape[-1]
        t = _dot(p[...], wo_ref[off:off + k, :])
        mix = t if mix is None else mix + t
        off += k
    x1 = x_ref[...] + _rms(mix, g_mix)
    x2 = x1 + 0.5 * _rms(_swiglu(_rms(x1, g_in).astype(BF16), wgu_ref, wd_ref, chunk), g_out)
    gate = jax.nn.sigmoid(_dot(_rms(x2, g_gate).astype(BF16), wgate_ref[...]))
    emb = _dot(p_ref[...].astype(BF16), wple_ref[...])
    o_ref[...] = x2 + _rms(gate * emb, g_ple)


def _post_mixer(x, p, layer, gains, w_out, w_gu, w_dn, w_gate, w_ple, parts, tm, tf):
    m, d = x.shape
    row = lambda i: (i, 0)
    return pl.pallas_call(
        functools.partial(_post_mixer_kernel, n_parts=len(parts), chunk=tf),
        grid=(m // tm,),
        in_specs=[
            pl.BlockSpec((tm, d), row),
            pl.BlockSpec((None, tm, p.shape[-1]), lambda i: (layer, i, 0)),
        ] + [_whole(gains), _whole(w_out)] + [_layer(a, layer) for a in (w_gu, w_dn, w_gate, w_ple)]
        + [pl.BlockSpec((tm, a.shape[-1]), row) for a in parts],
        out_specs=pl.BlockSpec((tm, d), row),
        out_shape=jax.ShapeDtypeStruct((m, d), F32),
        compiler_params=_cparams(("parallel",)),
        name="post_mixer",
    )(x, p, gains, w_out, w_gu, w_dn, w_gate, w_ple, *parts)


def _mla_in_kernel(x_ref, g_ref, w_ref, qn_ref, kvn_ref, wuq_ref,
                   qcos_ref, qsin_ref, kcos_ref, ksup_ref, ksdn_ref,
                   q_ref, lat_ref, kr_ref):
    h = _rms(x_ref[...], g_ref[...]).astype(BF16)
    proj = _dot(h, w_ref[...])
    c_q = _rms(proj[:, :Q_LORA], qn_ref[...]).astype(BF16)
    lat_ref[...] = _rms(proj[:, Q_LORA:Q_LORA + KV_LORA], kvn_ref[...])
    kr = _rope_slab(proj[:, Q_LORA + KV_LORA:], kcos_ref[...], ksup_ref[...], ksdn_ref[...],
                    MLA_ROPE // 2)
    kr_ref[...] = kr[:, :MLA_ROPE]
    q2 = _dot(c_q, wuq_ref[...])
    cos, sin = qcos_ref[...], qsin_ref[...]
    for hd in range(MLA_HEADS):
        sl = slice(hd * LANES, (hd + 1) * LANES)
        rot = slice(MLA_QW + hd * LANES, MLA_QW + (hd + 1) * LANES)
        r = q2[:, sl] * cos + q2[:, rot] * sin
        q_ref[:, sl] = (r * (MLA_QD ** -0.5 * LOG2E)).astype(BF16)


def _mla_in_proj(x, g, w_in, q_norm, kv_norm, w_uq, qtabs, ktabs, tm):
    m, d = x.shape
    n_tab = qtabs[0].shape[0] // tm
    row = lambda i: (i, 0)
    const = lambda i: (0, 0)
    tab = lambda i: (i % n_tab, 0)
    tspec = pl.BlockSpec((tm, LANES), tab)
    return pl.pallas_call(
        _mla_in_kernel,
        grid=(m // tm,),
        in_specs=[
            pl.BlockSpec((tm, d), row),
            pl.BlockSpec((1, d), const),
            pl.BlockSpec(w_in.shape, const),
            pl.BlockSpec((1, Q_LORA), const),
            pl.BlockSpec((1, KV_LORA), const),
            pl.BlockSpec(w_uq.shape, const),
        ] + [tspec] * 5,
        out_specs=[
            pl.BlockSpec((tm, MLA_QW), row),
            pl.BlockSpec((tm, KV_LORA), row),
            pl.BlockSpec((tm, MLA_ROPE), row),
        ],
        out_shape=[
            jax.ShapeDtypeStruct((m, MLA_QW), BF16),
            jax.ShapeDtypeStruct((m, KV_LORA), F32),
            jax.ShapeDtypeStruct((m, MLA_ROPE), F32),
        ],
        compiler_params=_cparams(("parallel",)),
        name="mla_in_proj",
    )(x, g, w_in, q_norm, kv_norm, w_uq, *qtabs, *ktabs)


def _mla_kv_kernel(lat_ref, kr_ref, wuk_ref, place_ref, wuv_ref, k_ref, v_ref):
    lat = lat_ref[...].astype(BF16)
    k_ref[...] = (_dot(lat, wuk_ref[...]) + _dot(kr_ref[...].astype(BF16), place_ref[...])).astype(BF16)
    v_ref[...] = _dot(lat, wuv_ref[...]).astype(BF16)


def _mla_kv_expand(lat, kr, w_uk, place, w_uv, tm):
    m = lat.shape[0]
    row = lambda i: (i, 0)
    const = lambda i: (0, 0)
    return pl.pallas_call(
        _mla_kv_kernel,
        grid=(m // tm,),
        in_specs=[
            pl.BlockSpec((tm, KV_LORA), row),
            pl.BlockSpec((tm, MLA_ROPE), row),
            pl.BlockSpec(w_uk.shape, const),
            pl.BlockSpec(place.shape, const),
            pl.BlockSpec(w_uv.shape, const),
        ],
        out_specs=[pl.BlockSpec((tm, MLA_QW), row), pl.BlockSpec((tm, MLA_VW), row)],
        out_shape=[jax.ShapeDtypeStruct((m, MLA_QW), BF16), jax.ShapeDtypeStruct((m, MLA_VW), BF16)],
        compiler_params=_cparams(("parallel",)),
        name="mla_kv_expand",
    )(lat, kr, w_uk, place, w_uv)


def _positions(q0, kb, tq, tk):
    t = q0 + lax.broadcasted_iota(jnp.int32, (tq, tk), 0)
    s = kb * tk + lax.broadcasted_iota(jnp.int32, (tq, tk), 1)
    return t, s


def _sb_phases(q_ref, k_ref, v_ref, q0, *, tq, tk):
    n_full = q0 // tk
    q = q_ref[0]
    groups = q.shape[-1] // LANES
    lane = lax.broadcasted_iota(jnp.int32, (tq, LANES), 1)
    q_heads = []
    for g in range(groups):
        qg = q[:, g * LANES:(g + 1) * LANES]
        q_heads += [jnp.where(lane < SB_HD, qg, jnp.zeros_like(qg)),
                    jnp.where(lane >= SB_HD, qg, jnp.zeros_like(qg))]
    group_of = lambda h: slice((h // 2) * LANES, (h // 2 + 1) * LANES)
    later = (lax.broadcasted_iota(jnp.int32, (tk, tk), 0)
             > lax.broadcasted_iota(jnp.int32, (tk, tk), 1)).astype(BF16)

    def load(kb):
        start = pl.multiple_of(kb * tk, tk)
        return k_ref[0, pl.ds(start, tk), :], v_ref[0, pl.ds(start, tk), :]

    def miss_cost(z):
        cost = jnp.maximum(jnp.log2(1.0 + jnp.exp2(jnp.minimum(z, SB_LINEAR))), z)
        return cost, z - cost

    def cost_after(cost):
        hi = cost.astype(BF16)
        lo = (cost - hi.astype(F32)).astype(BF16)
        return _dot(hi, later) + _dot(lo, later)

    def block_total(cost, after):
        return after[:, 0:1] + cost[:, 0:1]

    lower = jnp.maximum(n_full - 1, 0)

    def first_step():
        (k_lo, v_lo), (k_up, v_up) = load(lower), load(lower + 1)
        t, s_lo = _positions(q0, lower, tq, tk)
        earlier_lo = s_lo < t
        earlier_up = s_lo + tk < t
        state = []
        for h, qh in enumerate(q_heads):
            sl = group_of(h)
            cost_up, hit_up = miss_cost(_dot_t(qh, k_up[:, sl]))
            cost_lo, hit_lo = miss_cost(_dot_t(qh, k_lo[:, sl]))
            cost_up = jnp.where(earlier_up, cost_up, 0.0)
            cost_lo = jnp.where(earlier_lo, cost_lo, 0.0)
            after_up, after_lo = cost_after(cost_up), cost_after(cost_lo)
            run_up = block_total(cost_up, after_up)
            w_up = jnp.where(earlier_up, jnp.exp2(hit_up - after_up), 0.0)
            w_lo = jnp.where(earlier_lo, jnp.exp2(hit_lo - after_lo - run_up), 0.0)
            acc = _dot(w_up.astype(BF16), v_up[:, sl]) + _dot(w_lo.astype(BF16), v_lo[:, sl])
            state.append((run_up + block_total(cost_lo, after_lo), acc))
        return state

    def least_run(st):
        least = st[0][0]
        for run, _ in st[1:]:
            least = jnp.minimum(least, run)
        return jnp.min(least)

    def more(c):
        kb, least, _ = c
        return (kb >= 0) & (least < SB_UNDERFLOW)

    def block(c):
        kb, _, st = c
        ks, vs = load(kb)
        new = []
        for h, (qh, (run, acc)) in enumerate(zip(q_heads, st)):
            cost, hit = miss_cost(_dot_t(qh, ks[:, group_of(h)]))
            after = cost_after(cost)
            acc = acc + _dot(jnp.exp2(hit - after - run).astype(BF16), vs[:, group_of(h)])
            new.append((run + block_total(cost, after), acc))
        return kb - 1, least_run(new), tuple(new)

    def finish(state):
        _, _, state = lax.while_loop(more, block, (lower - 1, least_run(state), tuple(state)))
        return [jnp.where(lane < SB_HD, state[2 * g][1], state[2 * g + 1][1]) for g in range(groups)]

    return first_step, finish


def _sb_kernel(q_ref, k_ref, v_ref, o_ref, *, tq, tk, q_start):
    first_step, finish = _sb_phases(q_ref, k_ref, v_ref, q_start + pl.program_id(2) * tq, tq=tq, tk=tk)
    for g, out in enumerate(finish(first_step())):
        o_ref[0, :, _lane_slab(g)] = out.astype(o_ref.dtype)


def _softmax_bucketed(chains, k_ref, v_ref, q0, emit, *, tq, width, q_range, n_keys):
    bucket = q0 // width
    for j in range(q_range[0] // width, (q_range[1] - 1) // width + 1):
        @pl.when(bucket == j)
        def _(j=j):
            length = (j + 1) * width
            t_chunk = (q0 + lax.broadcasted_iota(jnp.int32, (tq, 1), 0)) // CHUNK
            s_pos = length - width + lax.broadcasted_iota(jnp.int32, (1, width), 1)
            s_chunk = jnp.where(s_pos < n_keys, s_pos // CHUNK, jnp.iinfo(jnp.int32).max)
            vis = s_chunk <= t_chunk
            outs = []
            for qc, k_sl, v_sl in chains:
                s = _dot_t(qc, k_ref[0, 0:length, k_sl])
                tail = jnp.where(vis, s[:, length - width:], -jnp.inf)
                s = tail if length == width else jnp.concatenate([s[:, :length - width], tail], axis=1)
                p = jnp.exp2(s - jnp.max(s, axis=-1, keepdims=True))
                l = jnp.sum(p, axis=-1, keepdims=True)
                outs.append(_dot(p.astype(BF16), v_ref[0, 0:length, v_sl]) / l)
            emit(outs)


def _lane_slab(i):
    return slice(i * LANES, (i + 1) * LANES)


def _df_kernel(lam_ref, sub_ref, q_ref, k_ref, v_ref, o_ref, *, tq, width, q_range, n_keys, lam_init):
    q0 = q_range[0] + pl.program_id(2) * tq
    q = q_ref[0]
    heads = q.shape[-1] // LANES
    lane = lax.broadcasted_iota(jnp.int32, (tq, LANES), 1)
    chains = []
    for h in range(heads):
        qh = q[:, _lane_slab(h)]
        chains.append((jnp.where(lane < DF_HD, qh, jnp.zeros_like(qh)), _lane_slab(h), _lane_slab(h)))
        chains.append((jnp.where(lane >= DF_HD, qh, jnp.zeros_like(qh)), _lane_slab(h), _lane_slab(h)))

    def emit(outs):
        lv = lam_ref[...]
        lam = (jnp.exp(jnp.sum(lv[0:1] * lv[1:2], axis=-1, keepdims=True))
               - jnp.exp(jnp.sum(lv[2:3] * lv[3:4], axis=-1, keepdims=True)) + lam_init)
        for h in range(heads):
            out = outs[2 * h] - lam * outs[2 * h + 1]
            o_ref[0, :, _lane_slab(h)] = (_rms(out, sub_ref[...]) * (1.0 - lam_init)).astype(o_ref.dtype)

    _softmax_bucketed(chains, k_ref, v_ref, q0, emit, tq=tq, width=width, q_range=q_range, n_keys=n_keys)


def _mla_kernel(q_ref, k_ref, v_ref, o_ref, *, tq, width, q_range, n_keys):
    q0 = q_range[0] + pl.program_id(2) * tq
    q = q_ref[0]
    heads = q.shape[-1] // LANES
    lane = lax.broadcasted_iota(jnp.int32, (tq, LANES), 1)
    chains = [(q[:, _lane_slab(h)], _lane_slab(h), _lane_slab(h // 2)) for h in range(heads)]

    def emit(outs):
        for g in range(heads // 2):
            o_ref[0, :, _lane_slab(g)] = jnp.where(
                lane < MLA_VD, outs[2 * g], outs[2 * g + 1]).astype(o_ref.dtype)

    _softmax_bucketed(chains, k_ref, v_ref, q0, emit, tq=tq, width=width, q_range=q_range, n_keys=n_keys)


def _mla_cached_kernel(q_ref, latc_ref, krc_ref, latn_ref, krn_ref, wuk_ref, place_ref, wuv_ref, o_ref,
                       *, q_start):
    q = q_ref[0]
    t = q.shape[0]
    slabs = [slice(h * LANES, (h + 1) * LANES) for h in range(MLA_HEADS)]
    q_all = jnp.concatenate([q[:, sl] for sl in slabs], axis=0)
    q_lat = jnp.concatenate([_dot_t(q[:, sl], wuk_ref[:, sl]) for sl in slabs], axis=0).astype(BF16)
    rows = MLA_HEADS * t
    pad = LANES - t

    def scores(lat, kr):
        kr_lanes = _dot(kr.astype(BF16), place_ref[...]).astype(BF16)
        return _dot_t(q_lat, lat) + _dot_t(q_all, kr_lanes)

    lat_c = latc_ref[0].astype(BF16)
    s_c = scores(lat_c, krc_ref[0])
    lat_n = jnp.concatenate([latn_ref[0], jnp.zeros((pad, KV_LORA), F32)], axis=0).astype(BF16)
    kr_n = jnp.concatenate([krn_ref[0], jnp.zeros((pad, MLA_ROPE), F32)], axis=0)
    t_pos = q_start + lax.broadcasted_iota(jnp.int32, (rows, LANES), 0) % t
    j = lax.broadcasted_iota(jnp.int32, (rows, LANES), 1)
    vis = (j < t) & (((q_start + j) // CHUNK) <= (t_pos // CHUNK))
    s_n = jnp.where(vis, scores(lat_n, kr_n), -jnp.inf)

    m = jnp.maximum(jnp.max(s_c, axis=-1, keepdims=True), jnp.max(s_n, axis=-1, keepdims=True))
    p_c, p_n = jnp.exp2(s_c - m), jnp.exp2(s_n - m)
    l = jnp.sum(p_c, axis=-1, keepdims=True) + jnp.sum(p_n, axis=-1, keepdims=True)
    o_lat = (_dot(p_c.astype(BF16), lat_c) + _dot(p_n.astype(BF16), lat_n)) / l
    full = _dot(o_lat.astype(BF16), wuv_ref[...])
    lane_head = lax.broadcasted_iota(jnp.int32, (t, MLA_VW), 1) // MLA_VD
    out = jnp.zeros((t, MLA_VW), F32)
    for h in range(MLA_HEADS):
        out = out + jnp.where(lane_head == h, full[h * t:(h + 1) * t, :], 0.0)
    o_ref[0] = out.astype(o_ref.dtype)


def _mla_cached_attention(q, lat_cache, kr_cache, lat_new, kr_new, w_uk, place, w_uv, q_start):
    b, t, _ = q.shape
    past = lat_cache.shape[1]
    assert t <= LANES and q_start == past
    per_b = lambda bi: (bi, 0, 0)
    const = lambda bi: (0, 0)
    return pl.pallas_call(
        functools.partial(_mla_cached_kernel, q_start=q_start),
        grid=(b,),
        in_specs=[
            pl.BlockSpec((1, t, MLA_QW), per_b),
            pl.BlockSpec((1, past, KV_LORA), per_b),
            pl.BlockSpec((1, past, MLA_ROPE), per_b),
            pl.BlockSpec((1, t, KV_LORA), per_b),
            pl.BlockSpec((1, t, MLA_ROPE), per_b),
            pl.BlockSpec(w_uk.shape, const),
            pl.BlockSpec(place.shape, const),
            pl.BlockSpec(w_uv.shape, const),
        ],
        out_specs=pl.BlockSpec((1, t, MLA_VW), per_b),
        out_shape=jax.ShapeDtypeStruct((b, t, MLA_VW), BF16),
        compiler_params=_cparams(("parallel",)),
        name="mla_cached_attn",
    )(q, lat_cache, kr_cache, lat_new, kr_new, w_uk, place, w_uv)


def _attention(kernel_fn, q, k, v, extra, *, q_lanes, k_lanes, tq, tk, name, v_groups=1):
    b, t_q, _ = q.shape
    t_k = k.shape[1]
    groups = v.shape[-1] // (LANES * v_groups)
    q_lanes, k_lanes, v_lanes = q_lanes * v_groups, k_lanes * v_groups, LANES * v_groups
    assert t_q % tq == 0 and t_k % tk == 0 and tk % tq == 0 and tk % CHUNK == 0
    const = lambda bi, g, qi: (0, 0)
    return pl.pallas_call(
        kernel_fn,
        grid=(b, groups, t_q // tq),
        in_specs=[pl.BlockSpec(e.shape, const) for e in extra] + [
            pl.BlockSpec((1, tq, q_lanes), lambda bi, g, qi: (bi, qi, g)),
            pl.BlockSpec((1, t_k, k_lanes), lambda bi, g, qi: (bi, 0, g)),
            pl.BlockSpec((1, t_k, v_lanes), lambda bi, g, qi: (bi, 0, g)),
        ],
        out_specs=pl.BlockSpec((1, tq, v_lanes), lambda bi, g, qi: (bi, qi, g)),
        out_shape=jax.ShapeDtypeStruct((b, t_q, groups * v_lanes), BF16),
        compiler_params=_cparams(("parallel", "parallel", "arbitrary")),
        name=name,
    )(*extra, q, k, v)


def _rope_tables(pos, rot, theta, period, offset):
    half = rot // 2
    inv = jnp.float32(theta) ** (-(jnp.arange(half, dtype=F32) * (2.0 / rot)))
    ang = pos.astype(F32)[:, None] * inv[None, :]
    cos, sin = jnp.cos(ang), jnp.sin(ang)
    lane = jnp.arange(LANES) % period - offset
    first = (lane >= 0) & (lane < half)
    second = (lane >= half) & (lane < rot)
    idx = jnp.clip(jnp.where(second, lane - half, lane), 0, half - 1)
    cos_l, sin_l = cos[:, idx], sin[:, idx]
    cos_t = jnp.where(first | second, cos_l, 1.0)
    sin_up = jnp.where(first, -sin_l, 0.0)
    sin_dn = jnp.where(second, sin_l, 0.0)
    return cos_t, sin_up, sin_dn


def _pad_heads(w, heads, width):
    k = w.shape[0]
    w = w.reshape(k, heads, width)
    return jnp.pad(w, ((0, 0), (0, 0), (0, LANES - width))).reshape(k, heads * LANES)


def _with_rotation_partner(w):
    k = w.shape[0]
    half = MLA_ROPE // 2
    w3 = w.reshape(k, MLA_HEADS, LANES)
    x1 = w3[..., MLA_NOPE:MLA_NOPE + half]
    x2 = w3[..., MLA_NOPE + half:MLA_NOPE + MLA_ROPE]
    rot = jnp.concatenate([jnp.zeros_like(w3[..., :MLA_NOPE]), -x2, x1,
                           jnp.zeros_like(w3[..., MLA_QD:])], axis=-1)
    return jnp.concatenate([w, rot.reshape(k, MLA_HEADS * LANES)], axis=-1)


def _with_cache(cache, new, t_pad, dtype):
    b, past, w = cache.shape
    pad = jnp.zeros((b, t_pad - past - new.shape[1], w), dtype)
    return jnp.concatenate([cache.astype(dtype), new.astype(dtype), pad], axis=1)


def _trunk(x, p, start, caches, W, *, tm, tf, tq, width, sb_tq, sb_tk, sb_groups, df_groups, mla_groups):
    b, t, d = x.shape
    m = b * t
    depth = p.shape[0]
    x = x.reshape(m, d)
    pos = start + jnp.arange(t)
    if t % tm == 0:
        tab_pos = pos
    else:
        tab_pos = jnp.tile(pos, m // t)
    df_tabs = _rope_tables(tab_pos, DF_ROT, ROPE_THETA, DF_HD, 0)
    q_cos, q_sin_up, q_sin_dn = _rope_tables(tab_pos, MLA_ROPE, MLA_THETA, LANES, MLA_NOPE)
    mq_tabs = (q_cos, q_sin_dn - q_sin_up)
    mk_tabs = _rope_tables(tab_pos, MLA_ROPE, MLA_THETA, LANES, 0)
    n_keys = start + t
    t_keys = -(-n_keys // width) * width
    assert t_keys % sb_tk == 0 and t_keys >= 2 * sb_tk and sb_tq <= sb_tk
    q_range = (start, start + t)

    ab_rows, mla_rows = [], []
    for i in range(depth):
        g = W['norms'][i][:, None, :]
        j = i // 2
        x = _ffn_block(x, g[0], g[1], W['ffn1_gu'], W['ffn1_dn'], i, tm, tf)
        if i % 2 == 0:
            (sbq, sbk, sbv, dfq, dfk, dfv, sbk_f, sbv_f, dfk_f, dfv_f) = _ab_in_proj(
                x, g[2], W['ab_in'][j], df_tabs, tm)
            ab_rows.append((sbk_f.reshape(b, t, SB_HEADS, SB_HD), sbv_f.reshape(b, t, SB_HEADS, SB_HD),
                            dfk_f.reshape(b, t, DF_HEADS, 2, DF_HD), dfv_f.reshape(b, t, DF_HEADS, 2 * DF_HD)))
            three = lambda a: a.reshape(b, t, a.shape[-1])
            if caches is None:
                keys = [three(a) for a in (sbk, sbv, dfk, dfv)]
            else:
                keys = [_with_cache(c[j].reshape(b, start, -1), n.reshape(b, t, -1), t_keys, BF16)
                        for c, n in zip(caches[:4], (sbk_f, sbv_f, dfk_f, dfv_f))]
            lam_init = 0.8 - 0.6 * math.exp(-0.3 * i)
            df_extra = [W['df_lambda'][j], W['df_subln'][j][None, :]]
            sb_out = _attention(
                functools.partial(_sb_kernel, tq=sb_tq, tk=sb_tk, q_start=start),
                three(sbq), keys[0], keys[1], [], q_lanes=LANES, k_lanes=LANES, tq=sb_tq, tk=sb_tk,
                name="sb_attn", v_groups=sb_groups)
            df_out = _attention(
                functools.partial(_df_kernel, tq=tq, width=width, q_range=q_range, n_keys=n_keys,
                                  lam_init=lam_init),
                three(dfq), keys[2], keys[3], df_extra,
                q_lanes=LANES, k_lanes=LANES, tq=tq, tk=width, name="df_attn", v_groups=df_groups)
            parts = [sb_out.reshape(m, SB_W), df_out.reshape(m, DF_W)]
            w_out = W['ab_out'][j]
        else:
            q, lat, kr = _mla_in_proj(x, g[2], W['mla_in'][j], W['mla_q_norm'][j][None, :],
                                      W['mla_kv_norm'][j][None, :], W['mla_uq'][j], mq_tabs, mk_tabs, tm)
            mla_rows.append((lat.reshape(b, t, KV_LORA), kr.reshape(b, t, MLA_ROPE)))
            if caches is None:
                k_all, v_all = _mla_kv_expand(lat, kr, W['mla_uk'][j], W['mla_place'], W['mla_uv'][j], tm)
                out = _attention(
                    functools.partial(_mla_kernel, tq=tq, width=width, q_range=q_range, n_keys=n_keys),
                    q.reshape(b, t, MLA_QW), k_all.reshape(b, -1, MLA_QW), v_all.reshape(b, -1, MLA_VW), [],
                    q_lanes=2 * LANES, k_lanes=2 * LANES, tq=tq, tk=width, name="mla_attn",
                    v_groups=mla_groups)
            else:
                out = _mla_cached_attention(
                    q.reshape(b, t, MLA_QW), caches[4][j], caches[5][j], lat.reshape(b, t, KV_LORA),
                    kr.reshape(b, t, MLA_ROPE), W['mla_uk'][j], W['mla_place'][:, :LANES], W['mla_uv'][j], start)
            parts = [out.reshape(m, MLA_VW)]
            w_out = W['mla_out'][j]
        x = _post_mixer(x, p.reshape(depth, m, -1), i, g[3:8], w_out, W['ffn2_gu'], W['ffn2_dn'],
                        W['ple_gate'], W['ple_in'], parts, tm, tf)
    sb_k, sb_v, df_k, df_v = [jnp.stack(r, axis=0) for r in zip(*ab_rows)]
    lat, kr = [jnp.stack(r, axis=0) for r in zip(*mla_rows)]
    return x.reshape(b, t, d), sb_k, sb_v, df_k, df_v, lat, kr


def _pick(n, candidates):
    for c in candidates:
        if n % c == 0:
            return c
    return n


def _cast_kernel(x_ref, o_ref):
    o_ref[...] = x_ref[...].astype(o_ref.dtype)


def _to_bf16(w):
    n_l, k, n = w.shape
    bk = _pick(k, (256, 128, 64, 32, 16))
    spec = pl.BlockSpec((1, bk, n), lambda l, i: (l, i, 0))
    return pl.pallas_call(
        _cast_kernel, grid=(n_l, k // bk), in_specs=[spec], out_specs=spec,
        out_shape=jax.ShapeDtypeStruct(w.shape, BF16),
        compiler_params=_cparams(("parallel", "parallel")), name="to_bf16",
    )(w)


def kernel(x_prompt, x_sample, p_prompt, p_sample, cache_sb_k, cache_sb_v, cache_df_k, cache_df_v,
           cache_mla_latent, cache_mla_krope, norms, ffn1_gu, ffn1_dn, ffn2_gu, ffn2_dn, ple_in, ple_gate,
           ab_in, ab_out, df_lambda, df_subln, mla_in, mla_q_norm, mla_kv_norm, mla_uq, mla_uk, mla_uv, mla_out):
    bf = lambda a: a.astype(BF16)
    n_odd = mla_in.shape[0]
    place = jnp.zeros((MLA_ROPE, MLA_HEADS, LANES), F32)
    place = place.at[jnp.arange(MLA_ROPE), :, MLA_NOPE + jnp.arange(MLA_ROPE)].set(1.0)
    W = dict(
        norms=norms, ffn1_gu=_to_bf16(ffn1_gu), ffn1_dn=_to_bf16(ffn1_dn), ffn2_gu=_to_bf16(ffn2_gu),
        ffn2_dn=_to_bf16(ffn2_dn), ple_in=bf(ple_in), ple_gate=_to_bf16(ple_gate), ab_in=_to_bf16(ab_in),
        ab_out=_to_bf16(ab_out),
        df_lambda=df_lambda, df_subln=df_subln,
        mla_in=bf(jnp.pad(mla_in, ((0, 0), (0, 0), (0, MLA_IN_PAD - mla_in.shape[-1])))),
        mla_q_norm=mla_q_norm, mla_kv_norm=mla_kv_norm,
        mla_uq=bf(jnp.stack([_with_rotation_partner(_pad_heads(mla_uq[j], MLA_HEADS, MLA_QD))
                             for j in range(n_odd)])),
        mla_uk=bf(jnp.stack([_pad_heads(mla_uk[j], MLA_HEADS, MLA_NOPE) for j in range(n_odd)])),
        mla_uv=bf(mla_uv), mla_out=bf(mla_out),
        mla_place=bf(place.reshape(MLA_ROPE, MLA_QW)),
    )
    d_ff = ffn1_dn.shape[1]
    tf = d_ff // 2 if d_ff % (2 * LANES) == 0 else d_ff

    t_p = x_prompt.shape[1]
    m_p = x_prompt.shape[0] * t_p
    width_p = _pick(t_p, (BUCKET, Q_TILE, 128, CHUNK))
    tq_p = _pick(width_p, (Q_TILE, 128, CHUNK))
    out_p = _trunk(x_prompt, p_prompt, 0, None, W, tm=_pick(m_p, (ROW_TILE, 256, 128)), tf=tf, tq=tq_p,
                   width=width_p, sb_tq=tq_p, sb_tk=tq_p, sb_groups=2, df_groups=2, mla_groups=2)

    caches = (cache_sb_k, cache_sb_v, cache_df_k, cache_df_v, cache_mla_latent, cache_mla_krope)
    past = cache_sb_k.shape[2]
    t_s = x_sample.shape[1]
    m_s = x_sample.shape[0] * t_s
    tk_s = _pick(past, (Q_TILE, 128, CHUNK))
    out_s = _trunk(x_sample, p_sample, past, caches, W, tm=_pick(m_s, (128,)), tf=tf, tq=t_s, width=tk_s,
                   sb_tq=t_s, sb_tk=tk_s, sb_groups=SB_HEADS // 2, df_groups=DF_HEADS, mla_groups=MLA_HEADS // 2)

    (y_p, sb_k_p, sb_v_p, df_k_p, df_v_p, lat_p, kr_p) = out_p
    (y_s, sb_k_s, sb_v_s, df_k_s, df_v_s, lat_s, kr_s) = out_s
    return (y_p, y_s, sb_k_p, sb_v_p, df_k_p, df_v_p, lat_p, kr_p,
            sb_k_s, sb_v_s, df_k_s, df_v_s, lat_s, kr_s)
```

```python
import functools
import math

import jax
import jax.numpy as jnp
from jax import lax
from jax.experimental import pallas as pl
from jax.experimental.pallas import tpu as pltpu

F32 = jnp.float32
BF16 = jnp.bfloat16

EPS = 1e-6
CHUNK = 64
SB_HEADS, SB_HD = 8, 64
DF_HEADS, DF_HD = 4, 64
DF_ROT = DF_HD // 4
ROPE_THETA = 500000.0
MLA_HEADS, MLA_NOPE, MLA_ROPE, MLA_VD = 16, 64, 32, 64
Q_LORA, KV_LORA = 384, 256
MLA_THETA = 10000.0
MLA_QD = MLA_NOPE + MLA_ROPE
LOG2E = math.log2(math.e)
SB_UNDERFLOW = 151.0
SB_LINEAR = 64.0
BUCKET = 512
ROW_TILE = 512
Q_TILE = 256

LANES = 128
SB_W = SB_HEADS * SB_HD
DF_W = DF_HEADS * 2 * DF_HD
MLA_QW = MLA_HEADS * LANES
MLA_VW = MLA_HEADS * MLA_VD
MLA_IN_PAD = -(-(Q_LORA + KV_LORA + MLA_ROPE) // LANES) * LANES

VMEM_LIMIT = 56 * 1024 * 1024


def _cparams(sem):
    return pltpu.CompilerParams(dimension_semantics=sem, vmem_limit_bytes=VMEM_LIMIT)


def _rms(x, g):
    return x * lax.rsqrt(jnp.mean(x * x, axis=-1, keepdims=True) + EPS) * g


def _dot(a, b):
    return jnp.dot(a, b, preferred_element_type=F32)


def _dot_t(a, b):
    return lax.dot_general(a, b, (((1,), (1,)), ((), ())), preferred_element_type=F32)


def _rope_slab(x, cos, sin_up, sin_dn, shift):
    n = x.shape[-1]
    return x * cos + pltpu.roll(x, n - shift, 1) * sin_up + pltpu.roll(x, shift, 1) * sin_dn


def _swiglu(h, wgu_ref, wd_ref, chunk):
    d_ff = wd_ref.shape[0]
    acc = None
    for c in range(d_ff // chunk):
        g = _dot(h, wgu_ref[:, c * chunk:(c + 1) * chunk])
        u = _dot(h, wgu_ref[:, d_ff + c * chunk:d_ff + (c + 1) * chunk])
        t = _dot((g * jax.nn.sigmoid(g) * u).astype(BF16), wd_ref[c * chunk:(c + 1) * chunk, :])
        acc = t if acc is None else acc + t
    return acc


def _whole(a):
    return pl.BlockSpec(a.shape, lambda *_: (0,) * a.ndim, pipeline_mode=pl.Buffered(1))


def _layer(stack, layer):
    return pl.BlockSpec((None,) + stack.shape[1:], lambda *_: (layer,) + (0,) * (stack.ndim - 1),
                        pipeline_mode=pl.Buffered(1))


def _ffn_kernel(x_ref, gin_ref, gout_ref, wgu_ref, wd_ref, o_ref, *, chunk):
    x = x_ref[...]
    h = _rms(x, gin_ref[...]).astype(BF16)
    o_ref[...] = x + 0.5 * _rms(_swiglu(h, wgu_ref, wd_ref, chunk), gout_ref[...])


def _ffn_block(x, g_in, g_out, w_gu, w_dn, layer, tm, tf):
    m, d = x.shape
    row = lambda i: (i, 0)
    return pl.pallas_call(
        functools.partial(_ffn_kernel, chunk=tf),
        grid=(m // tm,),
        in_specs=[pl.BlockSpec((tm, d), row), _whole(g_in), _whole(g_out), _layer(w_gu, layer),
                  _layer(w_dn, layer)],
        out_specs=pl.BlockSpec((tm, d), row),
        out_shape=jax.ShapeDtypeStruct((m, d), F32),
        compiler_params=_cparams(("parallel",)),
        name="ffn_block",
    )(x, g_in, g_out, w_gu, w_dn)


def _ab_in_kernel(x_ref, g_ref, w_ref, cos_ref, sup_ref, sdn_ref,
                  sbq_ref, sbk_ref, sbv_ref, dfq_ref, dfk_ref, dfv_ref,
                  sbk_f_ref, sbv_f_ref, dfk_f_ref, dfv_f_ref):
    h = _rms(x_ref[...], g_ref[...]).astype(BF16)
    cos, sup, sdn = cos_ref[...], sup_ref[...], sdn_ref[...]

    def proj(c):
        return _dot(h, w_ref[:, c * SB_W:(c + 1) * SB_W])

    def store_rows(out_f, x, first):
        for j in range(x.shape[-1] // SB_HD):
            out_f[pl.ds(first + j, x.shape[0], stride=SB_W // SB_HD), :] = x[:, j * SB_HD:(j + 1) * SB_HD]

    def rope(x, scale, out_b, out_f):
        for s in range(x.shape[-1] // LANES):
            sl = slice(s * LANES, (s + 1) * LANES)
            r = _rope_slab(x[:, sl], cos, sup, sdn, DF_ROT // 2)
            if out_f is not None:
                store_rows(out_f, r, s * (LANES // SB_HD))
            out_b[:, sl] = (r * scale).astype(BF16)

    sbq_ref[...] = (proj(0) * (SB_HD ** -0.5 * LOG2E)).astype(BF16)
    sbk = proj(1)
    store_rows(sbk_f_ref, sbk, 0)
    sbk_ref[...] = sbk.astype(BF16)
    sbv = proj(2)
    store_rows(sbv_f_ref, sbv, 0)
    sbv_ref[...] = sbv.astype(BF16)
    rope(proj(3), DF_HD ** -0.5 * LOG2E, dfq_ref, None)
    rope(proj(4), 1.0, dfk_ref, dfk_f_ref)
    dfv = proj(5)
    for j in range(DF_HEADS):
        dfv_f_ref[pl.ds(j, dfv.shape[0], stride=DF_HEADS), :] = dfv[:, j * LANES:(j + 1) * LANES]
    dfv_ref[...] = dfv.astype(BF16)


def _ab_in_proj(x, g, w_in, tabs, tm):
    m, d = x.shape
    n_tab = tabs[0].shape[0] // tm
    row = lambda i: (i, 0)
    const = lambda i: (0, 0)
    tab = lambda i: (i % n_tab, 0)
    wide = pl.BlockSpec((tm, SB_W), row)
    bf = jax.ShapeDtypeStruct((m, SB_W), BF16)
    n_rows = SB_W // SB_HD
    tall = pl.BlockSpec((tm * n_rows, SB_HD), row)
    f32_tall = jax.ShapeDtypeStruct((m * n_rows, SB_HD), F32)
    return pl.pallas_call(
        _ab_in_kernel,
        grid=(m // tm,),
        in_specs=[
            pl.BlockSpec((tm, d), row),
            pl.BlockSpec((1, d), const),
            pl.BlockSpec(w_in.shape, const),
            pl.BlockSpec((tm, LANES), tab),
            pl.BlockSpec((tm, LANES), tab),
            pl.BlockSpec((tm, LANES), tab),
        ],
        out_specs=[wide] * 6 + [tall] * 3 + [pl.BlockSpec((tm * DF_HEADS, LANES), row)],
        out_shape=[bf] * 6 + [f32_tall] * 3 + [jax.ShapeDtypeStruct((m * DF_HEADS, LANES), F32)],
        compiler_params=_cparams(("parallel",)),
        name="ab_in_proj",
    )(x, g, w_in, *tabs)


def _post_mixer_kernel(*refs, n_parts, chunk):
    (x_ref, p_ref, g_ref, wo_ref, wgu_ref, wd_ref, wgate_ref, wple_ref) = refs[:8]
    parts = refs[8:8 + n_parts]
    o_ref = refs[8 + n_parts]
    g_mix, g_in, g_out, g_gate, g_ple = (g_ref[i] for i in range(5))
    off = 0
    mix = None
    for p in parts:
        k = p.shape[-1]
        t = _dot(p[...], wo_ref[off:off + k, :])
        mix = t if mix is None else mix + t
        off += k
    x1 = x_ref[...] + _rms(mix, g_mix)
    x2 = x1 + 0.5 * _rms(_swiglu(_rms(x1, g_in).astype(BF16), wgu_ref, wd_ref, chunk), g_out)
    gate = jax.nn.sigmoid(_dot(_rms(x2, g_gate).astype(BF16), wgate_ref[...]))
    emb = _dot(p_ref[...].astype(BF16), wple_ref[...])
    o_ref[...] = x2 + _rms(gate * emb, g_ple)


def _post_mixer(x, p, layer, gains, w_out, w_gu, w_dn, w_gate, w_ple, parts, tm, tf):
    m, d = x.shape
    row = lambda i: (i, 0)
    return pl.pallas_call(
        functools.partial(_post_mixer_kernel, n_parts=len(parts), chunk=tf),
        grid=(m // tm,),
        in_specs=[
            pl.BlockSpec((tm, d), row),
            pl.BlockSpec((None, tm, p.shape[-1]), lambda i: (layer, i, 0)),
        ] + [_whole(gains), _whole(w_out)] + [_layer(a, layer) for a in (w_gu, w_dn, w_gate, w_ple)]
        + [pl.BlockSpec((tm, a.shape[-1]), row) for a in parts],
        out_specs=pl.BlockSpec((tm, d), row),
        out_shape=jax.ShapeDtypeStruct((m, d), F32),
        compiler_params=_cparams(("parallel",)),
        name="post_mixer",
    )(x, p, gains, w_out, w_gu, w_dn, w_gate, w_ple, *parts)


def _mla_in_kernel(x_ref, g_ref, w_ref, qn_ref, kvn_ref, wuq_ref,
                   qcos_ref, qsin_ref, kcos_ref, ksup_ref, ksdn_ref,
                   q_ref, lat_ref, kr_ref):
    h = _rms(x_ref[...], g_ref[...]).astype(BF16)
    proj = _dot(h, w_ref[...])
    c_q = _rms(proj[:, :Q_LORA], qn_ref[...]).astype(BF16)
    lat_ref[...] = _rms(proj[:, Q_LORA:Q_LORA + KV_LORA], kvn_ref[...])
    kr = _rope_slab(proj[:, Q_LORA + KV_LORA:], kcos_ref[...], ksup_ref[...], ksdn_ref[...],
                    MLA_ROPE // 2)
    kr_ref[...] = kr[:, :MLA_ROPE]
    q2 = _dot(c_q, wuq_ref[...])
    cos, sin = qcos_ref[...], qsin_ref[...]
    for hd in range(MLA_HEADS):
        sl = slice(hd * LANES, (hd + 1) * LANES)
        rot = slice(MLA_QW + hd * LANES, MLA_QW + (hd + 1) * LANES)
        r = q2[:, sl] * cos + q2[:, rot] * sin
        q_ref[:, sl] = (r * (MLA_QD ** -0.5 * LOG2E)).astype(BF16)


def _mla_in_proj(x, g, w_in, q_norm, kv_norm, w_uq, qtabs, ktabs, tm):
    m, d = x.shape
    n_tab = qtabs[0].shape[0] // tm
    row = lambda i: (i, 0)
    const = lambda i: (0, 0)
    tab = lambda i: (i % n_tab, 0)
    tspec = pl.BlockSpec((tm, LANES), tab)
    return pl.pallas_call(
        _mla_in_kernel,
        grid=(m // tm,),
        in_specs=[
            pl.BlockSpec((tm, d), row),
            pl.BlockSpec((1, d), const),
            pl.BlockSpec(w_in.shape, const),
            pl.BlockSpec((1, Q_LORA), const),
            pl.BlockSpec((1, KV_LORA), const),
            pl.BlockSpec(w_uq.shape, const),
        ] + [tspec] * 5,
        out_specs=[
            pl.BlockSpec((tm, MLA_QW), row),
            pl.BlockSpec((tm, KV_LORA), row),
            pl.BlockSpec((tm, MLA_ROPE), row),
        ],
        out_shape=[
            jax.ShapeDtypeStruct((m, MLA_QW), BF16),
            jax.ShapeDtypeStruct((m, KV_LORA), F32),
            jax.ShapeDtypeStruct((m, MLA_ROPE), F32),
        ],
        compiler_params=_cparams(("parallel",)),
        name="mla_in_proj",
    )(x, g, w_in, q_norm, kv_norm, w_uq, *qtabs, *ktabs)


def _mla_kv_kernel(lat_ref, kr_ref, wuk_ref, place_ref, wuv_ref, k_ref, v_ref):
    lat = lat_ref[...].astype(BF16)
    k_ref[...] = (_dot(lat, wuk_ref[...]) + _dot(kr_ref[...].astype(BF16), place_ref[...])).astype(BF16)
    v_ref[...] = _dot(lat, wuv_ref[...]).astype(BF16)


def _mla_kv_expand(lat, kr, w_uk, place, w_uv, tm):
    m = lat.shape[0]
    row = lambda i: (i, 0)
    const = lambda i: (0, 0)
    return pl.pallas_call(
        _mla_kv_kernel,
        grid=(m // tm,),
        in_specs=[
            pl.BlockSpec((tm, KV_LORA), row),
            pl.BlockSpec((tm, MLA_ROPE), row),
            pl.BlockSpec(w_uk.shape, const),
            pl.BlockSpec(place.shape, const),
            pl.BlockSpec(w_uv.shape, const),
        ],
        out_specs=[pl.BlockSpec((tm, MLA_QW), row), pl.BlockSpec((tm, MLA_VW), row)],
        out_shape=[jax.ShapeDtypeStruct((m, MLA_QW), BF16), jax.ShapeDtypeStruct((m, MLA_VW), BF16)],
        compiler_params=_cparams(("parallel",)),
        name="mla_kv_expand",
    )(lat, kr, w_uk, place, w_uv)


def _positions(q0, kb, tq, tk):
    t = q0 + lax.broadcasted_iota(jnp.int32, (tq, tk), 0)
    s = kb * tk + lax.broadcasted_iota(jnp.int32, (tq, tk), 1)
    return t, s


def _sb_phases(q_ref, k_ref, v_ref, q0, *, tq, tk):
    n_full = q0 // tk
    q = q_ref[0]
    groups = q.shape[-1] // LANES
    lane = lax.broadcasted_iota(jnp.int32, (tq, LANES), 1)
    q_heads = []
    for g in range(groups):
        qg = q[:, g * LANES:(g + 1) * LANES]
        q_heads += [jnp.where(lane < SB_HD, qg, jnp.zeros_like(qg)),
                    jnp.where(lane >= SB_HD, qg, jnp.zeros_like(qg))]
    group_of = lambda h: slice((h // 2) * LANES, (h // 2 + 1) * LANES)
    later = (lax.broadcasted_iota(jnp.int32, (tk, tk), 0)
             > lax.broadcasted_iota(jnp.int32, (tk, tk), 1)).astype(BF16)

    def load(kb):
        start = pl.multiple_of(kb * tk, tk)
        return k_ref[0, pl.ds(start, tk), :], v_ref[0, pl.ds(start, tk), :]

    def miss_cost(z):
        cost = jnp.maximum(jnp.log2(1.0 + jnp.exp2(jnp.minimum(z, SB_LINEAR))), z)
        return cost, z - cost

    def cost_after(cost):
        hi = cost.astype(BF16)
        lo = (cost - hi.astype(F32)).astype(BF16)
        return _dot(hi, later) + _dot(lo, later)

    def block_total(cost, after):
        return after[:, 0:1] + cost[:, 0:1]

    lower = jnp.maximum(n_full - 1, 0)

    def first_step():
        (k_lo, v_lo), (k_up, v_up) = load(lower), load(lower + 1)
        t, s_lo = _positions(q0, lower, tq, tk)
        earlier_lo = s_lo < t
        earlier_up = s_lo + tk < t
        state = []
        for h, qh in enumerate(q_heads):
            sl = group_of(h)
            cost_up, hit_up = miss_cost(_dot_t(qh, k_up[:, sl]))
            cost_lo, hit_lo = miss_cost(_dot_t(qh, k_lo[:, sl]))
            cost_up = jnp.where(earlier_up, cost_up, 0.0)
            cost_lo = jnp.where(earlier_lo, cost_lo, 0.0)
            after_up, after_lo = cost_after(cost_up), cost_after(cost_lo)
            run_up = block_total(cost_up, after_up)
            w_up = jnp.where(earlier_up, jnp.exp2(hit_up - after_up), 0.0)
            w_lo = jnp.where(earlier_lo, jnp.exp2(hit_lo - after_lo - run_up), 0.0)
            acc = _dot(w_up.astype(BF16), v_up[:, sl]) + _dot(w_lo.astype(BF16), v_lo[:, sl])
            state.append((run_up + block_total(cost_lo, after_lo), acc))
        return state

    def least_run(st):
        least = st[0][0]
        for run, _ in st[1:]:
            least = jnp.minimum(least, run)
        return jnp.min(least)

    def more(c):
        kb, least, _ = c
        return (kb >= 0) & (least < SB_UNDERFLOW)

    def block(c):
        kb, _, st = c
        ks, vs = load(kb)
        new = []
        for h, (qh, (run, acc)) in enumerate(zip(q_heads, st)):
            cost, hit = miss_cost(_dot_t(qh, ks[:, group_of(h)]))
            after = cost_after(cost)
            acc = acc + _dot(jnp.exp2(hit - after - run).astype(BF16), vs[:, group_of(h)])
            new.append((run + block_total(cost, after), acc))
        return kb - 1, least_run(new), tuple(new)

    def finish(state):
        _, _, state = lax.while_loop(more, block, (lower - 1, least_run(state), tuple(state)))
        return [jnp.where(lane < SB_HD, state[2 * g][1], state[2 * g + 1][1]) for g in range(groups)]

    return first_step, finish


def _sb_kernel(q_ref, k_ref, v_ref, o_ref, *, tq, tk, q_start):
    first_step, finish = _sb_phases(q_ref, k_ref, v_ref, q_start + pl.program_id(2) * tq, tq=tq, tk=tk)
    for g, out in enumerate(finish(first_step())):
        o_ref[0, :, _lane_slab(g)] = out.astype(o_ref.dtype)


def _softmax_bucketed(chains, k_ref, v_ref, q0, emit, *, tq, width, q_range, n_keys):
    bucket = q0 // width
    for j in range(q_range[0] // width, (q_range[1] - 1) // width + 1):
        @pl.when(bucket == j)
        def _(j=j):
            length = (j + 1) * width
            t_chunk = (q0 + lax.broadcasted_iota(jnp.int32, (tq, 1), 0)) // CHUNK
            s_pos = length - width + lax.broadcasted_iota(jnp.int32, (1, width), 1)
            s_chunk = jnp.where(s_pos < n_keys, s_pos // CHUNK, jnp.iinfo(jnp.int32).max)
            vis = s_chunk <= t_chunk
            outs = []
            for qc, k_sl, v_sl in chains:
                s = _dot_t(qc, k_ref[0, 0:length, k_sl])
                tail = jnp.where(vis, s[:, length - width:], -jnp.inf)
                s = tail if length == width else jnp.concatenate([s[:, :length - width], tail], axis=1)
                p = jnp.exp2(s - jnp.max(s, axis=-1, keepdims=True))
                l = jnp.sum(p, axis=-1, keepdims=True)
                outs.append(_dot(p.astype(BF16), v_ref[0, 0:length, v_sl]) / l)
            emit(outs)


def _lane_slab(i):
    return slice(i * LANES, (i + 1) * LANES)


def _df_kernel(lam_ref, sub_ref, q_ref, k_ref, v_ref, o_ref, *, tq, width, q_range, n_keys, lam_init):
    q0 = q_range[0] + pl.program_id(2) * tq
    q = q_ref[0]
    heads = q.shape[-1] // LANES
    lane = lax.broadcasted_iota(jnp.int32, (tq, LANES), 1)
    chains = []
    for h in range(heads):
        qh = q[:, _lane_slab(h)]
        chains.append((jnp.where(lane < DF_HD, qh, jnp.zeros_like(qh)), _lane_slab(h), _lane_slab(h)))
        chains.append((jnp.where(lane >= DF_HD, qh, jnp.zeros_like(qh)), _lane_slab(h), _lane_slab(h)))

    def emit(outs):
        lv = lam_ref[...]
        lam = (jnp.exp(jnp.sum(lv[0:1] * lv[1:2], axis=-1, keepdims=True))
               - jnp.exp(jnp.sum(lv[2:3] * lv[3:4], axis=-1, keepdims=True)) + lam_init)
        for h in range(heads):
            out = outs[2 * h] - lam * outs[2 * h + 1]
            o_ref[0, :, _lane_slab(h)] = (_rms(out, sub_ref[...]) * (1.0 - lam_init)).astype(o_ref.dtype)

    _softmax_bucketed(chains, k_ref, v_ref, q0, emit, tq=tq, width=width, q_range=q_range, n_keys=n_keys)


def _mla_kernel(q_ref, k_ref, v_ref, o_ref, *, tq, width, q_range, n_keys):
    q0 = q_range[0] + pl.program_id(2) * tq
    q = q_ref[0]
    heads = q.shape[-1] // LANES
    lane = lax.broadcasted_iota(jnp.int32, (tq, LANES), 1)
    chains = [(q[:, _lane_slab(h)], _lane_slab(h), _lane_slab(h // 2)) for h in range(heads)]

    def emit(outs):
        for g in range(heads // 2):
            o_ref[0, :, _lane_slab(g)] = jnp.where(
                lane < MLA_VD, outs[2 * g], outs[2 * g + 1]).astype(o_ref.dtype)

    _softmax_bucketed(chains, k_ref, v_ref, q0, emit, tq=tq, width=width, q_range=q_range, n_keys=n_keys)


def _mla_cached_kernel(q_ref, latc_ref, krc_ref, latn_ref, krn_ref, wuk_ref, place_ref, wuv_ref, o_ref,
                       *, q_start):
    q = q_ref[0]
    t = q.shape[0]
    slabs = [slice(h * LANES, (h + 1) * LANES) for h in range(MLA_HEADS)]
    q_all = jnp.concatenate([q[:, sl] for sl in slabs], axis=0)
    q_lat = jnp.concatenate([_dot_t(q[:, sl], wuk_ref[:, sl]) for sl in slabs], axis=0).astype(BF16)
    rows = MLA_HEADS * t
    pad = LANES - t

    def scores(lat, kr):
        kr_lanes = _dot(kr.astype(BF16), place_ref[...]).astype(BF16)
        return _dot_t(q_lat, lat) + _dot_t(q_all, kr_lanes)

    lat_c = latc_ref[0].astype(BF16)
    s_c = scores(lat_c, krc_ref[0])
    lat_n = jnp.concatenate([latn_ref[0], jnp.zeros((pad, KV_LORA), F32)], axis=0).astype(BF16)
    kr_n = jnp.concatenate([krn_ref[0], jnp.zeros((pad, MLA_ROPE), F32)], axis=0)
    t_pos = q_start + lax.broadcasted_iota(jnp.int32, (rows, LANES), 0) % t
    j = lax.broadcasted_iota(jnp.int32, (rows, LANES), 1)
    vis = (j < t) & (((q_start + j) // CHUNK) <= (t_pos // CHUNK))
    s_n = jnp.where(vis, scores(lat_n, kr_n), -jnp.inf)

    m = jnp.maximum(jnp.max(s_c, axis=-1, keepdims=True), jnp.max(s_n, axis=-1, keepdims=True))
    p_c, p_n = jnp.exp2(s_c - m), jnp.exp2(s_n - m)
    l = jnp.sum(p_c, axis=-1, keepdims=True) + jnp.sum(p_n, axis=-1, keepdims=True)
    o_lat = (_dot(p_c.astype(BF16), lat_c) + _dot(p_n.astype(BF16), lat_n)) / l
    full = _dot(o_lat.astype(BF16), wuv_ref[...])
    lane_head = lax.broadcasted_iota(jnp.int32, (t, MLA_VW), 1) // MLA_VD
    out = jnp.zeros((t, MLA_VW), F32)
    for h in range(MLA_HEADS):
        out = out + jnp.where(lane_head == h, full[h * t:(h + 1) * t, :], 0.0)
    o_ref[0] = out.astype(o_ref.dtype)


def _mla_cached_attention(q, lat_cache, kr_cache, lat_new, kr_new, w_uk, place, w_uv, q_start):
    b, t, _ = q.shape
    past = lat_cache.shape[1]
    assert t <= LANES and q_start == past
    per_b = lambda bi: (bi, 0, 0)
    const = lambda bi: (0, 0)
    return pl.pallas_call(
        functools.partial(_mla_cached_kernel, q_start=q_start),
        grid=(b,),
        in_specs=[
            pl.BlockSpec((1, t, MLA_QW), per_b),
            pl.BlockSpec((1, past, KV_LORA), per_b),
            pl.BlockSpec((1, past, MLA_ROPE), per_b),
            pl.BlockSpec((1, t, KV_LORA), per_b),
            pl.BlockSpec((1, t, MLA_ROPE), per_b),
            pl.BlockSpec(w_uk.shape, const),
            pl.BlockSpec(place.shape, const),
            pl.BlockSpec(w_uv.shape, const),
        ],
        out_specs=pl.BlockSpec((1, t, MLA_VW), per_b),
        out_shape=jax.ShapeDtypeStruct((b, t, MLA_VW), BF16),
        compiler_params=_cparams(("parallel",)),
        name="mla_cached_attn",
    )(q, lat_cache, kr_cache, lat_new, kr_new, w_uk, place, w_uv)


def _attention(kernel_fn, q, k, v, extra, *, q_lanes, k_lanes, tq, tk, name, v_groups=1):
    b, t_q, _ = q.shape
    t_k = k.shape[1]
    groups = v.shape[-1] // (LANES * v_groups)
    q_lanes, k_lanes, v_lanes = q_lanes * v_groups, k_lanes * v_groups, LANES * v_groups
    assert t_q % tq == 0 and t_k % tk == 0 and tk % tq == 0 and tk % CHUNK == 0
    const = lambda bi, g, qi: (0, 0)
    return pl.pallas_call(
        kernel_fn,
        grid=(b, groups, t_q // tq),
        in_specs=[pl.BlockSpec(e.shape, const) for e in extra] + [
            pl.BlockSpec((1, tq, q_lanes), lambda bi, g, qi: (bi, qi, g)),
            pl.BlockSpec((1, t_k, k_lanes), lambda bi, g, qi: (bi, 0, g)),
            pl.BlockSpec((1, t_k, v_lanes), lambda bi, g, qi: (bi, 0, g)),
        ],
        out_specs=pl.BlockSpec((1, tq, v_lanes), lambda bi, g, qi: (bi, qi, g)),
        out_shape=jax.ShapeDtypeStruct((b, t_q, groups * v_lanes), BF16),
        compiler_params=_cparams(("parallel", "parallel", "arbitrary")),
        name=name,
    )(*extra, q, k, v)


def _rope_tables(pos, rot, theta, period, offset):
    half = rot // 2
    inv = jnp.float32(theta) ** (-(jnp.arange(half, dtype=F32) * (2.0 / rot)))
    ang = pos.astype(F32)[:, None] * inv[None, :]
    cos, sin = jnp.cos(ang), jnp.sin(ang)
    lane = jnp.arange(LANES) % period - offset
    first = (lane >= 0) & (lane < half)
    second = (lane >= half) & (lane < rot)
    idx = jnp.clip(jnp.where(second, lane - half, lane), 0, half - 1)
    cos_l, sin_l = cos[:, idx], sin[:, idx]
    cos_t = jnp.where(first | second, cos_l, 1.0)
    sin_up = jnp.where(first, -sin_l, 0.0)
    sin_dn = jnp.where(second, sin_l, 0.0)
    return cos_t, sin_up, sin_dn


def _pad_heads(w, heads, width):
    k = w.shape[0]
    w = w.reshape(k, heads, width)
    return jnp.pad(w, ((0, 0), (0, 0), (0, LANES - width))).reshape(k, heads * LANES)


def _with_rotation_partner(w):
    k = w.shape[0]
    half = MLA_ROPE // 2
    w3 = w.reshape(k, MLA_HEADS, LANES)
    x1 = w3[..., MLA_NOPE:MLA_NOPE + half]
    x2 = w3[..., MLA_NOPE + half:MLA_NOPE + MLA_ROPE]
    rot = jnp.concatenate([jnp.zeros_like(w3[..., :MLA_NOPE]), -x2, x1,
                           jnp.zeros_like(w3[..., MLA_QD:])], axis=-1)
    return jnp.concatenate([w, rot.reshape(k, MLA_HEADS * LANES)], axis=-1)


def _with_cache(cache, new, t_pad, dtype):
    b, past, w = cache.shape
    pad = jnp.zeros((b, t_pad - past - new.shape[1], w), dtype)
    return jnp.concatenate([cache.astype(dtype), new.astype(dtype), pad], axis=1)


def _trunk(x, p, start, caches, W, *, tm, tf, tq, width, sb_tq, sb_tk, sb_groups, df_groups, mla_groups):
    b, t, d = x.shape
    m = b * t
    depth = p.shape[0]
    x = x.reshape(m, d)
    pos = start + jnp.arange(t)
    if t % tm == 0:
        tab_pos = pos
    else:
        tab_pos = jnp.tile(pos, m // t)
    df_tabs = _rope_tables(tab_pos, DF_ROT, ROPE_THETA, DF_HD, 0)
    q_cos, q_sin_up, q_sin_dn = _rope_tables(tab_pos, MLA_ROPE, MLA_THETA, LANES, MLA_NOPE)
    mq_tabs = (q_cos, q_sin_dn - q_sin_up)
    mk_tabs = _rope_tables(tab_pos, MLA_ROPE, MLA_THETA, LANES, 0)
    n_keys = start + t
    t_keys = -(-n_keys // width) * width
    assert t_keys % sb_tk == 0 and t_keys >= 2 * sb_tk and sb_tq <= sb_tk
    q_range = (start, start + t)

    ab_rows, mla_rows = [], []
    for i in range(depth):
        g = W['norms'][i][:, None, :]
        j = i // 2
        x = _ffn_block(x, g[0], g[1], W['ffn1_gu'], W['ffn1_dn'], i, tm, tf)
        if i % 2 == 0:
            (sbq, sbk, sbv, dfq, dfk, dfv, sbk_f, sbv_f, dfk_f, dfv_f) = _ab_in_proj(
                x, g[2], W['ab_in'][j], df_tabs, tm)
            ab_rows.append((sbk_f.reshape(b, t, SB_HEADS, SB_HD), sbv_f.reshape(b, t, SB_HEADS, SB_HD),
                            dfk_f.reshape(b, t, DF_HEADS, 2, DF_HD), dfv_f.reshape(b, t, DF_HEADS, 2 * DF_HD)))
            three = lambda a: a.reshape(b, t, a.shape[-1])
            if caches is None:
                keys = [three(a) for a in (sbk, sbv, dfk, dfv)]
            else:
                keys = [_with_cache(c[j].reshape(b, start, -1), n.reshape(b, t, -1), t_keys, BF16)
                        for c, n in zip(caches[:4], (sbk_f, sbv_f, dfk_f, dfv_f))]
            lam_init = 0.8 - 0.6 * math.exp(-0.3 * i)
            df_extra = [W['df_lambda'][j], W['df_subln'][j][None, :]]
            sb_out = _attention(
                functools.partial(_sb_kernel, tq=sb_tq, tk=sb_tk, q_start=start),
                three(sbq), keys[0], keys[1], [], q_lanes=LANES, k_lanes=LANES, tq=sb_tq, tk=sb_tk,
                name="sb_attn", v_groups=sb_groups)
            df_out = _attention(
                functools.partial(_df_kernel, tq=tq, width=width, q_range=q_range, n_keys=n_keys,
                                  lam_init=lam_init),
                three(dfq), keys[2], keys[3], df_extra,
                q_lanes=LANES, k_lanes=LANES, tq=tq, tk=width, name="df_attn", v_groups=df_groups)
            parts = [sb_out.reshape(m, SB_W), df_out.reshape(m, DF_W)]
            w_out = W['ab_out'][j]
        else:
            q, lat, kr = _mla_in_proj(x, g[2], W['mla_in'][j], W['mla_q_norm'][j][None, :],
                                      W['mla_kv_norm'][j][None, :], W['mla_uq'][j], mq_tabs, mk_tabs, tm)
            mla_rows.append((lat.reshape(b, t, KV_LORA), kr.reshape(b, t, MLA_ROPE)))
            if caches is None:
                k_all, v_all = _mla_kv_expand(lat, kr, W['mla_uk'][j], W['mla_place'], W['mla_uv'][j], tm)
                out = _attention(
                    functools.partial(_mla_kernel, tq=tq, width=width, q_range=q_range, n_keys=n_keys),
                    q.reshape(b, t, MLA_QW), k_all.reshape(b, -1, MLA_QW), v_all.reshape(b, -1, MLA_VW), [],
                    q_lanes=2 * LANES, k_lanes=2 * LANES, tq=tq, tk=width, name="mla_attn",
                    v_groups=mla_groups)
            else:
                out = _mla_cached_attention(
                    q.reshape(b, t, MLA_QW), caches[4][j], caches[5][j], lat.reshape(b, t, KV_LORA),
                    kr.reshape(b, t, MLA_ROPE), W['mla_uk'][j], W['mla_place'][:, :LANES], W['mla_uv'][j], start)
            parts = [out.reshape(m, MLA_VW)]
            w_out = W['mla_out'][j]
        x = _post_mixer(x, p.reshape(depth, m, -1), i, g[3:8], w_out, W['ffn2_gu'], W['ffn2_dn'],
                        W['ple_gate'], W['ple_in'], parts, tm, tf)
    sb_k, sb_v, df_k, df_v = [jnp.stack(r, axis=0) for r in zip(*ab_rows)]
    lat, kr = [jnp.stack(r, axis=0) for r in zip(*mla_rows)]
    return x.reshape(b, t, d), sb_k, sb_v, df_k, df_v, lat, kr


def _pick(n, candidates):
    for c in candidates:
        if n % c == 0:
            return c
    return n


def _cast_kernel(x_ref, o_ref):
    o_ref[...] = x_ref[...].astype(o_ref.dtype)


def _to_bf16(w):
    n_l, k, n = w.shape
    bk = _pick(k, (256, 128, 64, 32, 16))
    spec = pl.BlockSpec((1, bk, n), lambda l, i: (l, i, 0))
    return pl.pallas_call(
        _cast_kernel, grid=(n_l, k // bk), in_specs=[spec], out_specs=spec,
        out_shape=jax.ShapeDtypeStruct(w.shape, BF16),
        compiler_params=_cparams(("parallel", "parallel")), name="to_bf16",
    )(w)


def kernel(x_prompt, x_sample, p_prompt, p_sample, cache_sb_k, cache_sb_v, cache_df_k, cache_df_v,
           cache_mla_latent, cache_mla_krope, norms, ffn1_gu, ffn1_dn, ffn2_gu, ffn2_dn, ple_in, ple_gate,
           ab_in, ab_out, df_lambda, df_subln, mla_in, mla_q_norm, mla_kv_norm, mla_uq, mla_uk, mla_uv, mla_out):
    bf = lambda a: a.astype(BF16)
    n_odd = mla_in.shape[0]
    place = jnp.zeros((MLA_ROPE, MLA_HEADS, LANES), F32)
    place = place.at[jnp.arange(MLA_ROPE), :, MLA_NOPE + jnp.arange(MLA_ROPE)].set(1.0)
    W = dict(
        norms=norms, ffn1_gu=_to_bf16(ffn1_gu), ffn1_dn=_to_bf16(ffn1_dn), ffn2_gu=_to_bf16(ffn2_gu),
        ffn2_dn=_to_bf16(ffn2_dn), ple_in=bf(ple_in), ple_gate=_to_bf16(ple_gate), ab_in=_to_bf16(ab_in),
        ab_out=_to_bf16(ab_out),
        df_lambda=df_lambda, df_subln=df_subln,
        mla_in=bf(jnp.pad(mla_in, ((0, 0), (0, 0), (0, MLA_IN_PAD - mla_in.shape[-1])))),
        mla_q_norm=mla_q_norm, mla_kv_norm=mla_kv_norm,
        mla_uq=bf(jnp.stack([_with_rotation_partner(_pad_heads(mla_uq[j], MLA_HEADS, MLA_QD))
                             for j in range(n_odd)])),
        mla_uk=bf(jnp.stack([_pad_heads(mla_uk[j], MLA_HEADS, MLA_NOPE) for j in range(n_odd)])),
        mla_uv=bf(mla_uv), mla_out=bf(mla_out),
        mla_place=bf(place.reshape(MLA_ROPE, MLA_QW)),
    )
    d_ff = ffn1_dn.shape[1]
    tf = 256 if d_ff % 256 == 0 else d_ff

    t_p = x_prompt.shape[1]
    m_p = x_prompt.shape[0] * t_p
    width_p = _pick(t_p, (BUCKET, Q_TILE, 128, CHUNK))
    tq_p = _pick(width_p, (Q_TILE, 128, CHUNK))
    out_p = _trunk(x_prompt, p_prompt, 0, None, W, tm=_pick(m_p, (ROW_TILE, 256, 128)), tf=tf, tq=tq_p,
                   width=width_p, sb_tq=tq_p, sb_tk=tq_p, sb_groups=2, df_groups=2, mla_groups=2)

    caches = (cache_sb_k, cache_sb_v, cache_df_k, cache_df_v, cache_mla_latent, cache_mla_krope)
    past = cache_sb_k.shape[2]
    t_s = x_sample.shape[1]
    m_s = x_sample.shape[0] * t_s
    tk_s = _pick(past, (Q_TILE, 128, CHUNK))
    out_s = _trunk(x_sample, p_sample, past, caches, W, tm=_pick(m_s, (128,)), tf=tf, tq=t_s, width=tk_s,
                   sb_tq=t_s, sb_tk=tk_s, sb_groups=SB_HEADS // 2, df_groups=DF_HEADS, mla_groups=MLA_HEADS // 2)

    (y_p, sb_k_p, sb_v_p, df_k_p, df_v_p, lat_p, kr_p) = out_p
    (y_s, sb_k_s, sb_v_s, df_k_s, df_v_s, lat_s, kr_s) = out_s
    return (y_p, y_s, sb_k_p, sb_v_p, df_k_p, df_v_p, lat_p, kr_p,
            sb_k_s, sb_v_s, df_k_s, df_v_s, lat_s, kr_s)
```

```python
import functools
import math

import jax
import jax.numpy as jnp
from jax import lax
from jax.experimental import pallas as pl
from jax.experimental.pallas import tpu as pltpu

F32 = jnp.float32
BF16 = jnp.bfloat16

EPS = 1e-6
CHUNK = 64
SB_HEADS, SB_HD = 8, 64
DF_HEADS, DF_HD = 4, 64
DF_ROT = DF_HD // 4
ROPE_THETA = 500000.0
MLA_HEADS, MLA_NOPE, MLA_ROPE, MLA_VD = 16, 64, 32, 64
Q_LORA, KV_LORA = 384, 256
MLA_THETA = 10000.0
MLA_QD = MLA_NOPE + MLA_ROPE
LOG2E = math.log2(math.e)
SB_UNDERFLOW = 151.0
SB_LINEAR = 64.0
BUCKET = 512
MXU_TILE = 256
ROW_TILE = 512
Q_TILE = MXU_TILE

LANES = 128
SB_W = SB_HEADS * SB_HD
DF_W = DF_HEADS * 2 * DF_HD
MLA_QW = MLA_HEADS * LANES
MLA_VW = MLA_HEADS * MLA_VD
MLA_IN_PAD = -(-(Q_LORA + KV_LORA + MLA_ROPE) // LANES) * LANES

VMEM_LIMIT = 56 * 1024 * 1024


def _cparams(sem):
    return pltpu.CompilerParams(dimension_semantics=sem, vmem_limit_bytes=VMEM_LIMIT)


def _rms(x, g):
    return x * lax.rsqrt(jnp.mean(x * x, axis=-1, keepdims=True) + EPS) * g


def _dot(a, b):
    return jnp.dot(a, b, preferred_element_type=F32)


def _dot_t(a, b):
    return lax.dot_general(a, b, (((1,), (1,)), ((), ())), preferred_element_type=F32)


def _rope_slab(x, cos, sin_up, sin_dn, shift):
    n = x.shape[-1]
    return x * cos + pltpu.roll(x, n - shift, 1) * sin_up + pltpu.roll(x, shift, 1) * sin_dn


def _swiglu(h, wgu_ref, wd_ref, chunk):
    d_ff = wd_ref.shape[0]
    acc = None
    for c in range(d_ff // chunk):
        g = _dot(h, wgu_ref[:, c * chunk:(c + 1) * chunk])
        u = _dot(h, wgu_ref[:, d_ff + c * chunk:d_ff + (c + 1) * chunk])
        t = _dot((g * jax.nn.sigmoid(g) * u).astype(BF16), wd_ref[c * chunk:(c + 1) * chunk, :])
        acc = t if acc is None else acc + t
    return acc


def _whole(a):
    return pl.BlockSpec(a.shape, lambda *_: (0,) * a.ndim, pipeline_mode=pl.Buffered(1))


def _layer(stack, layer):
    return pl.BlockSpec((None,) + stack.shape[1:], lambda *_: (layer,) + (0,) * (stack.ndim - 1),
                        pipeline_mode=pl.Buffered(1))


def _ffn_kernel(x_ref, gin_ref, gout_ref, wgu_ref, wd_ref, o_ref, *, chunk):
    x = x_ref[...]
    h = _rms(x, gin_ref[...]).astype(BF16)
    o_ref[...] = x + 0.5 * _rms(_swiglu(h, wgu_ref, wd_ref, chunk), gout_ref[...])


def _ffn_block(x, g_in, g_out, w_gu, w_dn, layer, tm, tf):
    m, d = x.shape
    row = lambda i: (i, 0)
    return pl.pallas_call(
        functools.partial(_ffn_kernel, chunk=tf),
        grid=(m // tm,),
        in_specs=[pl.BlockSpec((tm, d), row), _whole(g_in), _whole(g_out), _layer(w_gu, layer),
                  _layer(w_dn, layer)],
        out_specs=pl.BlockSpec((tm, d), row),
        out_shape=jax.ShapeDtypeStruct((m, d), F32),
        compiler_params=_cparams(("parallel",)),
        name="ffn_block",
    )(x, g_in, g_out, w_gu, w_dn)


def _ab_in_kernel(x_ref, g_ref, w_ref, cos_ref, sup_ref, sdn_ref,
                  sbq_ref, sbk_ref, sbv_ref, dfq_ref, dfk_ref, dfv_ref,
                  sbk_f_ref, sbv_f_ref, dfk_f_ref, dfv_f_ref):
    h = _rms(x_ref[...], g_ref[...]).astype(BF16)
    cos, sup, sdn = cos_ref[...], sup_ref[...], sdn_ref[...]

    def proj(c):
        return _dot(h, w_ref[:, c * SB_W:(c + 1) * SB_W])

    def store_rows(out_f, x, first):
        for j in range(x.shape[-1] // SB_HD):
            out_f[pl.ds(first + j, x.shape[0], stride=SB_W // SB_HD), :] = x[:, j * SB_HD:(j + 1) * SB_HD]

    def rope(x, scale, out_b, out_f):
        for s in range(x.shape[-1] // LANES):
            sl = slice(s * LANES, (s + 1) * LANES)
            r = _rope_slab(x[:, sl], cos, sup, sdn, DF_ROT // 2)
            if out_f is not None:
                store_rows(out_f, r, s * (LANES // SB_HD))
            out_b[:, sl] = (r * scale).astype(BF16)

    sbq_ref[...] = (proj(0) * (SB_HD ** -0.5 * LOG2E)).astype(BF16)
    sbk = proj(1)
    store_rows(sbk_f_ref, sbk, 0)
    sbk_ref[...] = sbk.astype(BF16)
    sbv = proj(2)
    store_rows(sbv_f_ref, sbv, 0)
    sbv_ref[...] = sbv.astype(BF16)
    rope(proj(3), DF_HD ** -0.5 * LOG2E, dfq_ref, None)
    rope(proj(4), 1.0, dfk_ref, dfk_f_ref)
    dfv = proj(5)
    for j in range(DF_HEADS):
        dfv_f_ref[pl.ds(j, dfv.shape[0], stride=DF_HEADS), :] = dfv[:, j * LANES:(j + 1) * LANES]
    dfv_ref[...] = dfv.astype(BF16)


def _ab_in_proj(x, g, w_in, tabs, tm):
    m, d = x.shape
    n_tab = tabs[0].shape[0] // tm
    row = lambda i: (i, 0)
    const = lambda i: (0, 0)
    tab = lambda i: (i % n_tab, 0)
    wide = pl.BlockSpec((tm, SB_W), row)
    bf = jax.ShapeDtypeStruct((m, SB_W), BF16)
    n_rows = SB_W // SB_HD
    tall = pl.BlockSpec((tm * n_rows, SB_HD), row)
    f32_tall = jax.ShapeDtypeStruct((m * n_rows, SB_HD), F32)
    return pl.pallas_call(
        _ab_in_kernel,
        grid=(m // tm,),
        in_specs=[
            pl.BlockSpec((tm, d), row),
            pl.BlockSpec((1, d), const),
            pl.BlockSpec(w_in.shape, const),
            pl.BlockSpec((tm, LANES), tab),
            pl.BlockSpec((tm, LANES), tab),
            pl.BlockSpec((tm, LANES), tab),
        ],
        out_specs=[wide] * 6 + [tall] * 3 + [pl.BlockSpec((tm * DF_HEADS, LANES), row)],
        out_shape=[bf] * 6 + [f32_tall] * 3 + [jax.ShapeDtypeStruct((m * DF_HEADS, LANES), F32)],
        compiler_params=_cparams(("parallel",)),
        name="ab_in_proj",
    )(x, g, w_in, *tabs)


def _post_mixer_kernel(*refs, n_parts, chunk):
    (x_ref, p_ref, g_ref, wo_ref, wgu_ref, wd_ref, wgate_ref, wple_ref) = refs[:8]
    parts = refs[8:8 + n_parts]
    o_ref = refs[8 + n_parts]
    g_mix, g_in, g_out, g_gate, g_ple = (g_ref[i] for i in range(5))
    off = 0
    mix = None
    for p in parts:
        k = p.shape[-1]
        t = _dot(p[...], wo_ref[off:off + k, :])
        mix = t if mix is None else mix + t
        off += k
    x1 = x_ref[...] + _rms(mix, g_mix)
    x2 = x1 + 0.5 * _rms(_swiglu(_rms(x1, g_in).astype(BF16), wgu_ref, wd_ref, chunk), g_out)
    gate = jax.nn.sigmoid(_dot(_rms(x2, g_gate).astype(BF16), wgate_ref[...]))
    emb = _dot(p_ref[...].astype(BF16), wple_ref[...])
    o_ref[...] = x2 + _rms(gate * emb, g_ple)


def _post_mixer(x, p, layer, gains, w_out, w_gu, w_dn, w_gate, w_ple, parts, tm, tf):
    m, d = x.shape
    row = lambda i: (i, 0)
    return pl.pallas_call(
        functools.partial(_post_mixer_kernel, n_parts=len(parts), chunk=tf),
        grid=(m // tm,),
        in_specs=[
            pl.BlockSpec((tm, d), row),
            pl.BlockSpec((None, tm, p.shape[-1]), lambda i: (layer, i, 0)),
        ] + [_whole(gains), _whole(w_out)] + [_layer(a, layer) for a in (w_gu, w_dn, w_gate, w_ple)]
        + [pl.BlockSpec((tm, a.shape[-1]), row) for a in parts],
        out_specs=pl.BlockSpec((tm, d), row),
        out_shape=jax.ShapeDtypeStruct((m, d), F32),
        compiler_params=_cparams(("parallel",)),
        name="post_mixer",
    )(x, p, gains, w_out, w_gu, w_dn, w_gate, w_ple, *parts)


def _mla_in_kernel(x_ref, g_ref, w_ref, qn_ref, kvn_ref, wuq_ref,
                   qcos_ref, qsin_ref, kcos_ref, ksup_ref, ksdn_ref,
                   q_ref, lat_ref, kr_ref):
    h = _rms(x_ref[...], g_ref[...]).astype(BF16)
    proj = _dot(h, w_ref[...])
    c_q = _rms(proj[:, :Q_LORA], qn_ref[...]).astype(BF16)
    lat_ref[...] = _rms(proj[:, Q_LORA:Q_LORA + KV_LORA], kvn_ref[...])
    kr = _rope_slab(proj[:, Q_LORA + KV_LORA:], kcos_ref[...], ksup_ref[...], ksdn_ref[...],
                    MLA_ROPE // 2)
    kr_ref[...] = kr[:, :MLA_ROPE]
    q2 = _dot(c_q, wuq_ref[...])
    cos, sin = qcos_ref[...], qsin_ref[...]
    for hd in range(MLA_HEADS):
        sl = slice(hd * LANES, (hd + 1) * LANES)
        rot = slice(MLA_QW + hd * LANES, MLA_QW + (hd + 1) * LANES)
        r = q2[:, sl] * cos + q2[:, rot] * sin
        q_ref[:, sl] = (r * (MLA_QD ** -0.5 * LOG2E)).astype(BF16)


def _mla_in_proj(x, g, w_in, q_norm, kv_norm, w_uq, qtabs, ktabs, tm):
    m, d = x.shape
    n_tab = qtabs[0].shape[0] // tm
    row = lambda i: (i, 0)
    const = lambda i: (0, 0)
    tab = lambda i: (i % n_tab, 0)
    tspec = pl.BlockSpec((tm, LANES), tab)
    return pl.pallas_call(
        _mla_in_kernel,
        grid=(m // tm,),
        in_specs=[
            pl.BlockSpec((tm, d), row),
            pl.BlockSpec((1, d), const),
            pl.BlockSpec(w_in.shape, const),
            pl.BlockSpec((1, Q_LORA), const),
            pl.BlockSpec((1, KV_LORA), const),
            pl.BlockSpec(w_uq.shape, const),
        ] + [tspec] * 5,
        out_specs=[
            pl.BlockSpec((tm, MLA_QW), row),
            pl.BlockSpec((tm, KV_LORA), row),
            pl.BlockSpec((tm, MLA_ROPE), row),
        ],
        out_shape=[
            jax.ShapeDtypeStruct((m, MLA_QW), BF16),
            jax.ShapeDtypeStruct((m, KV_LORA), F32),
            jax.ShapeDtypeStruct((m, MLA_ROPE), F32),
        ],
        compiler_params=_cparams(("parallel",)),
        name="mla_in_proj",
    )(x, g, w_in, q_norm, kv_norm, w_uq, *qtabs, *ktabs)


def _mla_kv_kernel(lat_ref, kr_ref, wuk_ref, place_ref, wuv_ref, k_ref, v_ref):
    lat = lat_ref[...].astype(BF16)
    k_ref[...] = (_dot(lat, wuk_ref[...]) + _dot(kr_ref[...].astype(BF16), place_ref[...])).astype(BF16)
    v_ref[...] = _dot(lat, wuv_ref[...]).astype(BF16)


def _mla_kv_expand(lat, kr, w_uk, place, w_uv, tm):
    m = lat.shape[0]
    row = lambda i: (i, 0)
    const = lambda i: (0, 0)
    return pl.pallas_call(
        _mla_kv_kernel,
        grid=(m // tm,),
        in_specs=[
            pl.BlockSpec((tm, KV_LORA), row),
            pl.BlockSpec((tm, MLA_ROPE), row),
            pl.BlockSpec(w_uk.shape, const),
            pl.BlockSpec(place.shape, const),
            pl.BlockSpec(w_uv.shape, const),
        ],
        out_specs=[pl.BlockSpec((tm, MLA_QW), row), pl.BlockSpec((tm, MLA_VW), row)],
        out_shape=[jax.ShapeDtypeStruct((m, MLA_QW), BF16), jax.ShapeDtypeStruct((m, MLA_VW), BF16)],
        compiler_params=_cparams(("parallel",)),
        name="mla_kv_expand",
    )(lat, kr, w_uk, place, w_uv)


def _positions(q0, kb, tq, tk):
    t = q0 + lax.broadcasted_iota(jnp.int32, (tq, tk), 0)
    s = kb * tk + lax.broadcasted_iota(jnp.int32, (tq, tk), 1)
    return t, s


def _sb_phases(q_ref, k_ref, v_ref, q0, *, tq, tk):
    n_full = q0 // tk
    q = q_ref[0]
    groups = q.shape[-1] // LANES
    lane = lax.broadcasted_iota(jnp.int32, (tq, LANES), 1)
    q_heads = []
    for g in range(groups):
        qg = q[:, g * LANES:(g + 1) * LANES]
        q_heads += [jnp.where(lane < SB_HD, qg, jnp.zeros_like(qg)),
                    jnp.where(lane >= SB_HD, qg, jnp.zeros_like(qg))]
    group_of = lambda h: slice((h // 2) * LANES, (h // 2 + 1) * LANES)
    later = (lax.broadcasted_iota(jnp.int32, (tk, tk), 0)
             > lax.broadcasted_iota(jnp.int32, (tk, tk), 1)).astype(BF16)

    def load(kb):
        start = pl.multiple_of(kb * tk, tk)
        return k_ref[0, pl.ds(start, tk), :], v_ref[0, pl.ds(start, tk), :]

    def miss_cost(z):
        cost = jnp.maximum(jnp.log2(1.0 + jnp.exp2(jnp.minimum(z, SB_LINEAR))), z)
        return cost, z - cost

    def cost_after(cost):
        hi = cost.astype(BF16)
        lo = (cost - hi.astype(F32)).astype(BF16)
        return _dot(hi, later) + _dot(lo, later)

    def block_total(cost, after):
        return after[:, 0:1] + cost[:, 0:1]

    lower = jnp.maximum(n_full - 1, 0)

    def first_step():
        (k_lo, v_lo), (k_up, v_up) = load(lower), load(lower + 1)
        t, s_lo = _positions(q0, lower, tq, tk)
        earlier_lo = s_lo < t
        earlier_up = s_lo + tk < t
        state = []
        for h, qh in enumerate(q_heads):
            sl = group_of(h)
            cost_up, hit_up = miss_cost(_dot_t(qh, k_up[:, sl]))
            cost_lo, hit_lo = miss_cost(_dot_t(qh, k_lo[:, sl]))
            cost_up = jnp.where(earlier_up, cost_up, 0.0)
            cost_lo = jnp.where(earlier_lo, cost_lo, 0.0)
            after_up, after_lo = cost_after(cost_up), cost_after(cost_lo)
            run_up = block_total(cost_up, after_up)
            w_up = jnp.where(earlier_up, jnp.exp2(hit_up - after_up), 0.0)
            w_lo = jnp.where(earlier_lo, jnp.exp2(hit_lo - after_lo - run_up), 0.0)
            acc = _dot(w_up.astype(BF16), v_up[:, sl]) + _dot(w_lo.astype(BF16), v_lo[:, sl])
            state.append((run_up + block_total(cost_lo, after_lo), acc))
        return state

    def least_run(st):
        least = st[0][0]
        for run, _ in st[1:]:
            least = jnp.minimum(least, run)
        return jnp.min(least)

    def more(c):
        kb, least, _ = c
        return (kb >= 0) & (least < SB_UNDERFLOW)

    def block(c):
        kb, _, st = c
        ks, vs = load(kb)
        new = []
        for h, (qh, (run, acc)) in enumerate(zip(q_heads, st)):
            cost, hit = miss_cost(_dot_t(qh, ks[:, group_of(h)]))
            after = cost_after(cost)
            acc = acc + _dot(jnp.exp2(hit - after - run).astype(BF16), vs[:, group_of(h)])
            new.append((run + block_total(cost, after), acc))
        return kb - 1, least_run(new), tuple(new)

    def finish(state):
        _, _, state = lax.while_loop(more, block, (lower - 1, least_run(state), tuple(state)))
        return [jnp.where(lane < SB_HD, state[2 * g][1], state[2 * g + 1][1]) for g in range(groups)]

    return first_step, finish


def _sb_kernel(q_ref, k_ref, v_ref, o_ref, *, tq, tk, q_start):
    first_step, finish = _sb_phases(q_ref, k_ref, v_ref, q_start + pl.program_id(2) * tq, tq=tq, tk=tk)
    for g, out in enumerate(finish(first_step())):
        o_ref[0, :, _lane_slab(g)] = out.astype(o_ref.dtype)


def _softmax_bucketed(chains, k_ref, v_ref, q0, emit, *, tq, width, q_range, n_keys):
    bucket = q0 // width
    for j in range(q_range[0] // width, (q_range[1] - 1) // width + 1):
        @pl.when(bucket == j)
        def _(j=j):
            length = (j + 1) * width
            t_chunk = (q0 + lax.broadcasted_iota(jnp.int32, (tq, 1), 0)) // CHUNK
            s_pos = length - width + lax.broadcasted_iota(jnp.int32, (1, width), 1)
            s_chunk = jnp.where(s_pos < n_keys, s_pos // CHUNK, jnp.iinfo(jnp.int32).max)
            vis = s_chunk <= t_chunk
            outs = []
            for qc, k_sl, v_sl in chains:
                s = _dot_t(qc, k_ref[0, 0:length, k_sl])
                tail = jnp.where(vis, s[:, length - width:], -jnp.inf)
                s = tail if length == width else jnp.concatenate([s[:, :length - width], tail], axis=1)
                p = jnp.exp2(s - jnp.max(s, axis=-1, keepdims=True))
                l = jnp.sum(p, axis=-1, keepdims=True)
                outs.append(_dot(p.astype(BF16), v_ref[0, 0:length, v_sl]) / l)
            emit(outs)


def _lane_slab(i):
    return slice(i * LANES, (i + 1) * LANES)


def _df_kernel(lam_ref, sub_ref, q_ref, k_ref, v_ref, o_ref, *, tq, width, q_range, n_keys, lam_init):
    q0 = q_range[0] + pl.program_id(2) * tq
    q = q_ref[0]
    heads = q.shape[-1] // LANES
    lane = lax.broadcasted_iota(jnp.int32, (tq, LANES), 1)
    chains = []
    for h in range(heads):
        qh = q[:, _lane_slab(h)]
        chains.append((jnp.where(lane < DF_HD, qh, jnp.zeros_like(qh)), _lane_slab(h), _lane_slab(h)))
        chains.append((jnp.where(lane >= DF_HD, qh, jnp.zeros_like(qh)), _lane_slab(h), _lane_slab(h)))

    def emit(outs):
        lv = lam_ref[...]
        lam = (jnp.exp(jnp.sum(lv[0:1] * lv[1:2], axis=-1, keepdims=True))
               - jnp.exp(jnp.sum(lv[2:3] * lv[3:4], axis=-1, keepdims=True)) + lam_init)
        for h in range(heads):
            out = outs[2 * h] - lam * outs[2 * h + 1]
            o_ref[0, :, _lane_slab(h)] = (_rms(out, sub_ref[...]) * (1.0 - lam_init)).astype(o_ref.dtype)

    _softmax_bucketed(chains, k_ref, v_ref, q0, emit, tq=tq, width=width, q_range=q_range, n_keys=n_keys)


def _mla_kernel(q_ref, k_ref, v_ref, o_ref, *, tq, width, q_range, n_keys):
    q0 = q_range[0] + pl.program_id(2) * tq
    q = q_ref[0]
    heads = q.shape[-1] // LANES
    lane = lax.broadcasted_iota(jnp.int32, (tq, LANES), 1)
    chains = [(q[:, _lane_slab(h)], _lane_slab(h), _lane_slab(h // 2)) for h in range(heads)]

    def emit(outs):
        for g in range(heads // 2):
            o_ref[0, :, _lane_slab(g)] = jnp.where(
                lane < MLA_VD, outs[2 * g], outs[2 * g + 1]).astype(o_ref.dtype)

    _softmax_bucketed(chains, k_ref, v_ref, q0, emit, tq=tq, width=width, q_range=q_range, n_keys=n_keys)


def _mla_cached_kernel(q_ref, latc_ref, krc_ref, latn_ref, krn_ref, wuk_ref, place_ref, wuv_ref, o_ref,
                       *, q_start):
    q = q_ref[0]
    t = q.shape[0]
    slabs = [slice(h * LANES, (h + 1) * LANES) for h in range(MLA_HEADS)]
    q_all = jnp.concatenate([q[:, sl] for sl in slabs], axis=0)
    q_lat = jnp.concatenate([_dot_t(q[:, sl], wuk_ref[:, sl]) for sl in slabs], axis=0).astype(BF16)
    rows = MLA_HEADS * t
    pad = LANES - t

    def scores(lat, kr):
        kr_lanes = _dot(kr.astype(BF16), place_ref[...]).astype(BF16)
        return _dot_t(q_lat, lat) + _dot_t(q_all, kr_lanes)

    lat_c = latc_ref[0].astype(BF16)
    s_c = scores(lat_c, krc_ref[0])
    lat_n = jnp.concatenate([latn_ref[0], jnp.zeros((pad, KV_LORA), F32)], axis=0).astype(BF16)
    kr_n = jnp.concatenate([krn_ref[0], jnp.zeros((pad, MLA_ROPE), F32)], axis=0)
    t_pos = q_start + lax.broadcasted_iota(jnp.int32, (rows, LANES), 0) % t
    j = lax.broadcasted_iota(jnp.int32, (rows, LANES), 1)
    vis = (j < t) & (((q_start + j) // CHUNK) <= (t_pos // CHUNK))
    s_n = jnp.where(vis, scores(lat_n, kr_n), -jnp.inf)

    m = jnp.maximum(jnp.max(s_c, axis=-1, keepdims=True), jnp.max(s_n, axis=-1, keepdims=True))
    p_c, p_n = jnp.exp2(s_c - m), jnp.exp2(s_n - m)
    l = jnp.sum(p_c, axis=-1, keepdims=True) + jnp.sum(p_n, axis=-1, keepdims=True)
    o_lat = (_dot(p_c.astype(BF16), lat_c) + _dot(p_n.astype(BF16), lat_n)) / l
    full = _dot(o_lat.astype(BF16), wuv_ref[...])
    lane_head = lax.broadcasted_iota(jnp.int32, (t, MLA_VW), 1) // MLA_VD
    out = jnp.zeros((t, MLA_VW), F32)
    for h in range(MLA_HEADS):
        out = out + jnp.where(lane_head == h, full[h * t:(h + 1) * t, :], 0.0)
    o_ref[0] = out.astype(o_ref.dtype)


def _mla_cached_attention(q, lat_cache, kr_cache, lat_new, kr_new, w_uk, place, w_uv, q_start):
    b, t, _ = q.shape
    past = lat_cache.shape[1]
    assert t <= LANES and q_start == past
    per_b = lambda bi: (bi, 0, 0)
    const = lambda bi: (0, 0)
    return pl.pallas_call(
        functools.partial(_mla_cached_kernel, q_start=q_start),
        grid=(b,),
        in_specs=[
            pl.BlockSpec((1, t, MLA_QW), per_b),
            pl.BlockSpec((1, past, KV_LORA), per_b),
            pl.BlockSpec((1, past, MLA_ROPE), per_b),
            pl.BlockSpec((1, t, KV_LORA), per_b),
            pl.BlockSpec((1, t, MLA_ROPE), per_b),
            pl.BlockSpec(w_uk.shape, const),
            pl.BlockSpec(place.shape, const),
            pl.BlockSpec(w_uv.shape, const),
        ],
        out_specs=pl.BlockSpec((1, t, MLA_VW), per_b),
        out_shape=jax.ShapeDtypeStruct((b, t, MLA_VW), BF16),
        compiler_params=_cparams(("parallel",)),
        name="mla_cached_attn",
    )(q, lat_cache, kr_cache, lat_new, kr_new, w_uk, place, w_uv)


def _attention(kernel_fn, q, k, v, extra, *, q_lanes, k_lanes, tq, tk, name, v_groups=1):
    b, t_q, _ = q.shape
    t_k = k.shape[1]
    groups = v.shape[-1] // (LANES * v_groups)
    q_lanes, k_lanes, v_lanes = q_lanes * v_groups, k_lanes * v_groups, LANES * v_groups
    assert t_q % tq == 0 and t_k % tk == 0 and tk % tq == 0 and tk % CHUNK == 0
    const = lambda bi, g, qi: (0, 0)
    return pl.pallas_call(
        kernel_fn,
        grid=(b, groups, t_q // tq),
        in_specs=[pl.BlockSpec(e.shape, const) for e in extra] + [
            pl.BlockSpec((1, tq, q_lanes), lambda bi, g, qi: (bi, qi, g)),
            pl.BlockSpec((1, t_k, k_lanes), lambda bi, g, qi: (bi, 0, g)),
            pl.BlockSpec((1, t_k, v_lanes), lambda bi, g, qi: (bi, 0, g)),
        ],
        out_specs=pl.BlockSpec((1, tq, v_lanes), lambda bi, g, qi: (bi, qi, g)),
        out_shape=jax.ShapeDtypeStruct((b, t_q, groups * v_lanes), BF16),
        compiler_params=_cparams(("parallel", "parallel", "arbitrary")),
        name=name,
    )(*extra, q, k, v)


def _rope_tables(pos, rot, theta, period, offset):
    half = rot // 2
    inv = jnp.float32(theta) ** (-(jnp.arange(half, dtype=F32) * (2.0 / rot)))
    ang = pos.astype(F32)[:, None] * inv[None, :]
    cos, sin = jnp.cos(ang), jnp.sin(ang)
    lane = jnp.arange(LANES) % period - offset
    first = (lane >= 0) & (lane < half)
    second = (lane >= half) & (lane < rot)
    idx = jnp.clip(jnp.where(second, lane - half, lane), 0, half - 1)
    cos_l, sin_l = cos[:, idx], sin[:, idx]
    cos_t = jnp.where(first | second, cos_l, 1.0)
    sin_up = jnp.where(first, -sin_l, 0.0)
    sin_dn = jnp.where(second, sin_l, 0.0)
    return cos_t, sin_up, sin_dn


def _pad_heads(w, heads, width):
    k = w.shape[0]
    w = w.reshape(k, heads, width)
    return jnp.pad(w, ((0, 0), (0, 0), (0, LANES - width))).reshape(k, heads * LANES)


def _with_rotation_partner(w):
    k = w.shape[0]
    half = MLA_ROPE // 2
    w3 = w.reshape(k, MLA_HEADS, LANES)
    x1 = w3[..., MLA_NOPE:MLA_NOPE + half]
    x2 = w3[..., MLA_NOPE + half:MLA_NOPE + MLA_ROPE]
    rot = jnp.concatenate([jnp.zeros_like(w3[..., :MLA_NOPE]), -x2, x1,
                           jnp.zeros_like(w3[..., MLA_QD:])], axis=-1)
    return jnp.concatenate([w, rot.reshape(k, MLA_HEADS * LANES)], axis=-1)


def _with_cache(cache, new, t_pad, dtype):
    b, past, w = cache.shape
    pad = jnp.zeros((b, t_pad - past - new.shape[1], w), dtype)
    return jnp.concatenate([cache.astype(dtype), new.astype(dtype), pad], axis=1)


def _trunk(x, p, start, caches, W, *, tm, tf, tq, width, sb_tq, sb_tk, sb_groups, df_groups, mla_groups):
    b, t, d = x.shape
    m = b * t
    depth = p.shape[0]
    x = x.reshape(m, d)
    pos = start + jnp.arange(t)
    if t % tm == 0:
        tab_pos = pos
    else:
        tab_pos = jnp.tile(pos, m // t)
    df_tabs = _rope_tables(tab_pos, DF_ROT, ROPE_THETA, DF_HD, 0)
    q_cos, q_sin_up, q_sin_dn = _rope_tables(tab_pos, MLA_ROPE, MLA_THETA, LANES, MLA_NOPE)
    mq_tabs = (q_cos, q_sin_dn - q_sin_up)
    mk_tabs = _rope_tables(tab_pos, MLA_ROPE, MLA_THETA, LANES, 0)
    n_keys = start + t
    t_keys = -(-n_keys // width) * width
    assert t_keys % sb_tk == 0 and t_keys >= 2 * sb_tk and sb_tq <= sb_tk
    q_range = (start, start + t)

    ab_rows, mla_rows = [], []
    for i in range(depth):
        g = W['norms'][i][:, None, :]
        j = i // 2
        x = _ffn_block(x, g[0], g[1], W['ffn1_gu'], W['ffn1_dn'], i, tm, tf)
        if i % 2 == 0:
            (sbq, sbk, sbv, dfq, dfk, dfv, sbk_f, sbv_f, dfk_f, dfv_f) = _ab_in_proj(
                x, g[2], W['ab_in'][j], df_tabs, tm)
            ab_rows.append((sbk_f.reshape(b, t, SB_HEADS, SB_HD), sbv_f.reshape(b, t, SB_HEADS, SB_HD),
                            dfk_f.reshape(b, t, DF_HEADS, 2, DF_HD), dfv_f.reshape(b, t, DF_HEADS, 2 * DF_HD)))
            three = lambda a: a.reshape(b, t, a.shape[-1])
            if caches is None:
                keys = [three(a) for a in (sbk, sbv, dfk, dfv)]
            else:
                keys = [_with_cache(c[j].reshape(b, start, -1), n.reshape(b, t, -1), t_keys, BF16)
                        for c, n in zip(caches[:4], (sbk_f, sbv_f, dfk_f, dfv_f))]
            lam_init = 0.8 - 0.6 * math.exp(-0.3 * i)
            df_extra = [W['df_lambda'][j], W['df_subln'][j][None, :]]
            sb_out = _attention(
                functools.partial(_sb_kernel, tq=sb_tq, tk=sb_tk, q_start=start),
                three(sbq), keys[0], keys[1], [], q_lanes=LANES, k_lanes=LANES, tq=sb_tq, tk=sb_tk,
                name="sb_attn", v_groups=sb_groups)
            df_out = _attention(
                functools.partial(_df_kernel, tq=tq, width=width, q_range=q_range, n_keys=n_keys,
                                  lam_init=lam_init),
                three(dfq), keys[2], keys[3], df_extra,
                q_lanes=LANES, k_lanes=LANES, tq=tq, tk=width, name="df_attn", v_groups=df_groups)
            parts = [sb_out.reshape(m, SB_W), df_out.reshape(m, DF_W)]
            w_out = W['ab_out'][j]
        else:
            q, lat, kr = _mla_in_proj(x, g[2], W['mla_in'][j], W['mla_q_norm'][j][None, :],
                                      W['mla_kv_norm'][j][None, :], W['mla_uq'][j], mq_tabs, mk_tabs, tm)
            mla_rows.append((lat.reshape(b, t, KV_LORA), kr.reshape(b, t, MLA_ROPE)))
            if caches is None:
                k_all, v_all = _mla_kv_expand(lat, kr, W['mla_uk'][j], W['mla_place'], W['mla_uv'][j], tm)
                out = _attention(
                    functools.partial(_mla_kernel, tq=tq, width=width, q_range=q_range, n_keys=n_keys),
                    q.reshape(b, t, MLA_QW), k_all.reshape(b, -1, MLA_QW), v_all.reshape(b, -1, MLA_VW), [],
                    q_lanes=2 * LANES, k_lanes=2 * LANES, tq=tq, tk=width, name="mla_attn",
                    v_groups=mla_groups)
            else:
                out = _mla_cached_attention(
                    q.reshape(b, t, MLA_QW), caches[4][j], caches[5][j], lat.reshape(b, t, KV_LORA),
                    kr.reshape(b, t, MLA_ROPE), W['mla_uk'][j], W['mla_place'][:, :LANES], W['mla_uv'][j], start)
            parts = [out.reshape(m, MLA_VW)]
            w_out = W['mla_out'][j]
        x = _post_mixer(x, p.reshape(depth, m, -1), i, g[3:8], w_out, W['ffn2_gu'], W['ffn2_dn'],
                        W['ple_gate'], W['ple_in'], parts, tm, tf)
    sb_k, sb_v, df_k, df_v = [jnp.stack(r, axis=0) for r in zip(*ab_rows)]
    lat, kr = [jnp.stack(r, axis=0) for r in zip(*mla_rows)]
    return x.reshape(b, t, d), sb_k, sb_v, df_k, df_v, lat, kr


def _pick(n, candidates):
    for c in candidates:
        if n % c == 0:
            return c
    return n


def _cast_kernel(x_ref, o_ref):
    o_ref[...] = x_ref[...].astype(o_ref.dtype)


def _to_bf16(w):
    n_l, k, n = w.shape
    bk = _pick(k, (256, 128, 64, 32, 16))
    spec = pl.BlockSpec((1, bk, n), lambda l, i: (l, i, 0))
    return pl.pallas_call(
        _cast_kernel, grid=(n_l, k // bk), in_specs=[spec], out_specs=spec,
        out_shape=jax.ShapeDtypeStruct(w.shape, BF16),
        compiler_params=_cparams(("parallel", "parallel")), name="to_bf16",
    )(w)


def kernel(x_prompt, x_sample, p_prompt, p_sample, cache_sb_k, cache_sb_v, cache_df_k, cache_df_v,
           cache_mla_latent, cache_mla_krope, norms, ffn1_gu, ffn1_dn, ffn2_gu, ffn2_dn, ple_in, ple_gate,
           ab_in, ab_out, df_lambda, df_subln, mla_in, mla_q_norm, mla_kv_norm, mla_uq, mla_uk, mla_uv, mla_out):
    bf = lambda a: a.astype(BF16)
    n_odd = mla_in.shape[0]
    place = jnp.zeros((MLA_ROPE, MLA_HEADS, LANES), F32)
    place = place.at[jnp.arange(MLA_ROPE), :, MLA_NOPE + jnp.arange(MLA_ROPE)].set(1.0)
    W = dict(
        norms=norms, ffn1_gu=_to_bf16(ffn1_gu), ffn1_dn=_to_bf16(ffn1_dn), ffn2_gu=_to_bf16(ffn2_gu),
        ffn2_dn=_to_bf16(ffn2_dn), ple_in=bf(ple_in), ple_gate=_to_bf16(ple_gate), ab_in=_to_bf16(ab_in),
        ab_out=_to_bf16(ab_out),
        df_lambda=df_lambda, df_subln=df_subln,
        mla_in=bf(jnp.pad(mla_in, ((0, 0), (0, 0), (0, MLA_IN_PAD - mla_in.shape[-1])))),
        mla_q_norm=mla_q_norm, mla_kv_norm=mla_kv_norm,
        mla_uq=bf(jnp.stack([_with_rotation_partner(_pad_heads(mla_uq[j], MLA_HEADS, MLA_QD))
                             for j in range(n_odd)])),
        mla_uk=bf(jnp.stack([_pad_heads(mla_uk[j], MLA_HEADS, MLA_NOPE) for j in range(n_odd)])),
        mla_uv=bf(mla_uv), mla_out=bf(mla_out),
        mla_place=bf(place.reshape(MLA_ROPE, MLA_QW)),
    )
    d_ff = ffn1_dn.shape[1]
    tf = MXU_TILE if d_ff % MXU_TILE == 0 else d_ff

    t_p = x_prompt.shape[1]
    m_p = x_prompt.shape[0] * t_p
    width_p = _pick(t_p, (BUCKET, Q_TILE, 128, CHUNK))
    tq_p = _pick(width_p, (Q_TILE, 128, CHUNK))
    out_p = _trunk(x_prompt, p_prompt, 0, None, W, tm=_pick(m_p, (ROW_TILE, 256, 128)), tf=tf, tq=tq_p,
                   width=width_p, sb_tq=tq_p, sb_tk=tq_p, sb_groups=2, df_groups=2, mla_groups=2)

    caches = (cache_sb_k, cache_sb_v, cache_df_k, cache_df_v, cache_mla_latent, cache_mla_krope)
    past = cache_sb_k.shape[2]
    t_s = x_sample.shape[1]
    m_s = x_sample.shape[0] * t_s
    tk_s = _pick(past, (Q_TILE, 128, CHUNK))
    out_s = _trunk(x_sample, p_sample, past, caches, W, tm=_pick(m_s, (128,)), tf=tf, tq=t_s, width=tk_s,
                   sb_tq=t_s, sb_tk=tk_s, sb_groups=SB_HEADS // 2, df_groups=DF_HEADS, mla_groups=MLA_HEADS // 2)

    (y_p, sb_k_p, sb_v_p, df_k_p, df_v_p, lat_p, kr_p) = out_p
    (y_s, sb_k_s, sb_v_s, df_k_s, df_v_s, lat_s, kr_s) = out_s
    return (y_p, y_s, sb_k_p, sb_v_p, df_k_p, df_v_p, lat_p, kr_p,
            sb_k_s, sb_v_s, df_k_s, df_v_s, lat_s, kr_s)
```
